```python
import math
import numpy as np
import jax
import jax.numpy as jnp
from jax import lax


D_MODEL = 2048
BATCH = 1
SEQ = 16384
DEPTH = 2

HEAD_DIM = 128
N_HEADS_NSA = 8
N_KV_NSA = 2
N_HEADS_MOBA = 8
NSA_WIDTH = N_HEADS_NSA * HEAD_DIM
NSA_KV_WIDTH = N_KV_NSA * HEAD_DIM
MOBA_WIDTH = N_HEADS_MOBA * HEAD_DIM
MIX_WIDTH = NSA_WIDTH + MOBA_WIDTH
IN_SPLITS = [NSA_WIDTH] + [NSA_KV_WIDTH] * 6 + [3 * N_HEADS_NSA] + [MOBA_WIDTH] * 3
IN_WIDTH = sum(IN_SPLITS)
ROPE_THETA = 10000.0
CMP_LEN = 32
CMP_STRIDE = 16
CMP_HIDDEN = 256
SLC_BLOCK = 64
SLC_TOPN = 16
WINDOW = 512
MOBA_BLOCK = 256
MOBA_TOPK = 3
Q_CHUNK = 64
N_EXPERTS = 32
N_GROUPS = 4
EXPERTS_PER_GROUP = N_EXPERTS // N_GROUPS
TOP_K_EXPERTS = 2
D_EXPERT = 512
MOE_BLOCK = 128
RMS_EPS = 1e-6
NEG_INF = -1e30
FORCE = 1e9
F32 = jnp.float32

kernel_name = 'hybrid_nsa_moba_grouped_moe_adaln'


def rms_norm(x, g):
    x32 = x.astype(F32)
    y = x32 * lax.rsqrt(jnp.mean(x32 * x32, axis=-1, keepdims=True) + RMS_EPS)
    return (y * g.astype(F32)).astype(x.dtype)


def rope_tables(pos):
    inv = 1.0 / (ROPE_THETA ** (jnp.arange(0, HEAD_DIM, 2, dtype=F32) / HEAD_DIM))
    ang = pos.astype(F32)[:, None] * inv[None, :]
    ang = jnp.concatenate([ang, ang], axis=-1)
    return jnp.cos(ang), jnp.sin(ang)


def apply_rope(x, cos, sin):
    x32 = x.astype(F32)
    x1, x2 = jnp.split(x32, 2, axis=-1)
    rot = jnp.concatenate([-x2, x1], axis=-1)
    return (x32 * cos[:, None, :] + rot * sin[:, None, :]).astype(x.dtype)


def masked_softmax(s, mask):
    s = jnp.where(mask, s.astype(F32), NEG_INF)
    p = jax.nn.softmax(s, axis=-1)
    return jnp.where(mask, p, 0.0)


def compress(raw, pe, w1, b1, w2):
    S, G, Dh = raw.shape
    ch = raw.reshape(S // CMP_STRIDE, CMP_STRIDE, G, Dh)
    blocks = jnp.concatenate([ch[:-1], ch[1:]], axis=1) + pe[None, :, None, :]
    flat = blocks.transpose(0, 2, 1, 3).reshape(blocks.shape[0], G, CMP_LEN * Dh)
    return jax.nn.gelu(flat @ w1 + b1) @ w2


def nsa_attention(q, k_cmp, v_cmp, k_slc, v_slc, k_win, v_win, gates):
    S = q.shape[0]
    G, R = N_KV_NSA, N_HEADS_NSA // N_KV_NSA
    nc = k_cmp.shape[0]
    nsb = S // SLC_BLOCK
    topn = min(SLC_TOPN, nsb)
    scale = HEAD_DIM ** -0.5
    cmp_start = jnp.arange(nc) * CMP_STRIDE
    cmp_end = cmp_start + CMP_LEN - 1
    slc_start = jnp.arange(nsb) * SLC_BLOCK
    overlap = ((cmp_end[:, None] >= slc_start[None, :]) &
               (cmp_start[:, None] < slc_start[None, :] + SLC_BLOCK)).astype(F32)
    kb = k_slc.reshape(nsb, SLC_BLOCK, G, HEAD_DIM).transpose(2, 0, 1, 3)
    vb = v_slc.reshape(nsb, SLC_BLOCK, G, HEAD_DIM).transpose(2, 0, 1, 3)
    kw = jnp.pad(k_win, ((WINDOW, 0), (0, 0), (0, 0)))
    vw = jnp.pad(v_win, ((WINDOW, 0), (0, 0), (0, 0)))
    g_idx = jnp.arange(G)[None, :, None]

    def chunk(ci):
        c0 = ci * Q_CHUNK
        t = c0 + jnp.arange(Q_CHUNK)
        qc = lax.dynamic_slice_in_dim(q, c0, Q_CHUNK, 0).reshape(Q_CHUNK, G, R, HEAD_DIM) * scale
        gc = jax.nn.sigmoid(lax.dynamic_slice_in_dim(gates, c0, Q_CHUNK, 0).astype(F32)).reshape(Q_CHUNK, G, R, 3)
        s = jnp.einsum('qgrd,cgd->qgrc', qc, k_cmp)
        m = (cmp_end[None, :] <= t[:, None])[:, None, None, :]
        p_cmp = masked_softmax(s, m)
        o_cmp = jnp.einsum('qgrc,cgd->qgrd', p_cmp.astype(v_cmp.dtype), v_cmp)
        imp = jnp.einsum('qgrc,cj->qgj', p_cmp, overlap)
        jblk = jnp.arange(nsb)[None, None, :]
        tb = (t // SLC_BLOCK)[:, None, None]
        valid = jblk <= tb
        forced = (jblk == 0) | (jblk == tb) | (jblk == tb - 1)
        imp = jnp.where(valid, jnp.where(forced, FORCE, imp), NEG_INF)
        _, idx = lax.top_k(imp, topn)
        ks = kb[g_idx, idx].reshape(Q_CHUNK, G, topn * SLC_BLOCK, HEAD_DIM)
        vs = vb[g_idx, idx].reshape(Q_CHUNK, G, topn * SLC_BLOCK, HEAD_DIM)
        s = jnp.einsum('qgrd,qgmd->qgrm', qc, ks)
        kpos = (idx[..., None] * SLC_BLOCK + jnp.arange(SLC_BLOCK)).reshape(Q_CHUNK, G, 1, topn * SLC_BLOCK)
        p = masked_softmax(s, kpos <= t[:, None, None, None])
        o_slc = jnp.einsum('qgrm,qgmd->qgrd', p.astype(vs.dtype), vs)
        kwc = lax.dynamic_slice_in_dim(kw, c0, WINDOW + Q_CHUNK, 0)
        vwc = lax.dynamic_slice_in_dim(vw, c0, WINDOW + Q_CHUNK, 0)
        s = jnp.einsum('qgrd,kgd->qgrk', qc, kwc)
        wpos = c0 - WINDOW + jnp.arange(WINDOW + Q_CHUNK)
        m = ((wpos[None, :] <= t[:, None]) & (wpos[None, :] > t[:, None] - WINDOW) & (wpos[None, :] >= 0))[:, None, None, :]
        p = masked_softmax(s, m)
        o_win = jnp.einsum('qgrk,kgd->qgrd', p.astype(vwc.dtype), vwc)
        o = gc[..., 0:1] * o_cmp + gc[..., 1:2] * o_slc + gc[..., 2:3] * o_win
        return o.reshape(Q_CHUNK, NSA_WIDTH).astype(q.dtype)

    return lax.map(chunk, jnp.arange(S // Q_CHUNK)).reshape(S, NSA_WIDTH)


def moba_attention(q, k, v):
    S = q.shape[0]
    H = N_HEADS_MOBA
    nb = -(-S // MOBA_BLOCK)
    topk = min(MOBA_TOPK, nb)
    pad = nb * MOBA_BLOCK - S
    kp = jnp.pad(k, ((0, pad), (0, 0), (0, 0)))
    vp = jnp.pad(v, ((0, pad), (0, 0), (0, 0)))
    kblk = kp.reshape(nb, MOBA_BLOCK, H, HEAD_DIM)
    kmean = jnp.mean(kblk.astype(F32), axis=1).astype(k.dtype)
    kb = kblk.transpose(2, 0, 1, 3)
    vb = vp.reshape(nb, MOBA_BLOCK, H, HEAD_DIM).transpose(2, 0, 1, 3)
    h_idx = jnp.arange(H)[None, :, None]
    scale = HEAD_DIM ** -0.5

    def chunk(ci):
        c0 = ci * Q_CHUNK
        t = c0 + jnp.arange(Q_CHUNK)
        own = c0 // MOBA_BLOCK
        qc = lax.dynamic_slice_in_dim(q, c0, Q_CHUNK, 0) * scale
        gate = jnp.einsum('qhd,nhd->qhn', qc, kmean).astype(F32)
        past = (jnp.arange(nb) < own)[None, None, :]
        _, idx = lax.top_k(jnp.where(past, gate, NEG_INF), topk)
        sel_ok = jnp.broadcast_to((idx < own)[..., None], (Q_CHUNK, H, topk, MOBA_BLOCK)).reshape(Q_CHUNK, H, topk * MOBA_BLOCK)
        ks = kb[h_idx, idx].reshape(Q_CHUNK, H, topk * MOBA_BLOCK, HEAD_DIM)
        vs = vb[h_idx, idx].reshape(Q_CHUNK, H, topk * MOBA_BLOCK, HEAD_DIM)
        s_sel = jnp.einsum('qhd,qhmd->qhm', qc, ks)
        k_own = lax.dynamic_slice_in_dim(kp, own * MOBA_BLOCK, MOBA_BLOCK, 0)
        v_own = lax.dynamic_slice_in_dim(vp, own * MOBA_BLOCK, MOBA_BLOCK, 0)
        s_own = jnp.einsum('qhd,khd->qhk', qc, k_own)
        opos = own * MOBA_BLOCK + jnp.arange(MOBA_BLOCK)
        m_own = jnp.broadcast_to((opos[None, :] <= t[:, None])[:, None, :], s_own.shape)
        p = masked_softmax(jnp.concatenate([s_sel, s_own], axis=-1), jnp.concatenate([sel_ok, m_own], axis=-1))
        p = p.astype(v.dtype)
        n_sel = topk * MOBA_BLOCK
        o = jnp.einsum('qhm,qhmd->qhd', p[..., :n_sel], vs) + jnp.einsum('qhk,khd->qhd', p[..., n_sel:], v_own)
        return o.reshape(Q_CHUNK, MOBA_WIDTH)

    return lax.map(chunk, jnp.arange(S // Q_CHUNK)).reshape(S, MOBA_WIDTH)


def hybrid_mixer(h, w_in, w_out, cmp_pe, cmp_w1, cmp_b1, cmp_w2, cos, sin, cos_c, sin_c):
    B, S, _ = h.shape
    proj = h @ w_in
    offs = list(np.cumsum(IN_SPLITS)[:-1])
    q_n, kc, vc, ks, vs, kw, vw, gts, q_m, k_m, v_m = jnp.split(proj, offs, axis=-1)
    kvshape = (B, S, N_KV_NSA, HEAD_DIM)
    q_n = apply_rope(q_n.reshape(B, S, N_HEADS_NSA, HEAD_DIM), cos, sin)
    ks = apply_rope(ks.reshape(kvshape), cos, sin)
    kw = apply_rope(kw.reshape(kvshape), cos, sin)
    cmp = jax.vmap(compress, in_axes=(0, None, None, None, None))
    k_cmp = apply_rope(cmp(kc.reshape(kvshape), cmp_pe[0], cmp_w1[0], cmp_b1[0], cmp_w2[0]), cos_c, sin_c)
    v_cmp = cmp(vc.reshape(kvshape), cmp_pe[1], cmp_w1[1], cmp_b1[1], cmp_w2[1])
    o_nsa = jax.vmap(nsa_attention)(q_n, k_cmp, v_cmp, ks, vs.reshape(kvshape), kw, vw.reshape(kvshape),
                                    gts.reshape(B, S, N_HEADS_NSA, 3))
    mshape = (B, S, N_HEADS_MOBA, HEAD_DIM)
    o_moba = jax.vmap(moba_attention)(apply_rope(q_m.reshape(mshape), cos, sin),
                                      apply_rope(k_m.reshape(mshape), cos, sin), v_m.reshape(mshape))
    return jnp.concatenate([o_nsa, o_moba], axis=-1) @ w_out


def moe_ffn(h, w_router, b_router, w_gate, w_up, w_down):
    B, S, D = h.shape
    n = B * S
    hf = h.reshape(n, D)
    scores = jax.nn.sigmoid((hf @ w_router).astype(F32))
    biased = (scores + b_router.astype(F32)).reshape(n, N_GROUPS, EXPERTS_PER_GROUP)
    grp = jnp.argmax(lax.top_k(biased, 2)[0].sum(-1), axis=-1)
    in_grp = jnp.take_along_axis(biased, grp[:, None, None], axis=1)[:, 0]
    _, local = lax.top_k(in_grp, TOP_K_EXPERTS)
    expert = grp[:, None] * EXPERTS_PER_GROUP + local
    wsel = jnp.take_along_axis(scores, expert, axis=1)
    wsel = wsel / jnp.sum(wsel, axis=-1, keepdims=True)
    n_pair = n * TOP_K_EXPERTS
    flat_e = expert.reshape(-1)
    flat_t = jnp.repeat(jnp.arange(n, dtype=jnp.int32), TOP_K_EXPERTS)
    order = jnp.argsort(flat_e)
    se, st, sw = flat_e[order], flat_t[order], wsel.reshape(-1)[order]
    counts = jnp.bincount(flat_e, length=N_EXPERTS)
    starts = jnp.cumsum(counts) - counts
    padded = (counts + MOE_BLOCK - 1) // MOE_BLOCK * MOE_BLOCK
    pend = jnp.cumsum(padded)
    pstart = pend - padded
    slot = pstart[se] + jnp.arange(n_pair) - starts[se]
    n_slots = n_pair + N_EXPERTS * MOE_BLOCK
    slot_tok = jnp.zeros((n_slots,), jnp.int32).at[slot].set(st)
    slot_w = jnp.zeros((n_slots,), F32).at[slot].set(sw)
    n_blk = n_slots // MOE_BLOCK
    blk_exp = jnp.minimum(jnp.searchsorted(pend, jnp.arange(n_blk) * MOE_BLOCK, side='right'), N_EXPERTS - 1)

    def run_block(b):
        tok = lax.dynamic_slice_in_dim(slot_tok, b * MOE_BLOCK, MOE_BLOCK)
        wt = lax.dynamic_slice_in_dim(slot_w, b * MOE_BLOCK, MOE_BLOCK)
        e = blk_exp[b]
        xb = hf[tok]
        y = (jax.nn.silu(xb @ w_gate[e]) * (xb @ w_up[e])) @ w_down[e]
        return (y.astype(F32) * wt[:, None]).astype(h.dtype)

    ys = lax.map(run_block, jnp.arange(n_blk))
    out = jnp.zeros((n, D), h.dtype).at[slot_tok].add(ys.reshape(n_slots, D))
    return out.reshape(B, S, D)


def setup_inputs(seed: int = 0) -> dict:
    key = jax.random.key(seed)
    k = jax.random.split(key, 20)

    def nrm(kk, shape, s):
        return jax.random.normal(kk, shape, F32) * s

    return {
        'x': nrm(k[0], (BATCH, SEQ, D_MODEL), 1.0),
        'c': nrm(k[1], (BATCH, D_MODEL), 1.0),
        'w_ada': nrm(k[2], (DEPTH, D_MODEL, 6 * D_MODEL), 0.5 * D_MODEL ** -0.5),
        'b_ada': nrm(k[3], (DEPTH, 6 * D_MODEL), 0.01),
        'g_attn': 1.0 + nrm(k[4], (DEPTH, D_MODEL), 0.02),
        'g_ffn': 1.0 + nrm(k[5], (DEPTH, D_MODEL), 0.02),
        'w_in': nrm(k[6], (DEPTH, D_MODEL, IN_WIDTH), D_MODEL ** -0.5),
        'w_out': nrm(k[7], (DEPTH, MIX_WIDTH, D_MODEL), MIX_WIDTH ** -0.5),
        'cmp_pe': nrm(k[8], (DEPTH, 2, CMP_LEN, HEAD_DIM), 0.1),
        'cmp_w1': nrm(k[9], (DEPTH, 2, CMP_LEN * HEAD_DIM, CMP_HIDDEN), (CMP_LEN * HEAD_DIM) ** -0.5),
        'cmp_b1': nrm(k[10], (DEPTH, 2, CMP_HIDDEN), 0.01),
        'cmp_w2': nrm(k[11], (DEPTH, 2, CMP_HIDDEN, HEAD_DIM), CMP_HIDDEN ** -0.5),
        'w_router': nrm(k[12], (D_MODEL, N_EXPERTS), D_MODEL ** -0.5),
        'b_router': nrm(k[13], (N_EXPERTS,), 0.01),
        'w_gate': nrm(k[14], (DEPTH, N_EXPERTS, D_MODEL, D_EXPERT), D_MODEL ** -0.5),
        'w_up': nrm(k[15], (DEPTH, N_EXPERTS, D_MODEL, D_EXPERT), D_MODEL ** -0.5),
        'w_down': nrm(k[16], (DEPTH, N_EXPERTS, D_EXPERT, D_MODEL), D_EXPERT ** -0.5),
        'g_final': 1.0 + nrm(k[17], (D_MODEL,), 0.02),
    }


def reference(x, c, w_ada, b_ada, g_attn, g_ffn, w_in, w_out, cmp_pe, cmp_w1, cmp_b1, cmp_w2,
              w_router, b_router, w_gate, w_up, w_down, g_final):
    B, S, D = x.shape
    cos, sin = rope_tables(jnp.arange(S))
    nc = S // CMP_STRIDE - 1
    cos_c, sin_c = rope_tables(jnp.arange(nc) * CMP_STRIDE + CMP_LEN - 1)
    cond = jax.nn.silu(c)
    for l in range(DEPTH):
        mod = (cond @ w_ada[l] + b_ada[l])[:, None, :]
        sh1, sc1, gt1, sh2, sc2, gt2 = jnp.split(mod, 6, axis=-1)
        h = rms_norm(x, g_attn[l]) * (1.0 + sc1) + sh1
        x = x + gt1 * hybrid_mixer(h, w_in[l], w_out[l], cmp_pe[l], cmp_w1[l], cmp_b1[l], cmp_w2[l],
                                   cos, sin, cos_c, sin_c)
        h = rms_norm(x, g_ffn[l]) * (1.0 + sc2) + sh2
        x = x + gt2 * moe_ffn(h, w_router, b_router, w_gate[l], w_up[l], w_down[l])
    return rms_norm(x, g_final)
```

```python
import functools

import jax
import jax.numpy as jnp
from jax import lax
from jax.experimental import pallas as pl
from jax.experimental.pallas import tpu as pltpu

F32 = jnp.float32
BF16 = jnp.bfloat16

D_MODEL = 2048
HEAD_DIM = 128
N_HEADS_NSA = 8
N_KV_NSA = 2
NSA_REP = N_HEADS_NSA // N_KV_NSA
N_HEADS_MOBA = 8
NSA_WIDTH = N_HEADS_NSA * HEAD_DIM
NSA_KV_WIDTH = N_KV_NSA * HEAD_DIM
MOBA_WIDTH = N_HEADS_MOBA * HEAD_DIM
N_GATES = 3 * N_HEADS_NSA
ROPE_THETA = 10000.0
CMP_LEN = 32
CMP_STRIDE = 16
CMP_HIDDEN = 256
SLC_BLOCK = 64
SLC_TOPN = 16
WINDOW = 512
MOBA_BLOCK = 256
MOBA_TOPK = 3
N_EXPERTS = 32
N_GROUPS = 4
EXPERTS_PER_GROUP = N_EXPERTS // N_GROUPS
TOP_K_EXPERTS = 2
D_EXPERT = 512
RMS_EPS = 1e-6
NEG_INF = -1e30
FORCE = 1e9
LOWEST = -3e38
SCALE = HEAD_DIM ** -0.5

LANE = 128
Q_TILE = 256
SLC_KEY_TILE = 512
MOE_ROWS = 256
GATHER_ROWS = 256
PROJ_TM = 1024
PROJ_TN = 512
OUT_TM = 256
VMEM_LIMIT = 56 * 1024 * 1024

COL_QN = 0
COL_KC = 8
COL_VC = 10
COL_KS = 12
COL_VS = 14
COL_KW = 16
COL_VW = 18
COL_QM = 20
COL_KM = 28
COL_VM = 36
PROJ_COLS = 44
ROPE_FLAGS = tuple(
    1 if (c < 8 or c in (12, 13, 16, 17) or 20 <= c < 36) else 0 for c in range(PROJ_COLS))


def _cparams(sem):
    return pltpu.CompilerParams(dimension_semantics=sem, vmem_limit_bytes=VMEM_LIMIT)


def _dot(a, b):
    return jnp.dot(a, b, preferred_element_type=F32)


def _dot_nt(a, b):
    return lax.dot_general(a, b, (((1,), (1,)), ((), ())), preferred_element_type=F32)


def _split_bf16(v):
    hi = v.astype(BF16)
    lo = (v - hi.astype(F32)).astype(BF16)
    return hi, lo


def _mod_kernel(c_ref, w_ref, b_ref, o_ref):
    c = c_ref[...]
    cond = c * jax.nn.sigmoid(c)
    o_ref[0] = jnp.sum(w_ref[0] * cond, axis=0, keepdims=True) + b_ref[0]


def _ada_mod(c, w_ada, b_ada):
    depth, d, n = w_ada.shape
    tn = 1024
    return pl.pallas_call(
        _mod_kernel,
        grid=(depth, n // tn),
        in_specs=[
            pl.BlockSpec((d, 1), lambda l, j: (0, 0)),
            pl.BlockSpec((1, d, tn), lambda l, j: (l, 0, j)),
            pl.BlockSpec((1, 1, tn), lambda l, j: (l, 0, j)),
        ],
        out_specs=pl.BlockSpec((1, 1, tn), lambda l, j: (l, 0, j)),
        out_shape=jax.ShapeDtypeStruct((depth, 1, n), F32),
        compiler_params=_cparams(("arbitrary", "arbitrary")),
        name="ada_mod",
    )(c.reshape(d, 1), w_ada, b_ada.reshape(depth, 1, n))


def _rms_mod(x, g, sc, sh):
    y = x * lax.rsqrt(jnp.mean(x * x, axis=-1, keepdims=True) + RMS_EPS)
    return (y * g) * (1.0 + sc) + sh


def _proj_kernel(flags_ref, x_ref, g_ref, sc_ref, sh_ref, w_ref, wg_ref, cos_ref, sin_ref,
                 o_ref, gate_ref, h_scr):
    j = pl.program_id(1)

    @pl.when(j == 0)
    def _():
        hb = _rms_mod(x_ref[...], g_ref[...], sc_ref[...], sh_ref[...]).astype(BF16)
        h_scr[...] = hb
        gate_ref[...] = _dot(hb, wg_ref[...])

    acc = _dot(h_scr[...], w_ref[...])
    cos = cos_ref[...]
    sin = sin_ref[...]
    nch = PROJ_TN // LANE
    for c in range(nch):
        a = acc[:, c * LANE:(c + 1) * LANE]
        roped = a * cos + pltpu.roll(a, HEAD_DIM // 2, 1) * sin
        f = flags_ref[j * nch + c]
        o_ref[:, c * LANE:(c + 1) * LANE] = jnp.where(f > 0, roped, a).astype(BF16)


def _project(x, g, sc, sh, w_main, w_gate, cos, sin, flags):
    s, d = x.shape
    n = w_main.shape[1]
    tm = min(PROJ_TM, s)
    grid_spec = pltpu.PrefetchScalarGridSpec(
        num_scalar_prefetch=1,
        grid=(s // tm, n // PROJ_TN),
        in_specs=[
            pl.BlockSpec((tm, d), lambda i, j, f: (i, 0)),
            pl.BlockSpec((1, d), lambda i, j, f: (0, 0)),
            pl.BlockSpec((1, d), lambda i, j, f: (0, 0)),
            pl.BlockSpec((1, d), lambda i, j, f: (0, 0)),
            pl.BlockSpec((d, PROJ_TN), lambda i, j, f: (0, j)),
            pl.BlockSpec((d, 2 * LANE), lambda i, j, f: (0, 0)),
            pl.BlockSpec((tm, LANE), lambda i, j, f: (i, 0)),
            pl.BlockSpec((tm, LANE), lambda i, j, f: (i, 0)),
        ],
        out_specs=[
            pl.BlockSpec((tm, PROJ_TN), lambda i, j, f: (i, j)),
            pl.BlockSpec((tm, 2 * LANE), lambda i, j, f: (i, 0)),
        ],
        scratch_shapes=[pltpu.VMEM((tm, d), BF16)],
    )
    return pl.pallas_call(
        _proj_kernel,
        grid_spec=grid_spec,
        out_shape=[jax.ShapeDtypeStruct((s, n), BF16), jax.ShapeDtypeStruct((s, 2 * LANE), F32)],
        compiler_params=_cparams(("arbitrary", "arbitrary")),
        name="norm_proj_rope",
    )(flags, x, g, sc, sh, w_main, w_gate, cos, sin)


def _compress_kernel(ch_ref, w1_ref, b1_ref, w2_ref, pe_ref, cos_ref, sin_ref, o_ref):
    a = pl.program_id(0)
    ch = ch_ref[0]
    nc = ch.shape[0]
    half = CMP_STRIDE * HEAD_DIM
    w1 = w1_ref[0].astype(BF16)
    top = _dot(ch, w1[:half])
    bot = _dot(ch, w1[half:])
    bot = pltpu.roll(bot, nc - 1, 0)
    pe_term = _dot(pe_ref[0].astype(BF16), w1)[0:1]
    hid = jax.nn.gelu(top + bot + pe_term + b1_ref[0])
    out = _dot(hid.astype(BF16), w2_ref[0].astype(BF16))
    roped = out * cos_ref[...] + pltpu.roll(out, HEAD_DIM // 2, 1) * sin_ref[...]
    out = jnp.where(a < N_KV_NSA, roped, out)
    row = lax.broadcasted_iota(jnp.int32, out.shape, 0)
    o_ref[0] = jnp.where(row < nc - 1, out, 0.0).astype(BF16)


def _compress(chunks, w1, b1, w2, pe8, cos_c, sin_c):
    na, nc, kw = chunks.shape
    return pl.pallas_call(
        _compress_kernel,
        grid=(na,),
        in_specs=[
            pl.BlockSpec((1, nc, kw), lambda a: (a, 0, 0)),
            pl.BlockSpec((1, CMP_LEN * HEAD_DIM, CMP_HIDDEN), lambda a: (a // N_KV_NSA, 0, 0)),
            pl.BlockSpec((1, 1, CMP_HIDDEN), lambda a: (a // N_KV_NSA, 0, 0)),
            pl.BlockSpec((1, CMP_HIDDEN, HEAD_DIM), lambda a: (a // N_KV_NSA, 0, 0)),
            pl.BlockSpec((1, 8, CMP_LEN * HEAD_DIM), lambda a: (a // N_KV_NSA, 0, 0)),
            pl.BlockSpec((nc, HEAD_DIM), lambda a: (0, 0)),
            pl.BlockSpec((nc, HEAD_DIM), lambda a: (0, 0)),
        ],
        out_specs=pl.BlockSpec((1, nc, HEAD_DIM), lambda a: (a, 0, 0)),
        out_shape=jax.ShapeDtypeStruct((na, nc, HEAD_DIM), BF16),
        compiler_params=_cparams(("arbitrary",)),
        name="nsa_compress",
    )(chunks, w1, b1, w2, pe8, cos_c, sin_c)


def _nsa_cmp_kernel(q_ref, kc_ref, vc_ref, ov_ref, gate_ref, ocmp_ref, sel_ref):
    i = pl.program_id(1)
    tq = q_ref.shape[0]
    ncp = kc_ref.shape[1]
    nsb = sel_ref.shape[2]
    t = i * tq + lax.broadcasted_iota(jnp.int32, (tq, 1), 0)
    cend = lax.broadcasted_iota(jnp.int32, (1, ncp), 1) * CMP_STRIDE + (CMP_LEN - 1)
    mask = cend <= t
    kc = kc_ref[0]
    vc = vc_ref[0]
    gates = jax.nn.sigmoid(gate_ref[...])
    psum = jnp.zeros((tq, ncp), F32)
    for r in range(NSA_REP):
        q = q_ref[:, r * HEAD_DIM:(r + 1) * HEAD_DIM]
        s = jnp.where(mask, _dot_nt(q, kc) * SCALE, NEG_INF)
        m = jnp.max(s, axis=-1, keepdims=True)
        e = jnp.where(mask, jnp.exp(s - m), 0.0)
        l = jnp.sum(e, axis=-1, keepdims=True)
        p = e / jnp.where(l > 0.0, l, 1.0)
        psum = psum + p
        o = _dot(p.astype(BF16), vc)
        ocmp_ref[:, r * HEAD_DIM:(r + 1) * HEAD_DIM] = gates[:, 3 * r:3 * r + 1] * o

    p_hi, p_lo = _split_bf16(psum)
    ov = ov_ref[...]
    imp = _dot(p_hi, ov) + _dot(p_lo, ov)
    jblk = lax.broadcasted_iota(jnp.int32, (1, nsb), 1)
    tb = t // SLC_BLOCK
    forced = (jblk == 0) | (jblk == tb) | (jblk == tb - 1)
    work = jnp.where(jblk <= tb, jnp.where(forced, FORCE, imp), NEG_INF)
    jf = jblk.astype(F32)
    sel = jnp.zeros((tq, nsb), F32)
    for _ in range(min(SLC_TOPN, nsb)):
        m = jnp.max(work, axis=-1, keepdims=True)
        first = jnp.min(jnp.where(work == m, jf, 1e9), axis=-1, keepdims=True)
        hit = jf == first
        sel = jnp.where(hit, 1.0, sel)
        work = jnp.where(hit, LOWEST, work)
    sel_ref[0] = sel.astype(BF16)


def _nsa_cmp(proj, cmp_kv, overlap, gates):
    s = proj.shape[0]
    ncp = cmp_kv.shape[1]
    nsb = overlap.shape[1]
    tq = min(Q_TILE, s)
    gw = NSA_REP * HEAD_DIM
    return pl.pallas_call(
        _nsa_cmp_kernel,
        grid=(N_KV_NSA, s // tq),
        in_specs=[
            pl.BlockSpec((tq, gw), lambda g, i: (i, g)),
            pl.BlockSpec((1, ncp, HEAD_DIM), lambda g, i: (g, 0, 0)),
            pl.BlockSpec((1, ncp, HEAD_DIM), lambda g, i: (N_KV_NSA + g, 0, 0)),
            pl.BlockSpec((ncp, nsb), lambda g, i: (0, 0)),
            pl.BlockSpec((tq, LANE), lambda g, i: (i, g)),
        ],
        out_specs=[
            pl.BlockSpec((tq, gw), lambda g, i: (i, g)),
            pl.BlockSpec((1, tq, nsb), lambda g, i: (g, i, 0)),
        ],
        out_shape=[jax.ShapeDtypeStruct((s, NSA_WIDTH), F32),
                   jax.ShapeDtypeStruct((N_KV_NSA, s, nsb), BF16)],
        compiler_params=_cparams(("arbitrary", "arbitrary")),
        name="nsa_cmp_select",
    )(proj, cmp_kv, cmp_kv, overlap, gates)


def _nsa_slc_kernel(q_ref, ks_ref, vs_ref, kw0_ref, kw1_ref, kw2_ref, vw0_ref, vw1_ref, vw2_ref,
                    sel_ref, gate_ref, ocmp_ref, o_ref, m_scr, l_scr, acc_scr):
    i = pl.program_id(1)
    tq = q_ref.shape[0]
    nsb = sel_ref.shape[2]
    tk = min(SLC_KEY_TILE, ks_ref.shape[0])
    bpt = tk // SLC_BLOCK
    t = i * tq + lax.broadcasted_iota(jnp.int32, (tq, 1), 0)
    sel = sel_ref[0]

    m_scr[...] = jnp.full(m_scr.shape, NEG_INF, F32)
    l_scr[...] = jnp.zeros(l_scr.shape, F32)
    acc_scr[...] = jnp.zeros(acc_scr.shape, F32)

    def key_tile(kt, carry):
        k0 = pl.multiple_of(kt * tk, tk)
        k = ks_ref[pl.ds(k0, tk), :]
        v = vs_ref[pl.ds(k0, tk), :]
        jrow = lax.broadcasted_iota(jnp.int32, (nsb, tk), 0)
        jcol = kt * bpt + lax.broadcasted_iota(jnp.int32, (nsb, tk), 1) // SLC_BLOCK
        expand = (jrow == jcol).astype(BF16)
        kpos = k0 + lax.broadcasted_iota(jnp.int32, (1, tk), 1)
        valid = (_dot(sel, expand) > 0.5) & (kpos <= t)
        for r in range(NSA_REP):
            q = q_ref[:, r * HEAD_DIM:(r + 1) * HEAD_DIM]
            s = jnp.where(valid, _dot_nt(q, k) * SCALE, NEG_INF)
            m_old = m_scr[r]
            m_new = jnp.maximum(m_old, jnp.max(s, axis=-1, keepdims=True))
            p = jnp.where(valid, jnp.exp(s - m_new), 0.0)
            alpha = jnp.exp(m_old - m_new)
            l_scr[r] = alpha * l_scr[r] + jnp.sum(p, axis=-1, keepdims=True)
            acc_scr[r] = alpha * acc_scr[r] + _dot(p.astype(BF16), v)
            m_scr[r] = m_new
        return carry

    n_kt = (i * tq + tq + tk - 1) // tk
    lax.fori_loop(0, n_kt, key_tile, 0)

    kwin = jnp.concatenate([kw0_ref[...], kw1_ref[...], kw2_ref[...]], axis=0)
    vwin = jnp.concatenate([vw0_ref[...], vw1_ref[...], vw2_ref[...]], axis=0)
    nw = kwin.shape[0]
    wpos = (i - 2) * tq + lax.broadcasted_iota(jnp.int32, (1, nw), 1)
    wvalid = (wpos <= t) & (wpos > t - WINDOW) & (wpos >= 0)
    gates = jax.nn.sigmoid(gate_ref[...])
    for r in range(NSA_REP):
        q = q_ref[:, r * HEAD_DIM:(r + 1) * HEAD_DIM]
        s = jnp.where(wvalid, _dot_nt(q, kwin) * SCALE, NEG_INF)
        m = jnp.max(s, axis=-1, keepdims=True)
        e = jnp.where(wvalid, jnp.exp(s - m), 0.0)
        o_win = _dot(e.astype(BF16), vwin) / jnp.sum(e, axis=-1, keepdims=True)
        o_slc = acc_scr[r] / l_scr[r]
        o = (ocmp_ref[:, r * HEAD_DIM:(r + 1) * HEAD_DIM]
             + gates[:, 3 * r + 1:3 * r + 2] * o_slc + gates[:, 3 * r + 2:3 * r + 3] * o_win)
        o_ref[:, r * HEAD_DIM:(r + 1) * HEAD_DIM] = o.astype(BF16)


def _nsa_slc(proj, sel, gates, ocmp):
    s = proj.shape[0]
    nsb = sel.shape[2]
    tq = min(Q_TILE, s)
    gw = NSA_REP * HEAD_DIM

    def win_spec(col, d):
        return pl.BlockSpec((tq, HEAD_DIM), lambda g, i: (jnp.maximum(i - 2 + d, 0), col + g))

    return pl.pallas_call(
        _nsa_slc_kernel,
        grid=(N_KV_NSA, s // tq),
        in_specs=[
            pl.BlockSpec((tq, gw), lambda g, i: (i, g)),
            pl.BlockSpec((s, HEAD_DIM), lambda g, i: (0, COL_KS + g)),
            pl.BlockSpec((s, HEAD_DIM), lambda g, i: (0, COL_VS + g)),
            win_spec(COL_KW, 0), win_spec(COL_KW, 1), win_spec(COL_KW, 2),
            win_spec(COL_VW, 0), win_spec(COL_VW, 1), win_spec(COL_VW, 2),
            pl.BlockSpec((1, tq, nsb), lambda g, i: (g, i, 0)),
            pl.BlockSpec((tq, LANE), lambda g, i: (i, g)),
            pl.BlockSpec((tq, gw), lambda g, i: (i, g)),
        ],
        out_specs=pl.BlockSpec((tq, gw), lambda g, i: (i, g)),
        out_shape=jax.ShapeDtypeStruct((s, NSA_WIDTH), BF16),
        scratch_shapes=[pltpu.VMEM((NSA_REP, tq, 1), F32), pltpu.VMEM((NSA_REP, tq, 1), F32),
                        pltpu.VMEM((NSA_REP, tq, HEAD_DIM), F32)],
        compiler_params=_cparams(("arbitrary", "arbitrary")),
        name="nsa_slc_win",
    )(proj, proj, proj, proj, proj, proj, proj, proj, proj, sel, gates, ocmp)


def _moba_kernel(q_ref, k_ref, v_ref, o_ref, km_scr, m_scr, l_scr, acc_scr):
    i = pl.program_id(1)
    tq = q_ref.shape[0]
    s_len = k_ref.shape[0]
    nb = s_len // MOBA_BLOCK
    nbp = km_scr.shape[0]

    @pl.when(i == 0)
    def _():
        km_scr[...] = jnp.zeros(km_scr.shape, F32)

        def blk_mean(n, carry):
            kb = k_ref[pl.ds(pl.multiple_of(n * MOBA_BLOCK, MOBA_BLOCK), MOBA_BLOCK), :]
            km_scr[pl.ds(n, 1), :] = jnp.mean(kb.astype(F32), axis=0, keepdims=True)
            return carry

        lax.fori_loop(0, nb, blk_mean, 0)

    q = q_ref[...]
    km_hi, km_lo = _split_bf16(km_scr[...])
    gate = _dot_nt(q, km_hi) + _dot_nt(q, km_lo)
    nidx = lax.broadcasted_iota(jnp.int32, (1, nbp), 1)
    nf = nidx.astype(F32)
    work = jnp.where(nidx < i, gate, LOWEST)
    sel = jnp.zeros((tq, nbp), F32)
    for _ in range(MOBA_TOPK):
        m = jnp.max(work, axis=-1, keepdims=True)
        first = jnp.min(jnp.where((work == m) & (m > 0.5 * LOWEST), nf, 1e9), axis=-1, keepdims=True)
        hit = nf == first
        sel = jnp.where(hit, 1.0, sel)
        work = jnp.where(hit, LOWEST, work)
    sel = sel.astype(BF16)

    m_scr[...] = jnp.full(m_scr.shape, NEG_INF, F32)
    l_scr[...] = jnp.zeros(l_scr.shape, F32)
    acc_scr[...] = jnp.zeros(acc_scr.shape, F32)

    def attend(k, v, valid):
        s = jnp.where(valid, _dot_nt(q, k) * SCALE, NEG_INF)
        m_old = m_scr[...]
        m_new = jnp.maximum(m_old, jnp.max(s, axis=-1, keepdims=True))
        p = jnp.where(valid, jnp.exp(s - m_new), 0.0)
        alpha = jnp.exp(m_old - m_new)
        l_scr[...] = alpha * l_scr[...] + jnp.sum(p, axis=-1, keepdims=True)
        acc_scr[...] = alpha * acc_scr[...] + _dot(p.astype(BF16), v)
        m_scr[...] = m_new

    def past_block(n, carry):
        k0 = pl.multiple_of(n * MOBA_BLOCK, MOBA_BLOCK)
        onehot = (lax.broadcasted_iota(jnp.int32, (nbp, LANE), 0) == n).astype(BF16)
        chosen = _dot(sel, onehot) > 0.5
        valid = jnp.concatenate([chosen] * (MOBA_BLOCK // LANE), axis=1)
        attend(k_ref[pl.ds(k0, MOBA_BLOCK), :], v_ref[pl.ds(k0, MOBA_BLOCK), :], valid)
        return carry

    lax.fori_loop(0, i, past_block, 0)

    k0 = pl.multiple_of(i * MOBA_BLOCK, MOBA_BLOCK)
    t = lax.broadcasted_iota(jnp.int32, (tq, 1), 0)
    kpos = lax.broadcasted_iota(jnp.int32, (1, MOBA_BLOCK), 1)
    attend(k_ref[pl.ds(k0, MOBA_BLOCK), :], v_ref[pl.ds(k0, MOBA_BLOCK), :], kpos <= t)
    o_ref[...] = (acc_scr[...] / l_scr[...]).astype(BF16)


def _moba(proj):
    s = proj.shape[0]
    tq = MOBA_BLOCK
    nbp = max(s // MOBA_BLOCK, LANE)
    return pl.pallas_call(
        _moba_kernel,
        grid=(N_HEADS_MOBA, s // tq),
        in_specs=[
            pl.BlockSpec((tq, HEAD_DIM), lambda h, i: (i, COL_QM + h)),
            pl.BlockSpec((s, HEAD_DIM), lambda h, i: (0, COL_KM + h)),
            pl.BlockSpec((s, HEAD_DIM), lambda h, i: (0, COL_VM + h)),
        ],
        out_specs=pl.BlockSpec((tq, HEAD_DIM), lambda h, i: (i, h)),
        out_shape=jax.ShapeDtypeStruct((s, MOBA_WIDTH), BF16),
        scratch_shapes=[pltpu.VMEM((nbp, HEAD_DIM), F32), pltpu.VMEM((tq, 1), F32),
                        pltpu.VMEM((tq, 1), F32), pltpu.VMEM((tq, HEAD_DIM), F32)],
        compiler_params=_cparams(("arbitrary", "arbitrary")),
        name="moba",
    )(proj, proj, proj)


def _out_kernel(on_ref, om_ref, x_ref, wn_ref, wm_ref, gt_ref, g_ref, sc_ref, sh_ref, wrh_ref, wrl_ref,
                x1_ref, h_ref, lg_ref):
    mix = _dot(on_ref[...], wn_ref[...]) + _dot(om_ref[...], wm_ref[...])
    x1 = x_ref[...] + gt_ref[...] * mix
    x1_ref[...] = x1
    h = _rms_mod(x1, g_ref[...], sc_ref[...], sh_ref[...])
    h_hi, h_lo = _split_bf16(h)
    h_ref[...] = h_hi
    wrh = wrh_ref[...]
    lg_ref[...] = _dot(h_hi, wrh) + _dot(h_lo, wrh) + _dot(h_hi, wrl_ref[...])


def _out_proj(o_nsa, o_moba, x, w_n, w_m, gt, g, sc, sh, wr_hi, wr_lo):
    s, d = x.shape
    tm = min(OUT_TM, s)
    row = lambda i: (i, 0)
    fix = lambda i: (0, 0)
    return pl.pallas_call(
        _out_kernel,
        grid=(s // tm,),
        in_specs=[
            pl.BlockSpec((tm, NSA_WIDTH), row),
            pl.BlockSpec((tm, MOBA_WIDTH), row),
            pl.BlockSpec((tm, d), row),
            pl.BlockSpec((NSA_WIDTH, d), fix),
            pl.BlockSpec((MOBA_WIDTH, d), fix),
            pl.BlockSpec((1, d), fix),
            pl.BlockSpec((1, d), fix),
            pl.BlockSpec((1, d), fix),
            pl.BlockSpec((1, d), fix),
            pl.BlockSpec((d, LANE), fix),
            pl.BlockSpec((d, LANE), fix),
        ],
        out_specs=[pl.BlockSpec((tm, d), row), pl.BlockSpec((tm, d), row), pl.BlockSpec((tm, LANE), row)],
        out_shape=[jax.ShapeDtypeStruct((s, d), F32), jax.ShapeDtypeStruct((s, d), BF16),
                   jax.ShapeDtypeStruct((s, LANE), F32)],
        compiler_params=_cparams(("arbitrary",)),
        name="out_proj_norm_router",
    )(o_nsa, o_moba, x, w_n, w_m, gt, g, sc, sh, wr_hi, wr_lo)


def _row_copy(src_ref, dst_ref, src_row, dst_row, sem):
    return pltpu.make_async_copy(src_ref.at[pl.ds(src_row, 1)], dst_ref.at[pl.ds(dst_row, 1)], sem)


def _gather_kernel(idx_ref, src_ref, dst_ref, sem):
    base = pl.program_id(0) * GATHER_ROWS

    def start(r, carry):
        _row_copy(src_ref, dst_ref, idx_ref[0, 0, r], base + r, sem).start()
        return carry

    def wait(r, carry):
        _row_copy(src_ref, dst_ref, idx_ref[0, 0, r], base + r, sem).wait()
        return carry

    lax.fori_loop(0, GATHER_ROWS, start, 0)
    lax.fori_loop(0, GATHER_ROWS, wait, 0)


def _gather_rows(src, idx):
    n = idx.shape[0]
    nchunk = n // GATHER_ROWS
    return pl.pallas_call(
        _gather_kernel,
        grid=(nchunk,),
        in_specs=[
            pl.BlockSpec((1, 1, GATHER_ROWS), lambda b: (b, 0, 0), memory_space=pltpu.SMEM),
            pl.BlockSpec(memory_space=pl.ANY),
        ],
        out_specs=pl.BlockSpec(memory_space=pl.ANY),
        out_shape=jax.ShapeDtypeStruct((n, src.shape[1]), src.dtype),
        scratch_shapes=[pltpu.SemaphoreType.DMA(())],
        compiler_params=_cparams(("arbitrary",)),
        name="moe_gather_rows",
    )(idx.reshape(nchunk, 1, GATHER_ROWS), src)


def _expert_kernel(be_ref, nu_ref, x_ref, wg_ref, wu_ref, wd_ref, sw_ref, y_ref, wg_s, wu_s, wd_s):
    b = pl.program_id(0)
    prev = be_ref[jnp.maximum(b - 1, 0)]

    @pl.when((b == 0) | (be_ref[b] != prev))
    def _():
        wg_s[...] = wg_ref[0, 0].astype(BF16)
        wu_s[...] = wu_ref[0, 0].astype(BF16)
        wd_s[...] = wd_ref[0, 0].astype(BF16)

    @pl.when(b < nu_ref[0])
    def _():
        x = x_ref[...]
        a = _dot(x, wg_s[...])
        u = _dot(x, wu_s[...])
        hid = (a * jax.nn.sigmoid(a) * u).astype(BF16)
        y_ref[...] = _dot(hid, wd_s[...]) * sw_ref[...]

    @pl.when(b >= nu_ref[0])
    def _():
        y_ref[...] = jnp.zeros(y_ref.shape, F32)


def _experts(xs, blk_exp, n_used, w_gate, w_up, w_down, slot_w, layer):
    n_slots, d = xs.shape
    n_blk = n_slots // MOE_ROWS
    grid_spec = pltpu.PrefetchScalarGridSpec(
        num_scalar_prefetch=2,
        grid=(n_blk,),
        in_specs=[
            pl.BlockSpec((MOE_ROWS, d), lambda b, be, nu: (b, 0)),
            pl.BlockSpec((1, 1, d, D_EXPERT), lambda b, be, nu: (layer, be[b], 0, 0)),
            pl.BlockSpec((1, 1, d, D_EXPERT), lambda b, be, nu: (layer, be[b], 0, 0)),
            pl.BlockSpec((1, 1, D_EXPERT, d), lambda b, be, nu: (layer, be[b], 0, 0)),
            pl.BlockSpec((MOE_ROWS, 1), lambda b, be, nu: (b, 0)),
        ],
        out_specs=pl.BlockSpec((MOE_ROWS, d), lambda b, be, nu: (b, 0)),
        scratch_shapes=[pltpu.VMEM((d, D_EXPERT), BF16), pltpu.VMEM((d, D_EXPERT), BF16),
                        pltpu.VMEM((D_EXPERT, d), BF16)],
    )
    return pl.pallas_call(
        _expert_kernel,
        grid_spec=grid_spec,
        out_shape=jax.ShapeDtypeStruct((n_slots, d), F32),
        compiler_params=_cparams(("arbitrary",)),
        name="moe_experts",
    )(blk_exp, n_used, xs, w_gate, w_up, w_down, slot_w.reshape(n_slots, 1))


def _combine_kernel(sa_ref, sb_ref, x_ref, gt_ref, gf_ref, y_ref, o_ref, buf_a, buf_b, sem_a, sem_b, *, final):
    tt = x_ref.shape[0]

    def start(r, carry):
        _row_copy(y_ref, buf_a, sa_ref[0, 0, r], r, sem_a).start()
        _row_copy(y_ref, buf_b, sb_ref[0, 0, r], r, sem_b).start()
        return carry

    def wait(r, carry):
        _row_copy(y_ref, buf_a, sa_ref[0, 0, r], r, sem_a).wait()
        _row_copy(y_ref, buf_b, sb_ref[0, 0, r], r, sem_b).wait()
        return carry

    lax.fori_loop(0, tt, start, 0)
    lax.fori_loop(0, tt, wait, 0)
    x2 = x_ref[...] + gt_ref[...] * (buf_a[...] + buf_b[...])
    if final:
        x2 = x2 * lax.rsqrt(jnp.mean(x2 * x2, axis=-1, keepdims=True) + RMS_EPS) * gf_ref[...]
    o_ref[...] = x2


def _combine(x1, ys, slot_a, slot_b, gt, g_final, final):
    s, d = x1.shape
    tt = min(GATHER_ROWS, s)
    nt = s // tt
    idx_spec = pl.BlockSpec((1, 1, tt), lambda b: (b, 0, 0), memory_space=pltpu.SMEM)
    return pl.pallas_call(
        functools.partial(_combine_kernel, final=final),
        grid=(nt,),
        in_specs=[
            idx_spec, idx_spec,
            pl.BlockSpec((tt, d), lambda b: (b, 0)),
            pl.BlockSpec((1, d), lambda b: (0, 0)),
            pl.BlockSpec((1, d), lambda b: (0, 0)),
            pl.BlockSpec(memory_space=pl.ANY),
        ],
        out_specs=pl.BlockSpec((tt, d), lambda b: (b, 0)),
        out_shape=jax.ShapeDtypeStruct((s, d), F32),
        scratch_shapes=[pltpu.VMEM((tt, d), F32), pltpu.VMEM((tt, d), F32),
                        pltpu.SemaphoreType.DMA(()), pltpu.SemaphoreType.DMA(())],
        compiler_params=_cparams(("arbitrary",)),
        name="moe_combine",
    )(slot_a.reshape(nt, 1, tt), slot_b.reshape(nt, 1, tt), x1, gt, g_final, ys)


def _route(logits, b_router):
    n = logits.shape[0]
    scores = jax.nn.sigmoid(logits[:, :N_EXPERTS])
    biased = (scores + b_router.astype(F32)).reshape(n, N_GROUPS, EXPERTS_PER_GROUP)
    grp = jnp.argmax(lax.top_k(biased, 2)[0].sum(-1), axis=-1)
    in_grp = jnp.take_along_axis(biased, grp[:, None, None], axis=1)[:, 0]
    _, local = lax.top_k(in_grp, TOP_K_EXPERTS)
    expert = (grp[:, None] * EXPERTS_PER_GROUP + local).astype(jnp.int32)
    wsel = jnp.take_along_axis(scores, expert, axis=1)
    wsel = wsel / jnp.sum(wsel, axis=-1, keepdims=True)

    n_pair = n * TOP_K_EXPERTS
    flat_e = expert.reshape(-1)
    flat_t = jnp.repeat(jnp.arange(n, dtype=jnp.int32), TOP_K_EXPERTS)
    order = jnp.argsort(flat_e).astype(jnp.int32)
    se, st, sw = flat_e[order], flat_t[order], wsel.reshape(-1)[order]
    counts = jnp.bincount(flat_e, length=N_EXPERTS).astype(jnp.int32)
    starts = jnp.cumsum(counts) - counts
    padded = (counts + MOE_ROWS - 1) // MOE_ROWS * MOE_ROWS
    pend = jnp.cumsum(padded)
    pstart = pend - padded
    slot = (pstart[se] + jnp.arange(n_pair, dtype=jnp.int32) - starts[se]).astype(jnp.int32)
    n_slots = n_pair + N_EXPERTS * MOE_ROWS
    slot_tok = jnp.zeros((n_slots,), jnp.int32).at[slot].set(st)
    slot_w = jnp.zeros((n_slots,), F32).at[slot].set(sw)
    pair_slot = jnp.zeros((n_pair,), jnp.int32).at[order].set(slot).reshape(n, TOP_K_EXPERTS)
    n_blk = n_slots // MOE_ROWS
    blk_exp = jnp.minimum(jnp.searchsorted(pend, jnp.arange(n_blk, dtype=jnp.int32) * MOE_ROWS, side='right'),
                          N_EXPERTS - 1).astype(jnp.int32)
    n_used = (pend[-1:] // MOE_ROWS).astype(jnp.int32)
    return slot_tok, slot_w, pair_slot, blk_exp, n_used


def _rope_tables(pos):
    inv = 1.0 / (ROPE_THETA ** (jnp.arange(0, HEAD_DIM, 2, dtype=F32) / HEAD_DIM))
    ang = pos.astype(F32)[:, None] * inv[None, :]
    cos = jnp.cos(ang)
    sin = jnp.sin(ang)
    return jnp.concatenate([cos, cos], axis=-1), jnp.concatenate([-sin, sin], axis=-1)


def _layout_w_in(w_in):
    o_g = NSA_WIDTH + 6 * NSA_KV_WIDTH
    main = jnp.concatenate([w_in[:, :o_g], w_in[:, o_g + N_GATES:]], axis=1).astype(BF16)
    wg = w_in[:, o_g:o_g + N_GATES].reshape(-1, N_KV_NSA, 3 * NSA_REP)
    wg = jnp.pad(wg, ((0, 0), (0, 0), (0, LANE - 3 * NSA_REP))).reshape(-1, N_KV_NSA * LANE)
    return main, wg.astype(BF16)


def _overlap(ncp, nsb):
    cs = jnp.arange(ncp, dtype=jnp.int32)[:, None] * CMP_STRIDE
    ss = jnp.arange(nsb, dtype=jnp.int32)[None, :] * SLC_BLOCK
    return ((cs + CMP_LEN - 1 >= ss) & (cs < ss + SLC_BLOCK)).astype(BF16)


def kernel(x, c, w_ada, b_ada, g_attn, g_ffn, w_in, w_out, cmp_pe, cmp_w1, cmp_b1, cmp_w2,
           w_router, b_router, w_gate, w_up, w_down, g_final):
    b, s, d = x.shape
    assert b == 1 and d == D_MODEL and s % max(Q_TILE, SLC_KEY_TILE) == 0
    depth = w_ada.shape[0]
    ncp = s // CMP_STRIDE
    nsb = s // SLC_BLOCK

    cos, sin = _rope_tables(jnp.arange(s))
    cos_c, sin_c = _rope_tables(jnp.arange(ncp) * CMP_STRIDE + CMP_LEN - 1)
    overlap = _overlap(ncp, nsb)
    flags = jnp.asarray(ROPE_FLAGS, jnp.int32)
    wr = jnp.pad(w_router.astype(F32), ((0, 0), (0, LANE - N_EXPERTS)))
    wr_hi, wr_lo = _split_bf16(wr)

    mod = _ada_mod(c.astype(F32), w_ada, b_ada)
    xs = x.reshape(s, d)
    for l in range(depth):
        sh1, sc1, gt1, sh2, sc2, gt2 = [mod[l, :, k * d:(k + 1) * d] for k in range(6)]
        w_main, w_gates = _layout_w_in(w_in[l])
        proj, gates = _project(xs, g_attn[l].reshape(1, d), sc1, sh1, w_main, w_gates, cos, sin, flags)

        chunks = proj[:, COL_KC * LANE:COL_KS * LANE].reshape(s, 2 * N_KV_NSA, HEAD_DIM)
        chunks = chunks.transpose(1, 0, 2).reshape(2 * N_KV_NSA, ncp, CMP_STRIDE * HEAD_DIM)
        pe8 = jnp.pad(cmp_pe[l].reshape(2, 1, CMP_LEN * HEAD_DIM), ((0, 0), (0, 7), (0, 0)))
        cmp_kv = _compress(chunks, cmp_w1[l], cmp_b1[l].reshape(2, 1, CMP_HIDDEN), cmp_w2[l], pe8, cos_c, sin_c)

        ocmp, sel = _nsa_cmp(proj, cmp_kv, overlap, gates)
        o_nsa = _nsa_slc(proj, sel, gates, ocmp)
        o_moba = _moba(proj)

        w_o = w_out[l].astype(BF16)
        x1, h2, logits = _out_proj(o_nsa, o_moba, xs, w_o[:NSA_WIDTH], w_o[NSA_WIDTH:], gt1,
                                   g_ffn[l].reshape(1, d), sc2, sh2, wr_hi, wr_lo)

        slot_tok, slot_w, pair_slot, blk_exp, n_used = _route(logits, b_router)
        h2_words = lax.bitcast_convert_type(h2.reshape(s, d // 2, 2), jnp.uint32)
        x_sorted = lax.bitcast_convert_type(_gather_rows(h2_words, slot_tok), BF16).reshape(-1, d)
        ys = _experts(x_sorted, blk_exp, n_used, w_gate, w_up, w_down, slot_w, l)
        xs = _combine(x1, ys, pair_slot[:, 0], pair_slot[:, 1], gt2, g_final.reshape(1, d), l == depth - 1)
    return xs.reshape(b, s, d)
```

```python
import functools

import jax
import jax.numpy as jnp
from jax import lax
from jax.experimental import pallas as pl
from jax.experimental.pallas import tpu as pltpu

F32 = jnp.float32
BF16 = jnp.bfloat16

D_MODEL = 2048
HEAD_DIM = 128
N_HEADS_NSA = 8
N_KV_NSA = 2
NSA_REP = N_HEADS_NSA // N_KV_NSA
N_HEADS_MOBA = 8
NSA_WIDTH = N_HEADS_NSA * HEAD_DIM
NSA_KV_WIDTH = N_KV_NSA * HEAD_DIM
MOBA_WIDTH = N_HEADS_MOBA * HEAD_DIM
N_GATES = 3 * N_HEADS_NSA
ROPE_THETA = 10000.0
CMP_LEN = 32
CMP_STRIDE = 16
CMP_HIDDEN = 256
SLC_BLOCK = 64
SLC_TOPN = 16
WINDOW = 512
MOBA_BLOCK = 256
MOBA_TOPK = 3
N_EXPERTS = 32
N_GROUPS = 4
EXPERTS_PER_GROUP = N_EXPERTS // N_GROUPS
TOP_K_EXPERTS = 2
D_EXPERT = 512
RMS_EPS = 1e-6
NEG_INF = -1e30
FORCE = 1e9
LOWEST = -3e38
SCALE = HEAD_DIM ** -0.5
EXP2_SCALE = SCALE * 1.4426950408889634

LANE = 128
Q_TILE = 256
MOBA_Q_TILE = 512
SLC_KEY_TILE = 512
MOE_ROWS = 256
GATHER_ROWS = 256
PROJ_TM = 1024
PROJ_TN = 512
OUT_TM = 256
VMEM_LIMIT = 56 * 1024 * 1024

COL_QN = 0
COL_KC = 8
COL_VC = 10
COL_KS = 12
COL_VS = 14
COL_KW = 16
COL_VW = 18
COL_QM = 20
COL_KM = 28
COL_VM = 36
PROJ_COLS = 44
ROPE_FLAGS = tuple(
    1 if (c < 8 or c in (12, 13, 16, 17) or 20 <= c < 36) else 0 for c in range(PROJ_COLS))


def _cparams(sem):
    return pltpu.CompilerParams(dimension_semantics=sem, vmem_limit_bytes=VMEM_LIMIT)


def _dot(a, b):
    return jnp.dot(a, b, preferred_element_type=F32)


def _dot_nt(a, b):
    return lax.dot_general(a, b, (((1,), (1,)), ((), ())), preferred_element_type=F32)


def _split_bf16(v):
    hi = v.astype(BF16)
    lo = (v - hi.astype(F32)).astype(BF16)
    return hi, lo


def _mod_kernel(c_ref, w_ref, b_ref, o_ref):
    c = c_ref[...]
    cond = c * jax.nn.sigmoid(c)
    o_ref[0] = jnp.sum(w_ref[0] * cond, axis=0, keepdims=True) + b_ref[0]


def _ada_mod(c, w_ada, b_ada):
    depth, d, n = w_ada.shape
    tn = 1024
    return pl.pallas_call(
        _mod_kernel,
        grid=(depth, n // tn),
        in_specs=[
            pl.BlockSpec((d, 1), lambda l, j: (0, 0)),
            pl.BlockSpec((1, d, tn), lambda l, j: (l, 0, j)),
            pl.BlockSpec((1, 1, tn), lambda l, j: (l, 0, j)),
        ],
        out_specs=pl.BlockSpec((1, 1, tn), lambda l, j: (l, 0, j)),
        out_shape=jax.ShapeDtypeStruct((depth, 1, n), F32),
        compiler_params=_cparams(("arbitrary", "arbitrary")),
        name="ada_mod",
    )(c.reshape(d, 1), w_ada, b_ada.reshape(depth, 1, n))


def _rms_mod(x, g, sc, sh):
    y = x * lax.rsqrt(jnp.mean(x * x, axis=-1, keepdims=True) + RMS_EPS)
    return (y * g) * (1.0 + sc) + sh


def _proj_kernel(flags_ref, x_ref, g_ref, sc_ref, sh_ref, w_ref, wg_ref, cos_ref, sin_ref,
                 o_ref, gate_ref, h_scr):
    j = pl.program_id(1)

    @pl.when(j == 0)
    def _():
        hb = _rms_mod(x_ref[...], g_ref[...], sc_ref[...], sh_ref[...]).astype(BF16)
        h_scr[...] = hb
        gate_ref[...] = _dot(hb, wg_ref[...])

    acc = _dot(h_scr[...], w_ref[...])
    cos = cos_ref[...]
    sin = sin_ref[...]
    nch = PROJ_TN // LANE
    for c in range(nch):
        a = acc[:, c * LANE:(c + 1) * LANE]
        roped = a * cos + pltpu.roll(a, HEAD_DIM // 2, 1) * sin
        f = flags_ref[j * nch + c]
        o_ref[:, c * LANE:(c + 1) * LANE] = jnp.where(f > 0, roped, a).astype(BF16)


def _project(x, g, sc, sh, w_main, w_gate, cos, sin, flags):
    s, d = x.shape
    n = w_main.shape[1]
    tm = min(PROJ_TM, s)
    grid_spec = pltpu.PrefetchScalarGridSpec(
        num_scalar_prefetch=1,
        grid=(s // tm, n // PROJ_TN),
        in_specs=[
            pl.BlockSpec((tm, d), lambda i, j, f: (i, 0)),
            pl.BlockSpec((1, d), lambda i, j, f: (0, 0)),
            pl.BlockSpec((1, d), lambda i, j, f: (0, 0)),
            pl.BlockSpec((1, d), lambda i, j, f: (0, 0)),
            pl.BlockSpec((d, PROJ_TN), lambda i, j, f: (0, j)),
            pl.BlockSpec((d, 2 * LANE), lambda i, j, f: (0, 0)),
            pl.BlockSpec((tm, LANE), lambda i, j, f: (i, 0)),
            pl.BlockSpec((tm, LANE), lambda i, j, f: (i, 0)),
        ],
        out_specs=[
            pl.BlockSpec((tm, PROJ_TN), lambda i, j, f: (i, j)),
            pl.BlockSpec((tm, 2 * LANE), lambda i, j, f: (i, 0)),
        ],
        scratch_shapes=[pltpu.VMEM((tm, d), BF16)],
    )
    return pl.pallas_call(
        _proj_kernel,
        grid_spec=grid_spec,
        out_shape=[jax.ShapeDtypeStruct((s, n), BF16), jax.ShapeDtypeStruct((s, 2 * LANE), F32)],
        compiler_params=_cparams(("arbitrary", "arbitrary")),
        name="norm_proj_rope",
    )(flags, x, g, sc, sh, w_main, w_gate, cos, sin)


def _compress_kernel(ch_ref, w1_ref, b1_ref, w2_ref, pe_ref, cos_ref, sin_ref, o_ref):
    a = pl.program_id(0)
    ch = ch_ref[0]
    nc = ch.shape[0]
    half = CMP_STRIDE * HEAD_DIM
    w1 = w1_ref[0].astype(BF16)
    top = _dot(ch, w1[:half])
    bot = _dot(ch, w1[half:])
    bot = pltpu.roll(bot, nc - 1, 0)
    pe_term = _dot(pe_ref[0].astype(BF16), w1)[0:1]
    hid = jax.nn.gelu(top + bot + pe_term + b1_ref[0])
    out = _dot(hid.astype(BF16), w2_ref[0].astype(BF16))
    roped = out * cos_ref[...] + pltpu.roll(out, HEAD_DIM // 2, 1) * sin_ref[...]
    out = jnp.where(a < N_KV_NSA, roped, out)
    row = lax.broadcasted_iota(jnp.int32, out.shape, 0)
    o_ref[0] = jnp.where(row < nc - 1, out, 0.0).astype(BF16)


def _compress(chunks, w1, b1, w2, pe8, cos_c, sin_c):
    na, nc, kw = chunks.shape
    return pl.pallas_call(
        _compress_kernel,
        grid=(na,),
        in_specs=[
            pl.BlockSpec((1, nc, kw), lambda a: (a, 0, 0)),
            pl.BlockSpec((1, CMP_LEN * HEAD_DIM, CMP_HIDDEN), lambda a: (a // N_KV_NSA, 0, 0)),
            pl.BlockSpec((1, 1, CMP_HIDDEN), lambda a: (a // N_KV_NSA, 0, 0)),
            pl.BlockSpec((1, CMP_HIDDEN, HEAD_DIM), lambda a: (a // N_KV_NSA, 0, 0)),
            pl.BlockSpec((1, 8, CMP_LEN * HEAD_DIM), lambda a: (a // N_KV_NSA, 0, 0)),
            pl.BlockSpec((nc, HEAD_DIM), lambda a: (0, 0)),
            pl.BlockSpec((nc, HEAD_DIM), lambda a: (0, 0)),
        ],
        out_specs=pl.BlockSpec((1, nc, HEAD_DIM), lambda a: (a, 0, 0)),
        out_shape=jax.ShapeDtypeStruct((na, nc, HEAD_DIM), BF16),
        compiler_params=_cparams(("arbitrary",)),
        name="nsa_compress",
    )(chunks, w1, b1, w2, pe8, cos_c, sin_c)


def _nsa_cmp_kernel(q_ref, kc_ref, vc_ref, ov_ref, gate_ref, ocmp_ref, sel_ref):
    i = pl.program_id(1)
    tq = q_ref.shape[0]
    ncp = kc_ref.shape[1]
    nsb = sel_ref.shape[2]
    t = i * tq + lax.broadcasted_iota(jnp.int32, (tq, 1), 0)
    cend = lax.broadcasted_iota(jnp.int32, (1, ncp), 1) * CMP_STRIDE + (CMP_LEN - 1)
    mask = cend <= t
    kc = kc_ref[0]
    vc = vc_ref[0]
    gates = jax.nn.sigmoid(gate_ref[...])
    psum = jnp.zeros((tq, ncp), F32)
    for r in range(NSA_REP):
        q = q_ref[:, r * HEAD_DIM:(r + 1) * HEAD_DIM]
        s = jnp.where(mask, _dot_nt(q, kc) * SCALE, NEG_INF)
        m = jnp.max(s, axis=-1, keepdims=True)
        e = jnp.where(mask, jnp.exp(s - m), 0.0)
        l = jnp.sum(e, axis=-1, keepdims=True)
        p = e / jnp.where(l > 0.0, l, 1.0)
        psum = psum + p
        o = _dot(p.astype(BF16), vc)
        ocmp_ref[:, r * HEAD_DIM:(r + 1) * HEAD_DIM] = gates[:, 3 * r:3 * r + 1] * o

    p_hi, p_lo = _split_bf16(psum)
    ov = ov_ref[...]
    imp = _dot(p_hi, ov) + _dot(p_lo, ov)
    jblk = lax.broadcasted_iota(jnp.int32, (1, nsb), 1)
    tb = t // SLC_BLOCK
    forced = (jblk == 0) | (jblk == tb) | (jblk == tb - 1)
    work = jnp.where(jblk <= tb, jnp.where(forced, FORCE, imp), NEG_INF)
    jf = jblk.astype(F32)
    sel = jnp.zeros((tq, nsb), F32)
    for _ in range(min(SLC_TOPN, nsb)):
        m = jnp.max(work, axis=-1, keepdims=True)
        first = jnp.min(jnp.where(work == m, jf, 1e9), axis=-1, keepdims=True)
        hit = jf == first
        sel = jnp.where(hit, 1.0, sel)
        work = jnp.where(hit, LOWEST, work)
    sel_ref[0] = sel.astype(BF16)


def _nsa_cmp(proj, cmp_kv, overlap, gates):
    s = proj.shape[0]
    ncp = cmp_kv.shape[1]
    nsb = overlap.shape[1]
    tq = min(Q_TILE, s)
    gw = NSA_REP * HEAD_DIM
    return pl.pallas_call(
        _nsa_cmp_kernel,
        grid=(N_KV_NSA, s // tq),
        in_specs=[
            pl.BlockSpec((tq, gw), lambda g, i: (i, g)),
            pl.BlockSpec((1, ncp, HEAD_DIM), lambda g, i: (g, 0, 0)),
            pl.BlockSpec((1, ncp, HEAD_DIM), lambda g, i: (N_KV_NSA + g, 0, 0)),
            pl.BlockSpec((ncp, nsb), lambda g, i: (0, 0)),
            pl.BlockSpec((tq, LANE), lambda g, i: (i, g)),
        ],
        out_specs=[
            pl.BlockSpec((tq, gw), lambda g, i: (i, g)),
            pl.BlockSpec((1, tq, nsb), lambda g, i: (g, i, 0)),
        ],
        out_shape=[jax.ShapeDtypeStruct((s, NSA_WIDTH), F32),
                   jax.ShapeDtypeStruct((N_KV_NSA, s, nsb), BF16)],
        compiler_params=_cparams(("arbitrary", "arbitrary")),
        name="nsa_cmp_select",
    )(proj, cmp_kv, cmp_kv, overlap, gates)


def _flash_init(s, v_aug, m_scr, acc_scr, idx):
    m = jnp.max(s, axis=-1, keepdims=True)
    p = jnp.exp2((s - m) * EXP2_SCALE)
    acc_scr[idx] = _dot(p.astype(BF16), v_aug)
    m_scr[idx] = m


def _flash_step(s, v_aug, m_scr, acc_scr, idx):
    m_old = m_scr[idx]
    m_new = jnp.maximum(m_old, jnp.max(s, axis=-1, keepdims=True))
    alpha = jnp.exp2((m_old - m_new) * EXP2_SCALE)
    p = jnp.exp2((s - m_new) * EXP2_SCALE)
    acc_scr[idx] = alpha * acc_scr[idx] + _dot(p.astype(BF16), v_aug)
    m_scr[idx] = m_new


def _with_ones(v):
    return jnp.concatenate([v, jnp.ones(v.shape, v.dtype)], axis=1)


def _block_onehot(k0, tk, block, lo, width):
    blk = (k0 + lax.broadcasted_iota(jnp.int32, (tk, 1), 0)) // block - lo
    return (blk == lax.broadcasted_iota(jnp.int32, (1, width), 1)).astype(BF16)


def _nsa_slc_kernel(q_ref, ks_ref, vs_ref, kw0_ref, kw1_ref, kw2_ref, vw0_ref, vw1_ref, vw2_ref,
                    sel_ref, gate_ref, ocmp_ref, o_ref, qa_scr, m_scr, acc_scr):
    i = pl.program_id(1)
    tq = q_ref.shape[0]
    nsb = sel_ref.shape[2]
    tk = min(SLC_KEY_TILE, ks_ref.shape[0])
    width = min(nsb, LANE)
    nhalf = max(nsb // LANE, 1)
    kt_per_half = (LANE * SLC_BLOCK) // tk
    t = i * tq + lax.broadcasted_iota(jnp.int32, (tq, 1), 0)
    bias = ((1.0 - sel_ref[0].astype(F32)) * NEG_INF).astype(BF16)
    kt_diag = (i * tq) // tk

    def load_q_aug(half):
        b = bias[:, half * width:(half + 1) * width]
        for r in range(NSA_REP):
            qa_scr[r] = jnp.concatenate([q_ref[:, r * HEAD_DIM:(r + 1) * HEAD_DIM], b], axis=1)

    def kv_aug(kt, half):
        k0 = pl.multiple_of(kt * tk, tk)
        onehot = _block_onehot(k0, tk, SLC_BLOCK, half * LANE, width)
        ka = jnp.concatenate([ks_ref[pl.ds(k0, tk), :], onehot], axis=1)
        return ka, _with_ones(vs_ref[pl.ds(k0, tk), :])

    for half in range(nhalf):
        @pl.when(kt_diag // kt_per_half == half)
        def _():
            load_q_aug(half)
            ka, va = kv_aug(kt_diag, half)
            kpos = kt_diag * tk + lax.broadcasted_iota(jnp.int32, (1, tk), 1)
            for r in range(NSA_REP):
                s = jnp.where(kpos <= t, _dot_nt(qa_scr[r], ka), NEG_INF)
                _flash_init(s, va, m_scr, acc_scr, r)

    for half in reversed(range(nhalf)):
        lo = half * kt_per_half
        hi = jnp.minimum(kt_diag, lo + kt_per_half)

        @pl.when(hi > lo)
        def _():
            load_q_aug(half)

            def key_tile(kt, carry):
                ka, va = kv_aug(kt, half)
                for r in range(NSA_REP):
                    _flash_step(_dot_nt(qa_scr[r], ka), va, m_scr, acc_scr, r)
                return carry

            lax.fori_loop(lo, hi, key_tile, 0)

    kwin = jnp.concatenate([kw0_ref[...], kw1_ref[...], kw2_ref[...]], axis=0)
    vwin = _with_ones(jnp.concatenate([vw0_ref[...], vw1_ref[...], vw2_ref[...]], axis=0))
    nw = kwin.shape[0]
    wpos = (i - 2) * tq + lax.broadcasted_iota(jnp.int32, (1, nw), 1)
    wvalid = (wpos <= t) & (wpos > t - WINDOW) & (wpos >= 0)
    gates = jax.nn.sigmoid(gate_ref[...])
    for r in range(NSA_REP):
        q = q_ref[:, r * HEAD_DIM:(r + 1) * HEAD_DIM]
        s = jnp.where(wvalid, _dot_nt(q, kwin), NEG_INF)
        m = jnp.max(s, axis=-1, keepdims=True)
        ow = _dot(jnp.exp2((s - m) * EXP2_SCALE).astype(BF16), vwin)
        o_win = ow[:, :HEAD_DIM] / ow[:, HEAD_DIM:]
        acc = acc_scr[r]
        o_slc = acc[:, :HEAD_DIM] / acc[:, HEAD_DIM:]
        o = (ocmp_ref[:, r * HEAD_DIM:(r + 1) * HEAD_DIM]
             + gates[:, 3 * r + 1:3 * r + 2] * o_slc + gates[:, 3 * r + 2:3 * r + 3] * o_win)
        o_ref[:, r * HEAD_DIM:(r + 1) * HEAD_DIM] = o.astype(BF16)


def _nsa_slc(proj, sel, gates, ocmp):
    s = proj.shape[0]
    nsb = sel.shape[2]
    tq = min(Q_TILE, s)
    gw = NSA_REP * HEAD_DIM
    assert WINDOW == 2 * tq and (nsb <= LANE or nsb % LANE == 0)

    def win_spec(col, d):
        return pl.BlockSpec((tq, HEAD_DIM), lambda g, i: (jnp.maximum(i - 2 + d, 0), col + g))

    return pl.pallas_call(
        _nsa_slc_kernel,
        grid=(N_KV_NSA, s // tq),
        in_specs=[
            pl.BlockSpec((tq, gw), lambda g, i: (i, g)),
            pl.BlockSpec((s, HEAD_DIM), lambda g, i: (0, COL_KS + g)),
            pl.BlockSpec((s, HEAD_DIM), lambda g, i: (0, COL_VS + g)),
            win_spec(COL_KW, 0), win_spec(COL_KW, 1), win_spec(COL_KW, 2),
            win_spec(COL_VW, 0), win_spec(COL_VW, 1), win_spec(COL_VW, 2),
            pl.BlockSpec((1, tq, nsb), lambda g, i: (g, i, 0)),
            pl.BlockSpec((tq, LANE), lambda g, i: (i, g)),
            pl.BlockSpec((tq, gw), lambda g, i: (i, g)),
        ],
        out_specs=pl.BlockSpec((tq, gw), lambda g, i: (i, g)),
        out_shape=jax.ShapeDtypeStruct((s, NSA_WIDTH), BF16),
        scratch_shapes=[pltpu.VMEM((NSA_REP, tq, HEAD_DIM + min(nsb, LANE)), BF16),
                        pltpu.VMEM((NSA_REP, tq, 1), F32),
                        pltpu.VMEM((NSA_REP, tq, 2 * HEAD_DIM), F32)],
        compiler_params=_cparams(("arbitrary", "arbitrary")),
        name="nsa_slc_win",
    )(proj, proj, proj, proj, proj, proj, proj, proj, proj, sel, gates, ocmp)


def _moba_kernel(q_ref, k_ref, v_ref, o_ref, km_scr, m_scr, acc_scr):
    i = pl.program_id(1)
    tq = q_ref.shape[0]
    s_len = k_ref.shape[0]
    nb = s_len // MOBA_BLOCK
    nbp = km_scr.shape[0]

    @pl.when(i == 0)
    def _():
        km_scr[...] = jnp.zeros(km_scr.shape, F32)

        def blk_mean(n, carry):
            kb = k_ref[pl.ds(pl.multiple_of(n * MOBA_BLOCK, MOBA_BLOCK), MOBA_BLOCK), :]
            km_scr[pl.ds(n, 1), :] = jnp.mean(kb.astype(F32), axis=0, keepdims=True)
            return carry

        lax.fori_loop(0, nb, blk_mean, 0)

    q = q_ref[...]
    km_hi, km_lo = _split_bf16(km_scr[...])
    gate = _dot_nt(q, km_hi) + _dot_nt(q, km_lo)
    t = i * tq + lax.broadcasted_iota(jnp.int32, (tq, 1), 0)
    own = t // MOBA_BLOCK
    nidx = lax.broadcasted_iota(jnp.int32, (1, nbp), 1)
    nf = nidx.astype(F32)
    work = jnp.where(nidx < own, gate, LOWEST)
    sel = nidx == own
    for _ in range(MOBA_TOPK):
        m = jnp.max(work, axis=-1, keepdims=True)
        first = jnp.min(jnp.where((work == m) & (m > 0.5 * LOWEST), nf, 1e9), axis=-1, keepdims=True)
        hit = nf == first
        sel = sel | hit
        work = jnp.where(hit, LOWEST, work)
    q_aug = jnp.concatenate([q, jnp.where(sel, 0.0, NEG_INF).astype(BF16)], axis=1)

    def tile(kt):
        k0 = pl.multiple_of(kt * tq, tq)
        k_aug = jnp.concatenate([k_ref[pl.ds(k0, tq), :], _block_onehot(k0, tq, MOBA_BLOCK, 0, nbp)], axis=1)
        return _dot_nt(q_aug, k_aug), _with_ones(v_ref[pl.ds(k0, tq), :])

    s, va = tile(i)
    kpos = i * tq + lax.broadcasted_iota(jnp.int32, (1, tq), 1)
    _flash_init(jnp.where(kpos <= t, s, NEG_INF), va, m_scr, acc_scr, Ellipsis)

    def past_tile(kt, carry):
        s, va = tile(kt)
        _flash_step(s, va, m_scr, acc_scr, Ellipsis)
        return carry

    lax.fori_loop(0, i, past_tile, 0)
    acc = acc_scr[...]
    o_ref[...] = (acc[:, :HEAD_DIM] / acc[:, HEAD_DIM:]).astype(BF16)


def _moba(proj):
    s = proj.shape[0]
    tq = min(MOBA_Q_TILE, s)
    nbp = -(-(s // MOBA_BLOCK) // LANE) * LANE
    return pl.pallas_call(
        _moba_kernel,
        grid=(N_HEADS_MOBA, s // tq),
        in_specs=[
            pl.BlockSpec((tq, HEAD_DIM), lambda h, i: (i, COL_QM + h)),
            pl.BlockSpec((s, HEAD_DIM), lambda h, i: (0, COL_KM + h)),
            pl.BlockSpec((s, HEAD_DIM), lambda h, i: (0, COL_VM + h)),
        ],
        out_specs=pl.BlockSpec((tq, HEAD_DIM), lambda h, i: (i, h)),
        out_shape=jax.ShapeDtypeStruct((s, MOBA_WIDTH), BF16),
        scratch_shapes=[pltpu.VMEM((nbp, HEAD_DIM), F32), pltpu.VMEM((tq, 1), F32),
                        pltpu.VMEM((tq, 2 * HEAD_DIM), F32)],
        compiler_params=_cparams(("arbitrary", "arbitrary")),
        name="moba",
    )(proj, proj, proj)


def _out_kernel(on_ref, om_ref, x_ref, wn_ref, wm_ref, gt_ref, g_ref, sc_ref, sh_ref, wrh_ref, wrl_ref,
                x1_ref, h_ref, lg_ref):
    mix = _dot(on_ref[...], wn_ref[...]) + _dot(om_ref[...], wm_ref[...])
    x1 = x_ref[...] + gt_ref[...] * mix
    x1_ref[...] = x1
    h = _rms_mod(x1, g_ref[...], sc_ref[...], sh_ref[...])
    h_hi, h_lo = _split_bf16(h)
    h_ref[...] = h
    wrh = wrh_ref[...]
    lg_ref[...] = _dot(h_hi, wrh) + _dot(h_lo, wrh) + _dot(h_hi, wrl_ref[...])


def _out_proj(o_nsa, o_moba, x, w_n, w_m, gt, g, sc, sh, wr_hi, wr_lo):
    s, d = x.shape
    tm = min(OUT_TM, s)
    row = lambda i: (i, 0)
    fix = lambda i: (0, 0)
    return pl.pallas_call(
        _out_kernel,
        grid=(s // tm,),
        in_specs=[
            pl.BlockSpec((tm, NSA_WIDTH), row),
            pl.BlockSpec((tm, MOBA_WIDTH), row),
            pl.BlockSpec((tm, d), row),
            pl.BlockSpec((NSA_WIDTH, d), fix),
            pl.BlockSpec((MOBA_WIDTH, d), fix),
            pl.BlockSpec((1, d), fix),
            pl.BlockSpec((1, d), fix),
            pl.BlockSpec((1, d), fix),
            pl.BlockSpec((1, d), fix),
            pl.BlockSpec((d, LANE), fix),
            pl.BlockSpec((d, LANE), fix),
        ],
        out_specs=[pl.BlockSpec((tm, d), row), pl.BlockSpec((tm, d), row), pl.BlockSpec((tm, LANE), row)],
        out_shape=[jax.ShapeDtypeStruct((s, d), F32), jax.ShapeDtypeStruct((s, d), F32),
                   jax.ShapeDtypeStruct((s, LANE), F32)],
        compiler_params=_cparams(("arbitrary",)),
        name="out_proj_norm_router",
    )(o_nsa, o_moba, x, w_n, w_m, gt, g, sc, sh, wr_hi, wr_lo)


def _row_copy(src_ref, dst_ref, src_row, dst_row, sem):
    return pltpu.make_async_copy(src_ref.at[pl.ds(src_row, 1)], dst_ref.at[pl.ds(dst_row, 1)], sem)


def _expert_kernel(be_ref, nu_ref, tok_ref, h_ref, wg_ref, wu_ref, wd_ref, sw_ref, y_ref,
                   x_buf, wg_s, wu_s, wd_s, sems):
    b = pl.program_id(0)
    n_used = nu_ref[0]

    def gather(blk, buf, wait):
        def row(r, carry):
            cp = _row_copy(h_ref, x_buf.at[buf], tok_ref[blk * MOE_ROWS + r], r, sems.at[buf])
            if wait:
                cp.wait()
            else:
                cp.start()
            return carry

        lax.fori_loop(0, MOE_ROWS, row, 0)

    @pl.when((b == 0) & (n_used > 0))
    def _():
        gather(0, 0, False)

    @pl.when(b + 1 < n_used)
    def _():
        gather(b + 1, (b + 1) % 2, False)

    prev = be_ref[jnp.maximum(b - 1, 0)]

    @pl.when((b == 0) | (be_ref[b] != prev))
    def _():
        wg_s[...] = wg_ref[0, 0].astype(BF16)
        wu_s[...] = wu_ref[0, 0].astype(BF16)
        wd_s[...] = wd_ref[0, 0].astype(BF16)

    @pl.when(b < n_used)
    def _():
        gather(b, b % 2, True)
        x = x_buf[b % 2].astype(BF16)
        a = _dot(x, wg_s[...])
        u = _dot(x, wu_s[...])
        hid = (a * jax.nn.sigmoid(a) * u).astype(BF16)
        y_ref[...] = _dot(hid, wd_s[...]) * sw_ref[...]

    @pl.when(b >= n_used)
    def _():
        y_ref[...] = jnp.zeros(y_ref.shape, F32)


def _experts(h, slot_tok, blk_exp, n_used, w_gate, w_up, w_down, slot_w, layer):
    n_slots = slot_tok.shape[0]
    d = h.shape[1]
    n_blk = n_slots // MOE_ROWS
    grid_spec = pltpu.PrefetchScalarGridSpec(
        num_scalar_prefetch=3,
        grid=(n_blk,),
        in_specs=[
            pl.BlockSpec(memory_space=pl.ANY),
            pl.BlockSpec((1, 1, d, D_EXPERT), lambda b, be, nu, tok: (layer, be[b], 0, 0)),
            pl.BlockSpec((1, 1, d, D_EXPERT), lambda b, be, nu, tok: (layer, be[b], 0, 0)),
            pl.BlockSpec((1, 1, D_EXPERT, d), lambda b, be, nu, tok: (layer, be[b], 0, 0)),
            pl.BlockSpec((MOE_ROWS, 1), lambda b, be, nu, tok: (b, 0)),
        ],
        out_specs=pl.BlockSpec((MOE_ROWS, d), lambda b, be, nu, tok: (b, 0)),
        scratch_shapes=[pltpu.VMEM((2, MOE_ROWS, d), F32),
                        pltpu.VMEM((d, D_EXPERT), BF16), pltpu.VMEM((d, D_EXPERT), BF16),
                        pltpu.VMEM((D_EXPERT, d), BF16), pltpu.SemaphoreType.DMA((2,))],
    )
    return pl.pallas_call(
        _expert_kernel,
        grid_spec=grid_spec,
        out_shape=jax.ShapeDtypeStruct((n_slots, d), F32),
        compiler_params=_cparams(("arbitrary",)),
        name="moe_experts",
    )(blk_exp, n_used, slot_tok, h, w_gate, w_up, w_down, slot_w.reshape(n_slots, 1))


def _combine_kernel(sa_ref, sb_ref, x_ref, gt_ref, gf_ref, y_ref, o_ref, buf_a, buf_b, sem_a, sem_b, *, final):
    tt = x_ref.shape[0]

    def start(r, carry):
        _row_copy(y_ref, buf_a, sa_ref[0, 0, r], r, sem_a).start()
        _row_copy(y_ref, buf_b, sb_ref[0, 0, r], r, sem_b).start()
        return carry

    def wait(r, carry):
        _row_copy(y_ref, buf_a, sa_ref[0, 0, r], r, sem_a).wait()
        _row_copy(y_ref, buf_b, sb_ref[0, 0, r], r, sem_b).wait()
        return carry

    lax.fori_loop(0, tt, start, 0)
    lax.fori_loop(0, tt, wait, 0)
    x2 = x_ref[...] + gt_ref[...] * (buf_a[...] + buf_b[...])
    if final:
        x2 = x2 * lax.rsqrt(jnp.mean(x2 * x2, axis=-1, keepdims=True) + RMS_EPS) * gf_ref[...]
    o_ref[...] = x2


def _combine(x1, ys, slot_a, slot_b, gt, g_final, final):
    s, d = x1.shape
    tt = min(GATHER_ROWS, s)
    nt = s // tt
    idx_spec = pl.BlockSpec((1, 1, tt), lambda b: (b, 0, 0), memory_space=pltpu.SMEM)
    return pl.pallas_call(
        functools.partial(_combine_kernel, final=final),
        grid=(nt,),
        in_specs=[
            idx_spec, idx_spec,
            pl.BlockSpec((tt, d), lambda b: (b, 0)),
            pl.BlockSpec((1, d), lambda b: (0, 0)),
            pl.BlockSpec((1, d), lambda b: (0, 0)),
            pl.BlockSpec(memory_space=pl.ANY),
        ],
        out_specs=pl.BlockSpec((tt, d), lambda b: (b, 0)),
        out_shape=jax.ShapeDtypeStruct((s, d), F32),
        scratch_shapes=[pltpu.VMEM((tt, d), F32), pltpu.VMEM((tt, d), F32),
                        pltpu.SemaphoreType.DMA(()), pltpu.SemaphoreType.DMA(())],
        compiler_params=_cparams(("arbitrary",)),
        name="moe_combine",
    )(slot_a.reshape(nt, 1, tt), slot_b.reshape(nt, 1, tt), x1, gt, g_final, ys)


def _route(logits, b_router):
    n = logits.shape[0]
    scores = jax.nn.sigmoid(logits[:, :N_EXPERTS])
    biased = (scores + b_router.astype(F32)).reshape(n, N_GROUPS, EXPERTS_PER_GROUP)
    grp = jnp.argmax(lax.top_k(biased, 2)[0].sum(-1), axis=-1)
    in_grp = jnp.take_along_axis(biased, grp[:, None, None], axis=1)[:, 0]
    _, local = lax.top_k(in_grp, TOP_K_EXPERTS)
    expert = (grp[:, None] * EXPERTS_PER_GROUP + local).astype(jnp.int32)
    wsel = jnp.take_along_axis(scores, expert, axis=1)
    wsel = wsel / jnp.sum(wsel, axis=-1, keepdims=True)

    n_pair = n * TOP_K_EXPERTS
    flat_e = expert.reshape(-1)
    flat_t = jnp.repeat(jnp.arange(n, dtype=jnp.int32), TOP_K_EXPERTS)
    onehot = (flat_e[:, None] == jnp.arange(N_EXPERTS, dtype=jnp.int32)[None, :]).astype(jnp.int32)
    running = jnp.cumsum(onehot, axis=0)
    rank = jnp.sum(running * onehot, axis=1) - 1
    counts = running[-1]
    padded = (counts + MOE_ROWS - 1) // MOE_ROWS * MOE_ROWS
    pend = jnp.cumsum(padded)
    pstart = pend - padded
    slot = (jnp.sum(pstart[None, :] * onehot, axis=1) + rank).astype(jnp.int32)
    n_slots = n_pair + N_EXPERTS * MOE_ROWS
    slot_tok = jnp.zeros((n_slots,), jnp.int32).at[slot].set(flat_t)
    slot_w = jnp.zeros((n_slots,), F32).at[slot].set(wsel.reshape(-1))
    pair_slot = slot.reshape(n, TOP_K_EXPERTS)
    n_blk = n_slots // MOE_ROWS
    blk_exp = jnp.minimum(jnp.searchsorted(pend, jnp.arange(n_blk, dtype=jnp.int32) * MOE_ROWS, side='right'),
                          N_EXPERTS - 1).astype(jnp.int32)
    n_used = (pend[-1:] // MOE_ROWS).astype(jnp.int32)
    return slot_tok, slot_w, pair_slot, blk_exp, n_used


def _rope_tables(pos):
    inv = 1.0 / (ROPE_THETA ** (jnp.arange(0, HEAD_DIM, 2, dtype=F32) / HEAD_DIM))
    ang = pos.astype(F32)[:, None] * inv[None, :]
    cos = jnp.cos(ang)
    sin = jnp.sin(ang)
    return jnp.concatenate([cos, cos], axis=-1), jnp.concatenate([-sin, sin], axis=-1)


def _layout_w_in(w_in):
    o_g = NSA_WIDTH + 6 * NSA_KV_WIDTH
    main = jnp.concatenate([w_in[:, :o_g], w_in[:, o_g + N_GATES:]], axis=1).astype(BF16)
    wg = w_in[:, o_g:o_g + N_GATES].reshape(-1, N_KV_NSA, 3 * NSA_REP)
    wg = jnp.pad(wg, ((0, 0), (0, 0), (0, LANE - 3 * NSA_REP))).reshape(-1, N_KV_NSA * LANE)
    return main, wg.astype(BF16)


def _overlap(ncp, nsb):
    cs = jnp.arange(ncp, dtype=jnp.int32)[:, None] * CMP_STRIDE
    ss = jnp.arange(nsb, dtype=jnp.int32)[None, :] * SLC_BLOCK
    return ((cs + CMP_LEN - 1 >= ss) & (cs < ss + SLC_BLOCK)).astype(BF16)


def kernel(x, c, w_ada, b_ada, g_attn, g_ffn, w_in, w_out, cmp_pe, cmp_w1, cmp_b1, cmp_w2,
           w_router, b_router, w_gate, w_up, w_down, g_final):
    b, s, d = x.shape
    assert b == 1 and d == D_MODEL and s % max(Q_TILE, SLC_KEY_TILE, MOBA_Q_TILE) == 0
    depth = w_ada.shape[0]
    ncp = s // CMP_STRIDE
    nsb = s // SLC_BLOCK

    cos, sin = _rope_tables(jnp.arange(s))
    cos_c, sin_c = _rope_tables(jnp.arange(ncp) * CMP_STRIDE + CMP_LEN - 1)
    overlap = _overlap(ncp, nsb)
    flags = jnp.asarray(ROPE_FLAGS, jnp.int32)
    wr = jnp.pad(w_router.astype(F32), ((0, 0), (0, LANE - N_EXPERTS)))
    wr_hi, wr_lo = _split_bf16(wr)

    mod = _ada_mod(c.astype(F32), w_ada, b_ada)
    xs = x.reshape(s, d)
    for l in range(depth):
        sh1, sc1, gt1, sh2, sc2, gt2 = [mod[l, :, k * d:(k + 1) * d] for k in range(6)]
        w_main, w_gates = _layout_w_in(w_in[l])
        proj, gates = _project(xs, g_attn[l].reshape(1, d), sc1, sh1, w_main, w_gates, cos, sin, flags)

        chunks = proj[:, COL_KC * LANE:COL_KS * LANE].reshape(s, 2 * N_KV_NSA, HEAD_DIM)
        chunks = chunks.transpose(1, 0, 2).reshape(2 * N_KV_NSA, ncp, CMP_STRIDE * HEAD_DIM)
        pe8 = jnp.pad(cmp_pe[l].reshape(2, 1, CMP_LEN * HEAD_DIM), ((0, 0), (0, 7), (0, 0)))
        cmp_kv = _compress(chunks, cmp_w1[l], cmp_b1[l].reshape(2, 1, CMP_HIDDEN), cmp_w2[l], pe8, cos_c, sin_c)

        ocmp, sel = _nsa_cmp(proj, cmp_kv, overlap, gates)
        o_nsa = _nsa_slc(proj, sel, gates, ocmp)
        o_moba = _moba(proj)

        w_o = w_out[l].astype(BF16)
        x1, h2, logits = _out_proj(o_nsa, o_moba, xs, w_o[:NSA_WIDTH], w_o[NSA_WIDTH:], gt1,
                                   g_ffn[l].reshape(1, d), sc2, sh2, wr_hi, wr_lo)

        slot_tok, slot_w, pair_slot, blk_exp, n_used = _route(logits, b_router)
        ys = _experts(h2, slot_tok, blk_exp, n_used, w_gate, w_up, w_down, slot_w, l)
        xs = _combine(x1, ys, pair_slot[:, 0], pair_slot[:, 1], gt2, g_final.reshape(1, d), l == depth - 1)
    return xs.reshape(b, s, d)
```

```python
import functools

import jax
import jax.numpy as jnp
from jax import lax
from jax.experimental import pallas as pl
from jax.experimental.pallas import tpu as pltpu

F32 = jnp.float32
BF16 = jnp.bfloat16

D_MODEL = 2048
HEAD_DIM = 128
N_HEADS_NSA = 8
N_KV_NSA = 2
NSA_REP = N_HEADS_NSA // N_KV_NSA
N_HEADS_MOBA = 8
NSA_WIDTH = N_HEADS_NSA * HEAD_DIM
NSA_KV_WIDTH = N_KV_NSA * HEAD_DIM
MOBA_WIDTH = N_HEADS_MOBA * HEAD_DIM
N_GATES = 3 * N_HEADS_NSA
ROPE_THETA = 10000.0
CMP_LEN = 32
CMP_STRIDE = 16
CMP_HIDDEN = 256
SLC_BLOCK = 64
SLC_TOPN = 16
WINDOW = 512
MOBA_BLOCK = 256
MOBA_TOPK = 3
N_EXPERTS = 32
N_GROUPS = 4
EXPERTS_PER_GROUP = N_EXPERTS // N_GROUPS
TOP_K_EXPERTS = 2
D_EXPERT = 512
RMS_EPS = 1e-6
NEG_INF = -1e30
FORCE = 1e9
LOWEST = -3e38
SCALE = HEAD_DIM ** -0.5
EXP2_SCALE = SCALE * 1.4426950408889634

LANE = 128
Q_TILE = 256
MOBA_Q_TILE = 512
SLC_KEY_TILE = 512
CMP_BAND = 256
CMP_ROW_TILE = 256
MOE_ROWS = 256
GATHER_ROWS = 256
PROJ_TM = 1024
PROJ_TN = 512
OUT_TM = 256
VMEM_LIMIT = 56 * 1024 * 1024

COL_QN = 0
COL_KC = 8
COL_VC = 10
COL_KS = 12
COL_VS = 14
COL_KW = 16
COL_VW = 18
COL_QM = 20
COL_KM = 28
COL_VM = 36
PROJ_COLS = 44
ROPE_FLAGS = tuple(
    1 if (c < 8 or c in (12, 13, 16, 17) or 20 <= c < 36) else 0 for c in range(PROJ_COLS))


def _cparams(sem):
    return pltpu.CompilerParams(dimension_semantics=sem, vmem_limit_bytes=VMEM_LIMIT)


def _dot(a, b):
    return jnp.dot(a, b, preferred_element_type=F32)


def _dot_nt(a, b):
    return lax.dot_general(a, b, (((1,), (1,)), ((), ())), preferred_element_type=F32)


def _split_bf16(v):
    hi = v.astype(BF16)
    lo = (v - hi.astype(F32)).astype(BF16)
    return hi, lo


def _mod_kernel(c_ref, w_ref, b_ref, o_ref):
    c = c_ref[...]
    cond = c * jax.nn.sigmoid(c)
    o_ref[0] = jnp.sum(w_ref[0] * cond, axis=0, keepdims=True) + b_ref[0]


def _ada_mod(c, w_ada, b_ada):
    depth, d, n = w_ada.shape
    tn = 1024
    return pl.pallas_call(
        _mod_kernel,
        grid=(depth, n // tn),
        in_specs=[
            pl.BlockSpec((d, 1), lambda l, j: (0, 0)),
            pl.BlockSpec((1, d, tn), lambda l, j: (l, 0, j)),
            pl.BlockSpec((1, 1, tn), lambda l, j: (l, 0, j)),
        ],
        out_specs=pl.BlockSpec((1, 1, tn), lambda l, j: (l, 0, j)),
        out_shape=jax.ShapeDtypeStruct((depth, 1, n), F32),
        compiler_params=_cparams(("arbitrary", "arbitrary")),
        name="ada_mod",
    )(c.reshape(d, 1), w_ada, b_ada.reshape(depth, 1, n))


def _rms_mod(x, g, sc, sh):
    y = x * lax.rsqrt(jnp.mean(x * x, axis=-1, keepdims=True) + RMS_EPS)
    return (y * g) * (1.0 + sc) + sh


def _proj_kernel(flags_ref, x_ref, g_ref, sc_ref, sh_ref, w_ref, wg_ref, cos_ref, sin_ref,
                 o_ref, gate_ref, h_scr):
    j = pl.program_id(1)

    @pl.when(j == 0)
    def _():
        hb = _rms_mod(x_ref[...], g_ref[...], sc_ref[...], sh_ref[...]).astype(BF16)
        h_scr[...] = hb
        gate_ref[...] = _dot(hb, wg_ref[...])

    acc = _dot(h_scr[...], w_ref[...])
    cos = cos_ref[...]
    sin = sin_ref[...]
    nch = PROJ_TN // LANE
    for c in range(nch):
        a = acc[:, c * LANE:(c + 1) * LANE]
        roped = a * cos + pltpu.roll(a, HEAD_DIM // 2, 1) * sin
        f = flags_ref[j * nch + c]
        o_ref[:, c * LANE:(c + 1) * LANE] = jnp.where(f > 0, roped, a).astype(BF16)


def _project(x, g, sc, sh, w_main, w_gate, cos, sin, flags):
    s, d = x.shape
    n = w_main.shape[1]
    tm = min(PROJ_TM, s)
    grid_spec = pltpu.PrefetchScalarGridSpec(
        num_scalar_prefetch=1,
        grid=(s // tm, n // PROJ_TN),
        in_specs=[
            pl.BlockSpec((tm, d), lambda i, j, f: (i, 0)),
            pl.BlockSpec((1, d), lambda i, j, f: (0, 0)),
            pl.BlockSpec((1, d), lambda i, j, f: (0, 0)),
            pl.BlockSpec((1, d), lambda i, j, f: (0, 0)),
            pl.BlockSpec((d, PROJ_TN), lambda i, j, f: (0, j)),
            pl.BlockSpec((d, 2 * LANE), lambda i, j, f: (0, 0)),
            pl.BlockSpec((tm, LANE), lambda i, j, f: (i, 0)),
            pl.BlockSpec((tm, LANE), lambda i, j, f: (i, 0)),
        ],
        out_specs=[
            pl.BlockSpec((tm, PROJ_TN), lambda i, j, f: (i, j)),
            pl.BlockSpec((tm, 2 * LANE), lambda i, j, f: (i, 0)),
        ],
        scratch_shapes=[pltpu.VMEM((tm, d), BF16)],
    )
    return pl.pallas_call(
        _proj_kernel,
        grid_spec=grid_spec,
        out_shape=[jax.ShapeDtypeStruct((s, n), BF16), jax.ShapeDtypeStruct((s, 2 * LANE), F32)],
        compiler_params=_cparams(("arbitrary", "arbitrary")),
        name="norm_proj_rope",
    )(flags, x, g, sc, sh, w_main, w_gate, cos, sin)


def _compress_kernel(ch_ref, w1_ref, b1_ref, w2_ref, pe_ref, cos_ref, sin_ref, o_ref):
    a = pl.program_id(0)
    ch = ch_ref[0]
    nc = ch.shape[0]
    half = CMP_STRIDE * HEAD_DIM
    w1 = w1_ref[0].astype(BF16)
    top = _dot(ch, w1[:half])
    bot = _dot(ch, w1[half:])
    bot = pltpu.roll(bot, nc - 1, 0)
    pe_term = _dot(pe_ref[0].astype(BF16), w1)[0:1]
    hid = jax.nn.gelu(top + bot + pe_term + b1_ref[0])
    out = _dot(hid.astype(BF16), w2_ref[0].astype(BF16))
    roped = out * cos_ref[...] + pltpu.roll(out, HEAD_DIM // 2, 1) * sin_ref[...]
    out = jnp.where(a < N_KV_NSA, roped, out)
    row = lax.broadcasted_iota(jnp.int32, out.shape, 0)
    o_ref[0] = jnp.where(row < nc - 1, out, 0.0).astype(BF16)


def _compress(chunks, w1, b1, w2, pe8, cos_c, sin_c):
    na, nc, kw = chunks.shape
    return pl.pallas_call(
        _compress_kernel,
        grid=(na,),
        in_specs=[
            pl.BlockSpec((1, nc, kw), lambda a: (a, 0, 0)),
            pl.BlockSpec((1, CMP_LEN * HEAD_DIM, CMP_HIDDEN), lambda a: (a // N_KV_NSA, 0, 0)),
            pl.BlockSpec((1, 1, CMP_HIDDEN), lambda a: (a // N_KV_NSA, 0, 0)),
            pl.BlockSpec((1, CMP_HIDDEN, HEAD_DIM), lambda a: (a // N_KV_NSA, 0, 0)),
            pl.BlockSpec((1, 8, CMP_LEN * HEAD_DIM), lambda a: (a // N_KV_NSA, 0, 0)),
            pl.BlockSpec((nc, HEAD_DIM), lambda a: (0, 0)),
            pl.BlockSpec((nc, HEAD_DIM), lambda a: (0, 0)),
        ],
        out_specs=pl.BlockSpec((1, nc, HEAD_DIM), lambda a: (a, 0, 0)),
        out_shape=jax.ShapeDtypeStruct((na, nc, HEAD_DIM), BF16),
        compiler_params=_cparams(("arbitrary",)),
        name="nsa_compress",
    )(chunks, w1, b1, w2, pe8, cos_c, sin_c)


def _nsa_cmp_band(q_ref, kc_ref, vc_ref, ov_ref, gate_ref, ocmp_ref, sel_ref, psum_scr, imp_scr, cw, sw):
    i = pl.program_id(1)
    tq = q_ref.shape[0]
    nsb = sel_ref.shape[2]
    t = i * tq + lax.broadcasted_iota(jnp.int32, (tq, 1), 0)
    cend = lax.broadcasted_iota(jnp.int32, (1, cw), 1) * CMP_STRIDE + (CMP_LEN - 1)
    mask = cend <= t
    kc = kc_ref[0, :cw, :]
    vc = vc_ref[0, :cw, :]
    gates = jax.nn.sigmoid(gate_ref[...])
    for r in range(NSA_REP):
        q = q_ref[:, r * HEAD_DIM:(r + 1) * HEAD_DIM]
        s = jnp.where(mask, _dot_nt(q, kc) * SCALE, NEG_INF)
        m = jnp.max(s, axis=-1, keepdims=True)
        e = jnp.where(mask, jnp.exp(s - m), 0.0)
        l = jnp.sum(e, axis=-1, keepdims=True)
        p = e / jnp.where(l > 0.0, l, 1.0)
        psum_scr[:, :cw] = p if r == 0 else psum_scr[:, :cw] + p
        o = _dot(p.astype(BF16), vc)
        ocmp_ref[:, r * HEAD_DIM:(r + 1) * HEAD_DIM] = gates[:, 3 * r:3 * r + 1] * o

    p_hi, p_lo = _split_bf16(psum_scr[:, :cw])
    ov = ov_ref[:cw, :sw]
    imp_scr[:, :sw] = _dot(p_hi, ov) + _dot(p_lo, ov)

    rt = min(CMP_ROW_TILE, tq)
    jblk = lax.broadcasted_iota(jnp.int32, (1, sw), 1)
    jf = jblk.astype(F32)

    def choose(rr, carry):
        r0 = pl.multiple_of(rr * rt, rt)
        tb = lax.shift_right_logical(i * tq + r0 + lax.broadcasted_iota(jnp.int32, (rt, 1), 0),
                                     SLC_BLOCK.bit_length() - 1)
        forced = (jblk == 0) | (jblk == tb) | (jblk == tb - 1)
        work = jnp.where(jblk <= tb, jnp.where(forced, FORCE, imp_scr[pl.ds(r0, rt), :sw]), NEG_INF)
        sel = jnp.zeros((rt, sw), F32)
        for _ in range(min(SLC_TOPN, nsb)):
            m = jnp.max(work, axis=-1, keepdims=True)
            first = jnp.min(jnp.where(work == m, jf, 1e9), axis=-1, keepdims=True)
            hit = jf == first
            sel = jnp.where(hit, 1.0, sel)
            work = jnp.where(hit, LOWEST, work)
        sel_ref[0, pl.ds(r0, rt), :sw] = sel.astype(BF16)
        return carry

    lax.fori_loop(0, tq // rt, choose, 0)
    if sw < nsb:
        sel_ref[0, :, sw:] = jnp.zeros((tq, nsb - sw), BF16)


def _nsa_cmp_kernel(q_ref, kc_ref, vc_ref, ov_ref, gate_ref, ocmp_ref, sel_ref, psum_scr, imp_scr):
    i = pl.program_id(1)
    tq = q_ref.shape[0]
    ncp = kc_ref.shape[1]
    nsb = sel_ref.shape[2]
    band_w = min(CMP_BAND, ncp)
    n_band = ncp // band_w
    band = jnp.minimum(((i + 1) * (tq // CMP_STRIDE) + band_w - 1) // band_w, n_band) - 1
    for b in range(n_band):
        cw = band_w * (b + 1)
        sw = min(nsb, -(-(cw * CMP_STRIDE // SLC_BLOCK) // LANE) * LANE)

        @pl.when(band == b)
        def _():
            _nsa_cmp_band(q_ref, kc_ref, vc_ref, ov_ref, gate_ref, ocmp_ref, sel_ref, psum_scr, imp_scr, cw, sw)


def _nsa_cmp(proj, cmp_kv, overlap, gates):
    s = proj.shape[0]
    ncp = cmp_kv.shape[1]
    nsb = overlap.shape[1]
    tq = min(Q_TILE, s)
    gw = NSA_REP * HEAD_DIM
    assert ncp % min(CMP_BAND, ncp) == 0
    return pl.pallas_call(
        _nsa_cmp_kernel,
        grid=(N_KV_NSA, s // tq),
        in_specs=[
            pl.BlockSpec((tq, gw), lambda g, i: (i, g)),
            pl.BlockSpec((1, ncp, HEAD_DIM), lambda g, i: (g, 0, 0)),
            pl.BlockSpec((1, ncp, HEAD_DIM), lambda g, i: (N_KV_NSA + g, 0, 0)),
            pl.BlockSpec((ncp, nsb), lambda g, i: (0, 0)),
            pl.BlockSpec((tq, LANE), lambda g, i: (i, g)),
        ],
        out_specs=[
            pl.BlockSpec((tq, gw), lambda g, i: (i, g)),
            pl.BlockSpec((1, tq, nsb), lambda g, i: (g, i, 0)),
        ],
        out_shape=[jax.ShapeDtypeStruct((s, NSA_WIDTH), F32),
                   jax.ShapeDtypeStruct((N_KV_NSA, s, nsb), BF16)],
        scratch_shapes=[pltpu.VMEM((tq, ncp), F32), pltpu.VMEM((tq, nsb), F32)],
        compiler_params=_cparams(("arbitrary", "arbitrary")),
        name="nsa_cmp_select",
    )(proj, cmp_kv, cmp_kv, overlap, gates)


def _lane_chunks(s):
    return [s[:, c * LANE:(c + 1) * LANE] for c in range(s.shape[1] // LANE)]


def _row_max(chunks):
    m = functools.reduce(jnp.maximum, chunks)
    return jnp.broadcast_to(jnp.max(m, axis=-1, keepdims=True), m.shape)


def _softmax_weights(chunks, m):
    return jnp.concatenate([jnp.exp2((c - m) * EXP2_SCALE) for c in chunks], axis=1).astype(BF16)


def _flash_init(s, v_aug, m_scr, acc_scr, idx):
    chunks = _lane_chunks(s)
    m = _row_max(chunks)
    acc_scr[idx] = _dot(_softmax_weights(chunks, m), v_aug)
    m_scr[idx] = m


def _flash_step(s, v_aug, m_scr, acc_scr, idx):
    chunks = _lane_chunks(s)
    m_old = m_scr[idx]
    m_new = jnp.maximum(m_old, _row_max(chunks))
    alpha = jnp.exp2((m_old - m_new) * EXP2_SCALE)
    pv = _dot(_softmax_weights(chunks, m_new), v_aug)
    acc = acc_scr[idx]
    acc_scr[idx] = jnp.concatenate([alpha * a for a in _lane_chunks(acc)], axis=1) + pv
    m_scr[idx] = m_new


def _with_ones(v):
    return jnp.concatenate([v, jnp.ones(v.shape, v.dtype)], axis=1)


def _block_onehot(k0, tk, block, width):
    pos = k0 + lax.broadcasted_iota(jnp.int32, (tk, 1), 0)
    blk = lax.shift_right_logical(pos, block.bit_length() - 1) & (width - 1)
    return (blk == lax.broadcasted_iota(jnp.int32, (1, width), 1)).astype(BF16)


def _nsa_slc_kernel(q_ref, ks_ref, vs_ref, kw0_ref, kw1_ref, kw2_ref, vw0_ref, vw1_ref, vw2_ref,
                    sel_ref, gate_ref, ocmp_ref, o_ref, ka_scr, va_scr, qa_scr, m_scr, acc_scr):
    i = pl.program_id(1)
    tq = q_ref.shape[0]
    nsb = sel_ref.shape[2]
    s_len = ks_ref.shape[0]
    tk = min(SLC_KEY_TILE, s_len)
    width = min(nsb, LANE)
    nhalf = max(nsb // LANE, 1)
    kt_per_half = (LANE * SLC_BLOCK) // tk
    t = i * tq + lax.broadcasted_iota(jnp.int32, (tq, 1), 0)
    bias = ((1.0 - sel_ref[0].astype(F32)) * NEG_INF).astype(BF16)
    kt_diag = (i * tq) // tk

    @pl.when(i == 0)
    def _():
        def build(n, carry):
            k0 = pl.multiple_of(n * tk, tk)
            ka_scr[pl.ds(k0, tk), :] = jnp.concatenate(
                [ks_ref[pl.ds(k0, tk), :], _block_onehot(k0, tk, SLC_BLOCK, width)], axis=1)
            va_scr[pl.ds(k0, tk), :] = _with_ones(vs_ref[pl.ds(k0, tk), :])
            return carry

        lax.fori_loop(0, s_len // tk, build, 0)

    def load_q_aug(half):
        b = bias[:, half * width:(half + 1) * width]
        for r in range(NSA_REP):
            qa_scr[r] = jnp.concatenate([q_ref[:, r * HEAD_DIM:(r + 1) * HEAD_DIM], b], axis=1)

    def kv_aug(kt, half):
        k0 = pl.multiple_of(kt * tk, tk)
        return ka_scr[pl.ds(k0, tk), :], va_scr[pl.ds(k0, tk), :]

    for half in range(nhalf):
        @pl.when(kt_diag // kt_per_half == half)
        def _():
            load_q_aug(half)
            ka, va = kv_aug(kt_diag, half)
            kpos = kt_diag * tk + lax.broadcasted_iota(jnp.int32, (1, tk), 1)
            for r in range(NSA_REP):
                s = jnp.where(kpos <= t, _dot_nt(qa_scr[r], ka), NEG_INF)
                _flash_init(s, va, m_scr, acc_scr, r)

    for half in reversed(range(nhalf)):
        lo = half * kt_per_half
        hi = jnp.minimum(kt_diag, lo + kt_per_half)

        @pl.when(hi > lo)
        def _():
            load_q_aug(half)

            def key_tile(kt, carry):
                ka, va = kv_aug(kt, half)
                for r in range(NSA_REP):
                    _flash_step(_dot_nt(qa_scr[r], ka), va, m_scr, acc_scr, r)
                return carry

            lax.fori_loop(lo, hi, key_tile, 0)

    kwin = jnp.concatenate([kw0_ref[...], kw1_ref[...], kw2_ref[...]], axis=0)
    vwin = _with_ones(jnp.concatenate([vw0_ref[...], vw1_ref[...], vw2_ref[...]], axis=0))
    nw = kwin.shape[0]
    wpos = (i - 2) * tq + lax.broadcasted_iota(jnp.int32, (1, nw), 1)
    wvalid = (wpos <= t) & (wpos > t - WINDOW) & (wpos >= 0)
    gates = jax.nn.sigmoid(gate_ref[...])
    for r in range(NSA_REP):
        q = q_ref[:, r * HEAD_DIM:(r + 1) * HEAD_DIM]
        s = jnp.where(wvalid, _dot_nt(q, kwin), NEG_INF)
        m = jnp.max(s, axis=-1, keepdims=True)
        ow = _dot(jnp.exp2((s - m) * EXP2_SCALE).astype(BF16), vwin)
        o_win = ow[:, :HEAD_DIM] / ow[:, HEAD_DIM:]
        acc = acc_scr[r]
        o_slc = acc[:, :HEAD_DIM] / acc[:, HEAD_DIM:]
        o = (ocmp_ref[:, r * HEAD_DIM:(r + 1) * HEAD_DIM]
             + gates[:, 3 * r + 1:3 * r + 2] * o_slc + gates[:, 3 * r + 2:3 * r + 3] * o_win)
        o_ref[:, r * HEAD_DIM:(r + 1) * HEAD_DIM] = o.astype(BF16)


def _nsa_slc(proj, sel, gates, ocmp):
    s = proj.shape[0]
    nsb = sel.shape[2]
    tq = min(Q_TILE, s)
    gw = NSA_REP * HEAD_DIM
    assert WINDOW == 2 * tq and (nsb <= LANE or nsb % LANE == 0)

    def win_spec(col, d):
        return pl.BlockSpec((tq, HEAD_DIM), lambda g, i: (jnp.maximum(i - 2 + d, 0), col + g))

    return pl.pallas_call(
        _nsa_slc_kernel,
        grid=(N_KV_NSA, s // tq),
        in_specs=[
            pl.BlockSpec((tq, gw), lambda g, i: (i, g)),
            pl.BlockSpec((s, HEAD_DIM), lambda g, i: (0, COL_KS + g)),
            pl.BlockSpec((s, HEAD_DIM), lambda g, i: (0, COL_VS + g)),
            win_spec(COL_KW, 0), win_spec(COL_KW, 1), win_spec(COL_KW, 2),
            win_spec(COL_VW, 0), win_spec(COL_VW, 1), win_spec(COL_VW, 2),
            pl.BlockSpec((1, tq, nsb), lambda g, i: (g, i, 0)),
            pl.BlockSpec((tq, LANE), lambda g, i: (i, g)),
            pl.BlockSpec((tq, gw), lambda g, i: (i, g)),
        ],
        out_specs=pl.BlockSpec((tq, gw), lambda g, i: (i, g)),
        out_shape=jax.ShapeDtypeStruct((s, NSA_WIDTH), BF16),
        scratch_shapes=[pltpu.VMEM((s, HEAD_DIM + min(nsb, LANE)), BF16),
                        pltpu.VMEM((s, 2 * HEAD_DIM), BF16),
                        pltpu.VMEM((NSA_REP, tq, HEAD_DIM + min(nsb, LANE)), BF16),
                        pltpu.VMEM((NSA_REP, tq, LANE), F32),
                        pltpu.VMEM((NSA_REP, tq, 2 * HEAD_DIM), F32)],
        compiler_params=_cparams(("arbitrary", "arbitrary")),
        name="nsa_slc_win",
    )(proj, proj, proj, proj, proj, proj, proj, proj, proj, sel, gates, ocmp)


def _moba_kernel(q_ref, k_ref, v_ref, o_ref, ka_scr, va_scr, qa_scr, km_scr, m_scr, acc_scr, sa_scr, sb_scr):
    i = pl.program_id(1)
    tq = q_ref.shape[0]
    s_len = k_ref.shape[0]
    nbp = km_scr.shape[0]
    bpt = tq // MOBA_BLOCK

    @pl.when(i == 0)
    def _():
        km_scr[...] = jnp.zeros(km_scr.shape, F32)

        def build(n, carry):
            k0 = pl.multiple_of(n * tq, tq)
            kt = k_ref[pl.ds(k0, tq), :]
            ka_scr[pl.ds(k0, tq), :] = jnp.concatenate([kt, _block_onehot(k0, tq, MOBA_BLOCK, nbp)], axis=1)
            va_scr[pl.ds(k0, tq), :] = _with_ones(v_ref[pl.ds(k0, tq), :])
            for j in range(bpt):
                kb = kt[j * MOBA_BLOCK:(j + 1) * MOBA_BLOCK].astype(F32)
                km_scr[pl.ds(n * bpt + j, 1), :] = jnp.mean(kb, axis=0, keepdims=True)
            return carry

        lax.fori_loop(0, s_len // tq, build, 0)

    q = q_ref[...]
    km_hi, km_lo = _split_bf16(km_scr[...])
    gate = _dot_nt(q, km_hi) + _dot_nt(q, km_lo)
    t = i * tq + lax.broadcasted_iota(jnp.int32, (tq, 1), 0)
    own = lax.shift_right_logical(t, MOBA_BLOCK.bit_length() - 1)
    nidx = lax.broadcasted_iota(jnp.int32, (1, nbp), 1)
    nf = nidx.astype(F32)
    work = jnp.where(nidx < own, gate, LOWEST)
    sel = nidx == own
    for _ in range(MOBA_TOPK):
        m = jnp.max(work, axis=-1, keepdims=True)
        first = jnp.min(jnp.where((work == m) & (m > 0.5 * LOWEST), nf, 1e9), axis=-1, keepdims=True)
        hit = nf == first
        sel = sel | hit
        work = jnp.where(hit, LOWEST, work)
    qa_scr[...] = jnp.concatenate([q, jnp.where(sel, 0.0, NEG_INF).astype(BF16)], axis=1)

    def scores(kt):
        return _dot_nt(qa_scr[...], ka_scr[pl.ds(pl.multiple_of(kt * tq, tq), tq), :])

    def values(kt):
        return va_scr[pl.ds(pl.multiple_of(kt * tq, tq), tq), :]

    kpos = i * tq + lax.broadcasted_iota(jnp.int32, (1, tq), 1)
    _flash_init(jnp.where(kpos <= t, scores(i), NEG_INF), values(i), m_scr, acc_scr, Ellipsis)

    last = jnp.maximum(i - 1, 0)

    @pl.when(i > 0)
    def _():
        sa_scr[...] = scores(0)

    def tile_pair(j, carry):
        sb_scr[...] = scores(jnp.minimum(2 * j + 1, last))
        _flash_step(sa_scr[...], values(2 * j), m_scr, acc_scr, Ellipsis)
        sa_scr[...] = scores(jnp.minimum(2 * j + 2, last))
        _flash_step(sb_scr[...], values(2 * j + 1), m_scr, acc_scr, Ellipsis)
        return carry

    lax.fori_loop(0, i // 2, tile_pair, 0)

    @pl.when(i % 2 == 1)
    def _():
        _flash_step(sa_scr[...], values(i - 1), m_scr, acc_scr, Ellipsis)

    acc = acc_scr[...]
    o_ref[...] = (acc[:, :HEAD_DIM] / acc[:, HEAD_DIM:]).astype(BF16)


def _moba(proj):
    s = proj.shape[0]
    tq = min(MOBA_Q_TILE, s)
    nbp = max(LANE, s // MOBA_BLOCK)
    assert nbp & (nbp - 1) == 0
    return pl.pallas_call(
        _moba_kernel,
        grid=(N_HEADS_MOBA, s // tq),
        in_specs=[
            pl.BlockSpec((tq, HEAD_DIM), lambda h, i: (i, COL_QM + h)),
            pl.BlockSpec((s, HEAD_DIM), lambda h, i: (0, COL_KM + h)),
            pl.BlockSpec((s, HEAD_DIM), lambda h, i: (0, COL_VM + h)),
        ],
        out_specs=pl.BlockSpec((tq, HEAD_DIM), lambda h, i: (i, h)),
        out_shape=jax.ShapeDtypeStruct((s, MOBA_WIDTH), BF16),
        scratch_shapes=[pltpu.VMEM((s, HEAD_DIM + nbp), BF16), pltpu.VMEM((s, 2 * HEAD_DIM), BF16),
                        pltpu.VMEM((tq, HEAD_DIM + nbp), BF16),
                        pltpu.VMEM((nbp, HEAD_DIM), F32), pltpu.VMEM((tq, LANE), F32),
                        pltpu.VMEM((tq, 2 * HEAD_DIM), F32),
                        pltpu.VMEM((tq, tq), F32), pltpu.VMEM((tq, tq), F32)],
        compiler_params=_cparams(("arbitrary", "arbitrary")),
        name="moba",
    )(proj, proj, proj)


def _out_kernel(on_ref, om_ref, x_ref, wn_ref, wm_ref, gt_ref, g_ref, sc_ref, sh_ref, wrh_ref, wrl_ref,
                x1_ref, h_ref, lg_ref):
    mix = _dot(on_ref[...], wn_ref[...]) + _dot(om_ref[...], wm_ref[...])
    x1 = x_ref[...] + gt_ref[...] * mix
    x1_ref[...] = x1
    h = _rms_mod(x1, g_ref[...], sc_ref[...], sh_ref[...])
    h_hi, h_lo = _split_bf16(h)
    h_ref[...] = h
    wrh = wrh_ref[...]
    lg_ref[...] = _dot(h_hi, wrh) + _dot(h_lo, wrh) + _dot(h_hi, wrl_ref[...])


def _out_proj(o_nsa, o_moba, x, w_n, w_m, gt, g, sc, sh, wr_hi, wr_lo):
    s, d = x.shape
    tm = min(OUT_TM, s)
    row = lambda i: (i, 0)
    fix = lambda i: (0, 0)
    return pl.pallas_call(
        _out_kernel,
        grid=(s // tm,),
        in_specs=[
            pl.BlockSpec((tm, NSA_WIDTH), row),
            pl.BlockSpec((tm, MOBA_WIDTH), row),
            pl.BlockSpec((tm, d), row),
            pl.BlockSpec((NSA_WIDTH, d), fix),
            pl.BlockSpec((MOBA_WIDTH, d), fix),
            pl.BlockSpec((1, d), fix),
            pl.BlockSpec((1, d), fix),
            pl.BlockSpec((1, d), fix),
            pl.BlockSpec((1, d), fix),
            pl.BlockSpec((d, LANE), fix),
            pl.BlockSpec((d, LANE), fix),
        ],
        out_specs=[pl.BlockSpec((tm, d), row), pl.BlockSpec((tm, d), row), pl.BlockSpec((tm, LANE), row)],
        out_shape=[jax.ShapeDtypeStruct((s, d), F32), jax.ShapeDtypeStruct((s, d), F32),
                   jax.ShapeDtypeStruct((s, LANE), F32)],
        compiler_params=_cparams(("arbitrary",)),
        name="out_proj_norm_router",
    )(o_nsa, o_moba, x, w_n, w_m, gt, g, sc, sh, wr_hi, wr_lo)


def _row_copy(src_ref, dst_ref, src_row, dst_row, sem):
    return pltpu.make_async_copy(src_ref.at[pl.ds(src_row, 1)], dst_ref.at[pl.ds(dst_row, 1)], sem)


def _expert_kernel(be_ref, nu_ref, tok_ref, h_ref, wg_ref, wu_ref, wd_ref, sw_ref, y_ref,
                   x_buf, wg_s, wu_s, wd_s, sems):
    b = pl.program_id(0)
    n_used = nu_ref[0]

    def gather(blk, buf, wait):
        def row(r, carry):
            cp = _row_copy(h_ref, x_buf.at[buf], tok_ref[blk * MOE_ROWS + r], r, sems.at[buf])
            if wait:
                cp.wait()
            else:
                cp.start()
            return carry

        lax.fori_loop(0, MOE_ROWS, row, 0)

    @pl.when((b == 0) & (n_used > 0))
    def _():
        gather(0, 0, False)

    @pl.when(b + 1 < n_used)
    def _():
        gather(b + 1, (b + 1) % 2, False)

    prev = be_ref[jnp.maximum(b - 1, 0)]

    @pl.when((b == 0) | (be_ref[b] != prev))
    def _():
        wg_s[...] = wg_ref[0, 0].astype(BF16)
        wu_s[...] = wu_ref[0, 0].astype(BF16)
        wd_s[...] = wd_ref[0, 0].astype(BF16)

    @pl.when(b < n_used)
    def _():
        gather(b, b % 2, True)
        x = x_buf[b % 2].astype(BF16)
        a = _dot(x, wg_s[...])
        u = _dot(x, wu_s[...])
        hid = (a * jax.nn.sigmoid(a) * u).astype(BF16)
        y_ref[...] = _dot(hid, wd_s[...]) * sw_ref[...]

    @pl.when(b >= n_used)
    def _():
        y_ref[...] = jnp.zeros(y_ref.shape, F32)


def _experts(h, slot_tok, blk_exp, n_used, w_gate, w_up, w_down, slot_w, layer):
    n_slots = slot_tok.shape[0]
    d = h.shape[1]
    n_blk = n_slots // MOE_ROWS
    grid_spec = pltpu.PrefetchScalarGridSpec(
        num_scalar_prefetch=3,
        grid=(n_blk,),
        in_specs=[
            pl.BlockSpec(memory_space=pl.ANY),
            pl.BlockSpec((1, 1, d, D_EXPERT), lambda b, be, nu, tok: (layer, be[b], 0, 0)),
            pl.BlockSpec((1, 1, d, D_EXPERT), lambda b, be, nu, tok: (layer, be[b], 0, 0)),
            pl.BlockSpec((1, 1, D_EXPERT, d), lambda b, be, nu, tok: (layer, be[b], 0, 0)),
            pl.BlockSpec((MOE_ROWS, 1), lambda b, be, nu, tok: (b, 0)),
        ],
        out_specs=pl.BlockSpec((MOE_ROWS, d), lambda b, be, nu, tok: (b, 0)),
        scratch_shapes=[pltpu.VMEM((2, MOE_ROWS, d), F32),
                        pltpu.VMEM((d, D_EXPERT), BF16), pltpu.VMEM((d, D_EXPERT), BF16),
                        pltpu.VMEM((D_EXPERT, d), BF16), pltpu.SemaphoreType.DMA((2,))],
    )
    return pl.pallas_call(
        _expert_kernel,
        grid_spec=grid_spec,
        out_shape=jax.ShapeDtypeStruct((n_slots, d), F32),
        compiler_params=_cparams(("arbitrary",)),
        name="moe_experts",
    )(blk_exp, n_used, slot_tok, h, w_gate, w_up, w_down, slot_w.reshape(n_slots, 1))


def _combine_kernel(sa_ref, sb_ref, x_ref, gt_ref, gf_ref, y_ref, o_ref, buf_a, buf_b, sem_a, sem_b, *, final):
    tt = x_ref.shape[0]

    def start(r, carry):
        _row_copy(y_ref, buf_a, sa_ref[0, 0, r], r, sem_a).start()
        _row_copy(y_ref, buf_b, sb_ref[0, 0, r], r, sem_b).start()
        return carry

    def wait(r, carry):
        _row_copy(y_ref, buf_a, sa_ref[0, 0, r], r, sem_a).wait()
        _row_copy(y_ref, buf_b, sb_ref[0, 0, r], r, sem_b).wait()
        return carry

    lax.fori_loop(0, tt, start, 0)
    lax.fori_loop(0, tt, wait, 0)
    x2 = x_ref[...] + gt_ref[...] * (buf_a[...] + buf_b[...])
    if final:
        x2 = x2 * lax.rsqrt(jnp.mean(x2 * x2, axis=-1, keepdims=True) + RMS_EPS) * gf_ref[...]
    o_ref[...] = x2


def _combine(x1, ys, slot_a, slot_b, gt, g_final, final):
    s, d = x1.shape
    tt = min(GATHER_ROWS, s)
    nt = s // tt
    idx_spec = pl.BlockSpec((1, 1, tt), lambda b: (b, 0, 0), memory_space=pltpu.SMEM)
    return pl.pallas_call(
        functools.partial(_combine_kernel, final=final),
        grid=(nt,),
        in_specs=[
            idx_spec, idx_spec,
            pl.BlockSpec((tt, d), lambda b: (b, 0)),
            pl.BlockSpec((1, d), lambda b: (0, 0)),
            pl.BlockSpec((1, d), lambda b: (0, 0)),
            pl.BlockSpec(memory_space=pl.ANY),
        ],
        out_specs=pl.BlockSpec((tt, d), lambda b: (b, 0)),
        out_shape=jax.ShapeDtypeStruct((s, d), F32),
        scratch_shapes=[pltpu.VMEM((tt, d), F32), pltpu.VMEM((tt, d), F32),
                        pltpu.SemaphoreType.DMA(()), pltpu.SemaphoreType.DMA(())],
        compiler_params=_cparams(("arbitrary",)),
        name="moe_combine",
    )(slot_a.reshape(nt, 1, tt), slot_b.reshape(nt, 1, tt), x1, gt, g_final, ys)


def _route(logits, b_router):
    n = logits.shape[0]
    scores = jax.nn.sigmoid(logits[:, :N_EXPERTS])
    biased = (scores + b_router.astype(F32)).reshape(n, N_GROUPS, EXPERTS_PER_GROUP)

    def top2(v):
        idx = lax.broadcasted_iota(jnp.int32, v.shape, v.ndim - 1)
        v1 = jnp.max(v, axis=-1, keepdims=True)
        i1 = jnp.min(jnp.where(v == v1, idx, v.shape[-1]), axis=-1, keepdims=True)
        rest = jnp.where(idx == i1, -jnp.inf, v)
        v2 = jnp.max(rest, axis=-1, keepdims=True)
        i2 = jnp.min(jnp.where(rest == v2, idx, v.shape[-1]), axis=-1, keepdims=True)
        return v1, v2, i1, i2

    g1, g2, _, _ = top2(biased)
    grp = jnp.argmax((g1 + g2)[..., 0], axis=-1)
    grp_hot = grp[:, None] == jnp.arange(N_GROUPS, dtype=grp.dtype)[None, :]
    in_grp = jnp.sum(jnp.where(grp_hot[:, :, None], biased, 0.0), axis=1)
    _, _, l1, l2 = top2(in_grp)
    expert = (grp[:, None] * EXPERTS_PER_GROUP + jnp.concatenate([l1, l2], axis=1)).astype(jnp.int32)
    e_hot = expert[:, :, None] == jnp.arange(N_EXPERTS, dtype=jnp.int32)[None, None, :]
    wsel = jnp.sum(jnp.where(e_hot, scores[:, None, :], 0.0), axis=-1)
    wsel = wsel / jnp.sum(wsel, axis=-1, keepdims=True)

    n_pair = n * TOP_K_EXPERTS
    flat_e = expert.reshape(-1)
    flat_t = jnp.repeat(jnp.arange(n, dtype=jnp.int32), TOP_K_EXPERTS)
    onehot = (flat_e[:, None] == jnp.arange(N_EXPERTS, dtype=jnp.int32)[None, :]).astype(jnp.int32)
    running = jnp.cumsum(onehot, axis=0)
    rank = jnp.sum(running * onehot, axis=1) - 1
    counts = running[-1]
    padded = (counts + MOE_ROWS - 1) // MOE_ROWS * MOE_ROWS
    pend = jnp.cumsum(padded)
    pstart = pend - padded
    slot = (jnp.sum(pstart[None, :] * onehot, axis=1) + rank).astype(jnp.int32)
    n_slots = n_pair + N_EXPERTS * MOE_ROWS
    slot_tok = jnp.zeros((n_slots,), jnp.int32).at[slot].set(flat_t)
    slot_w = jnp.zeros((n_slots,), F32).at[slot].set(wsel.reshape(-1))
    pair_slot = slot.reshape(n, TOP_K_EXPERTS)
    n_blk = n_slots // MOE_ROWS
    blk_exp = jnp.minimum(jnp.searchsorted(pend, jnp.arange(n_blk, dtype=jnp.int32) * MOE_ROWS, side='right'),
                          N_EXPERTS - 1).astype(jnp.int32)
    n_used = (pend[-1:] // MOE_ROWS).astype(jnp.int32)
    return slot_tok, slot_w, pair_slot, blk_exp, n_used


def _rope_tables(pos):
    inv = 1.0 / (ROPE_THETA ** (jnp.arange(0, HEAD_DIM, 2, dtype=F32) / HEAD_DIM))
    ang = pos.astype(F32)[:, None] * inv[None, :]
    cos = jnp.cos(ang)
    sin = jnp.sin(ang)
    return jnp.concatenate([cos, cos], axis=-1), jnp.concatenate([-sin, sin], axis=-1)


def _layout_w_in(w_in):
    o_g = NSA_WIDTH + 6 * NSA_KV_WIDTH
    main = jnp.concatenate([w_in[:, :o_g], w_in[:, o_g + N_GATES:]], axis=1).astype(BF16)
    wg = w_in[:, o_g:o_g + N_GATES].reshape(-1, N_KV_NSA, 3 * NSA_REP)
    wg = jnp.pad(wg, ((0, 0), (0, 0), (0, LANE - 3 * NSA_REP))).reshape(-1, N_KV_NSA * LANE)
    return main, wg.astype(BF16)


def _overlap(ncp, nsb):
    cs = jnp.arange(ncp, dtype=jnp.int32)[:, None] * CMP_STRIDE
    ss = jnp.arange(nsb, dtype=jnp.int32)[None, :] * SLC_BLOCK
    return ((cs + CMP_LEN - 1 >= ss) & (cs < ss + SLC_BLOCK)).astype(BF16)


def kernel(x, c, w_ada, b_ada, g_attn, g_ffn, w_in, w_out, cmp_pe, cmp_w1, cmp_b1, cmp_w2,
           w_router, b_router, w_gate, w_up, w_down, g_final):
    b, s, d = x.shape
    assert b == 1 and d == D_MODEL and s % max(Q_TILE, SLC_KEY_TILE, MOBA_Q_TILE) == 0
    depth = w_ada.shape[0]
    ncp = s // CMP_STRIDE
    nsb = s // SLC_BLOCK

    cos, sin = _rope_tables(jnp.arange(s))
    cos_c, sin_c = _rope_tables(jnp.arange(ncp) * CMP_STRIDE + CMP_LEN - 1)
    overlap = _overlap(ncp, nsb)
    flags = jnp.asarray(ROPE_FLAGS, jnp.int32)
    wr = jnp.pad(w_router.astype(F32), ((0, 0), (0, LANE - N_EXPERTS)))
    wr_hi, wr_lo = _split_bf16(wr)

    mod = _ada_mod(c.astype(F32), w_ada, b_ada)
    xs = x.reshape(s, d)
    for l in range(depth):
        sh1, sc1, gt1, sh2, sc2, gt2 = [mod[l, :, k * d:(k + 1) * d] for k in range(6)]
        w_main, w_gates = _layout_w_in(w_in[l])
        proj, gates = _project(xs, g_attn[l].reshape(1, d), sc1, sh1, w_main, w_gates, cos, sin, flags)

        chunks = proj[:, COL_KC * LANE:COL_KS * LANE].reshape(s, 2 * N_KV_NSA, HEAD_DIM)
        chunks = chunks.transpose(1, 0, 2).reshape(2 * N_KV_NSA, ncp, CMP_STRIDE * HEAD_DIM)
        pe8 = jnp.pad(cmp_pe[l].reshape(2, 1, CMP_LEN * HEAD_DIM), ((0, 0), (0, 7), (0, 0)))
        cmp_kv = _compress(chunks, cmp_w1[l], cmp_b1[l].reshape(2, 1, CMP_HIDDEN), cmp_w2[l], pe8, cos_c, sin_c)

        ocmp, sel = _nsa_cmp(proj, cmp_kv, overlap, gates)
        o_nsa = _nsa_slc(proj, sel, gates, ocmp)
        o_moba = _moba(proj)

        w_o = w_out[l].astype(BF16)
        x1, h2, logits = _out_proj(o_nsa, o_moba, xs, w_o[:NSA_WIDTH], w_o[NSA_WIDTH:], gt1,
                                   g_ffn[l].reshape(1, d), sc2, sh2, wr_hi, wr_lo)

        slot_tok, slot_w, pair_slot, blk_exp, n_used = _route(logits, b_router)
        ys = _experts(h2, slot_tok, blk_exp, n_used, w_gate, w_up, w_down, slot_w, l)
        xs = _combine(x1, ys, pair_slot[:, 0], pair_slot[:, 1], gt2, g_final.reshape(1, d), l == depth - 1)
    return xs.reshape(b, s, d)
```

```python
import functools

import jax
import jax.numpy as jnp
from jax import lax
from jax.experimental import pallas as pl
from jax.experimental.pallas import tpu as pltpu

F32 = jnp.float32
BF16 = jnp.bfloat16

D_MODEL = 2048
HEAD_DIM = 128
N_HEADS_NSA = 8
N_KV_NSA = 2
NSA_REP = N_HEADS_NSA // N_KV_NSA
N_HEADS_MOBA = 8
NSA_WIDTH = N_HEADS_NSA * HEAD_DIM
NSA_KV_WIDTH = N_KV_NSA * HEAD_DIM
MOBA_WIDTH = N_HEADS_MOBA * HEAD_DIM
N_GATES = 3 * N_HEADS_NSA
ROPE_THETA = 10000.0
CMP_LEN = 32
CMP_STRIDE = 16
CMP_HIDDEN = 256
SLC_BLOCK = 64
SLC_TOPN = 16
WINDOW = 512
MOBA_BLOCK = 256
MOBA_TOPK = 3
N_EXPERTS = 32
N_GROUPS = 4
EXPERTS_PER_GROUP = N_EXPERTS // N_GROUPS
TOP_K_EXPERTS = 2
D_EXPERT = 512
RMS_EPS = 1e-6
NEG_INF = -1e30
FORCE = 1e9
LOWEST = -3e38
SCALE = HEAD_DIM ** -0.5
EXP2_SCALE = SCALE * 1.4426950408889634

LANE = 128
Q_TILE = 256
MOBA_Q_TILE = 512
SLC_KEY_TILE = 512
CMP_BAND = 256
CMP_ROW_TILE = 256
MOE_ROWS = 256
GATHER_ROWS = 256
DMA_UNROLL = 8
PROJ_TM = 1024
PROJ_TN = 512
OUT_TM = 256
VMEM_LIMIT = 56 * 1024 * 1024

COL_QN = 0
COL_KC = 8
COL_VC = 10
COL_KS = 12
COL_VS = 14
COL_KW = 16
COL_VW = 18
COL_QM = 20
COL_KM = 28
COL_VM = 36
PROJ_COLS = 44
ROPE_FLAGS = tuple(
    1 if (c < 8 or c in (12, 13, 16, 17) or 20 <= c < 36) else 0 for c in range(PROJ_COLS))


def _cparams(sem):
    return pltpu.CompilerParams(dimension_semantics=sem, vmem_limit_bytes=VMEM_LIMIT)


def _dot(a, b):
    return jnp.dot(a, b, preferred_element_type=F32)


def _dot_nt(a, b):
    return lax.dot_general(a, b, (((1,), (1,)), ((), ())), preferred_element_type=F32)


def _split_bf16(v):
    hi = v.astype(BF16)
    lo = (v - hi.astype(F32)).astype(BF16)
    return hi, lo


def _mod_kernel(c_ref, w_ref, b_ref, o_ref):
    c = c_ref[...]
    cond = c * jax.nn.sigmoid(c)
    o_ref[0] = jnp.sum(w_ref[0] * cond, axis=0, keepdims=True) + b_ref[0]


def _ada_mod(c, w_ada, b_ada):
    depth, d, n = w_ada.shape
    tn = 1024
    return pl.pallas_call(
        _mod_kernel,
        grid=(depth, n // tn),
        in_specs=[
            pl.BlockSpec((d, 1), lambda l, j: (0, 0)),
            pl.BlockSpec((1, d, tn), lambda l, j: (l, 0, j)),
            pl.BlockSpec((1, 1, tn), lambda l, j: (l, 0, j)),
        ],
        out_specs=pl.BlockSpec((1, 1, tn), lambda l, j: (l, 0, j)),
        out_shape=jax.ShapeDtypeStruct((depth, 1, n), F32),
        compiler_params=_cparams(("arbitrary", "arbitrary")),
        name="ada_mod",
    )(c.reshape(d, 1), w_ada, b_ada.reshape(depth, 1, n))


def _rms_mod(x, g, sc, sh):
    y = x * lax.rsqrt(jnp.mean(x * x, axis=-1, keepdims=True) + RMS_EPS)
    return (y * g) * (1.0 + sc) + sh


def _proj_kernel(flags_ref, x_ref, g_ref, sc_ref, sh_ref, w_ref, wg_ref, cos_ref, sin_ref,
                 o_ref, gate_ref, h_scr):
    j = pl.program_id(1)

    @pl.when(j == 0)
    def _():
        hb = _rms_mod(x_ref[...], g_ref[...], sc_ref[...], sh_ref[...]).astype(BF16)
        h_scr[...] = hb
        gate_ref[...] = _dot(hb, wg_ref[...])

    acc = _dot(h_scr[...], w_ref[...])
    cos = cos_ref[...]
    sin = sin_ref[...]
    nch = PROJ_TN // LANE
    for c in range(nch):
        a = acc[:, c * LANE:(c + 1) * LANE]
        roped = a * cos + pltpu.roll(a, HEAD_DIM // 2, 1) * sin
        f = flags_ref[j * nch + c]
        o_ref[:, c * LANE:(c + 1) * LANE] = jnp.where(f > 0, roped, a).astype(BF16)


def _project(x, g, sc, sh, w_main, w_gate, cos, sin, flags):
    s, d = x.shape
    n = w_main.shape[1]
    tm = min(PROJ_TM, s)
    grid_spec = pltpu.PrefetchScalarGridSpec(
        num_scalar_prefetch=1,
        grid=(s // tm, n // PROJ_TN),
        in_specs=[
            pl.BlockSpec((tm, d), lambda i, j, f: (i, 0)),
            pl.BlockSpec((1, d), lambda i, j, f: (0, 0)),
            pl.BlockSpec((1, d), lambda i, j, f: (0, 0)),
            pl.BlockSpec((1, d), lambda i, j, f: (0, 0)),
            pl.BlockSpec((d, PROJ_TN), lambda i, j, f: (0, j)),
            pl.BlockSpec((d, 2 * LANE), lambda i, j, f: (0, 0)),
            pl.BlockSpec((tm, LANE), lambda i, j, f: (i, 0)),
            pl.BlockSpec((tm, LANE), lambda i, j, f: (i, 0)),
        ],
        out_specs=[
            pl.BlockSpec((tm, PROJ_TN), lambda i, j, f: (i, j)),
            pl.BlockSpec((tm, 2 * LANE), lambda i, j, f: (i, 0)),
        ],
        scratch_shapes=[pltpu.VMEM((tm, d), BF16)],
    )
    return pl.pallas_call(
        _proj_kernel,
        grid_spec=grid_spec,
        out_shape=[jax.ShapeDtypeStruct((s, n), BF16), jax.ShapeDtypeStruct((s, 2 * LANE), F32)],
        compiler_params=_cparams(("arbitrary", "arbitrary")),
        name="norm_proj_rope",
    )(flags, x, g, sc, sh, w_main, w_gate, cos, sin)


def _compress_kernel(ch_ref, w1_ref, b1_ref, w2_ref, pe_ref, cos_ref, sin_ref, o_ref):
    a = pl.program_id(0)
    ch = ch_ref[0]
    nc = ch.shape[0]
    half = CMP_STRIDE * HEAD_DIM
    w1 = w1_ref[0].astype(BF16)
    top = _dot(ch, w1[:half])
    bot = _dot(ch, w1[half:])
    bot = pltpu.roll(bot, nc - 1, 0)
    pe_term = _dot(pe_ref[0].astype(BF16), w1)[0:1]
    hid = jax.nn.gelu(top + bot + pe_term + b1_ref[0])
    out = _dot(hid.astype(BF16), w2_ref[0].astype(BF16))
    roped = out * cos_ref[...] + pltpu.roll(out, HEAD_DIM // 2, 1) * sin_ref[...]
    out = jnp.where(a < N_KV_NSA, roped, out)
    row = lax.broadcasted_iota(jnp.int32, out.shape, 0)
    o_ref[0] = jnp.where(row < nc - 1, out, 0.0).astype(BF16)


def _compress(chunks, w1, b1, w2, pe8, cos_c, sin_c):
    na, nc, kw = chunks.shape
    return pl.pallas_call(
        _compress_kernel,
        grid=(na,),
        in_specs=[
            pl.BlockSpec((1, nc, kw), lambda a: (a, 0, 0)),
            pl.BlockSpec((1, CMP_LEN * HEAD_DIM, CMP_HIDDEN), lambda a: (a // N_KV_NSA, 0, 0)),
            pl.BlockSpec((1, 1, CMP_HIDDEN), lambda a: (a // N_KV_NSA, 0, 0)),
            pl.BlockSpec((1, CMP_HIDDEN, HEAD_DIM), lambda a: (a // N_KV_NSA, 0, 0)),
            pl.BlockSpec((1, 8, CMP_LEN * HEAD_DIM), lambda a: (a // N_KV_NSA, 0, 0)),
            pl.BlockSpec((nc, HEAD_DIM), lambda a: (0, 0)),
            pl.BlockSpec((nc, HEAD_DIM), lambda a: (0, 0)),
        ],
        out_specs=pl.BlockSpec((1, nc, HEAD_DIM), lambda a: (a, 0, 0)),
        out_shape=jax.ShapeDtypeStruct((na, nc, HEAD_DIM), BF16),
        compiler_params=_cparams(("arbitrary",)),
        name="nsa_compress",
    )(chunks, w1, b1, w2, pe8, cos_c, sin_c)


def _nsa_cmp_band(q_ref, kc_ref, vc_ref, ov_ref, gate_ref, ocmp_ref, sel_ref, psum_scr, imp_scr, cw, sw):
    i = pl.program_id(1)
    tq = q_ref.shape[0]
    nsb = sel_ref.shape[2]
    t = i * tq + lax.broadcasted_iota(jnp.int32, (tq, 1), 0)
    cend = lax.broadcasted_iota(jnp.int32, (1, cw), 1) * CMP_STRIDE + (CMP_LEN - 1)
    mask = cend <= t
    kc = kc_ref[0, :cw, :]
    vc = vc_ref[0, :cw, :]
    gates = jax.nn.sigmoid(gate_ref[...])
    for r in range(NSA_REP):
        q = q_ref[:, r * HEAD_DIM:(r + 1) * HEAD_DIM]
        s = jnp.where(mask, _dot_nt(q, kc) * SCALE, NEG_INF)
        m = jnp.max(s, axis=-1, keepdims=True)
        e = jnp.where(mask, jnp.exp(s - m), 0.0)
        l = jnp.sum(e, axis=-1, keepdims=True)
        p = e / jnp.where(l > 0.0, l, 1.0)
        psum_scr[:, :cw] = p if r == 0 else psum_scr[:, :cw] + p
        o = _dot(p.astype(BF16), vc)
        ocmp_ref[:, r * HEAD_DIM:(r + 1) * HEAD_DIM] = gates[:, 3 * r:3 * r + 1] * o

    p_hi, p_lo = _split_bf16(psum_scr[:, :cw])
    ov = ov_ref[:cw, :sw]
    imp_scr[:, :sw] = _dot(p_hi, ov) + _dot(p_lo, ov)

    rt = min(CMP_ROW_TILE, tq)
    jblk = lax.broadcasted_iota(jnp.int32, (1, sw), 1)
    jf = jblk.astype(F32)

    def choose(rr, carry):
        r0 = pl.multiple_of(rr * rt, rt)
        tb = lax.shift_right_logical(i * tq + r0 + lax.broadcasted_iota(jnp.int32, (rt, 1), 0),
                                     SLC_BLOCK.bit_length() - 1)
        forced = (jblk == 0) | (jblk == tb) | (jblk == tb - 1)
        work = jnp.where(jblk <= tb, jnp.where(forced, FORCE, imp_scr[pl.ds(r0, rt), :sw]), NEG_INF)
        sel = jnp.zeros((rt, sw), F32)
        for _ in range(min(SLC_TOPN, nsb)):
            m = jnp.max(work, axis=-1, keepdims=True)
            first = jnp.min(jnp.where(work == m, jf, 1e9), axis=-1, keepdims=True)
            hit = jf == first
            sel = jnp.where(hit, 1.0, sel)
            work = jnp.where(hit, LOWEST, work)
        sel_ref[0, pl.ds(r0, rt), :sw] = sel.astype(BF16)
        return carry

    lax.fori_loop(0, tq // rt, choose, 0)
    if sw < nsb:
        sel_ref[0, :, sw:] = jnp.zeros((tq, nsb - sw), BF16)


def _nsa_cmp_kernel(q_ref, kc_ref, vc_ref, ov_ref, gate_ref, ocmp_ref, sel_ref, psum_scr, imp_scr):
    i = pl.program_id(1)
    tq = q_ref.shape[0]
    ncp = kc_ref.shape[1]
    nsb = sel_ref.shape[2]
    band_w = min(CMP_BAND, ncp)
    n_band = ncp // band_w
    band = jnp.minimum(((i + 1) * (tq // CMP_STRIDE) + band_w - 1) // band_w, n_band) - 1
    for b in range(n_band):
        cw = band_w * (b + 1)
        sw = min(nsb, -(-(cw * CMP_STRIDE // SLC_BLOCK) // LANE) * LANE)

        @pl.when(band == b)
        def _():
            _nsa_cmp_band(q_ref, kc_ref, vc_ref, ov_ref, gate_ref, ocmp_ref, sel_ref, psum_scr, imp_scr, cw, sw)


def _nsa_cmp(proj, cmp_kv, overlap, gates):
    s = proj.shape[0]
    ncp = cmp_kv.shape[1]
    nsb = overlap.shape[1]
    tq = min(Q_TILE, s)
    gw = NSA_REP * HEAD_DIM
    assert ncp % min(CMP_BAND, ncp) == 0
    return pl.pallas_call(
        _nsa_cmp_kernel,
        grid=(N_KV_NSA, s // tq),
        in_specs=[
            pl.BlockSpec((tq, gw), lambda g, i: (i, g)),
            pl.BlockSpec((1, ncp, HEAD_DIM), lambda g, i: (g, 0, 0)),
            pl.BlockSpec((1, ncp, HEAD_DIM), lambda g, i: (N_KV_NSA + g, 0, 0)),
            pl.BlockSpec((ncp, nsb), lambda g, i: (0, 0)),
            pl.BlockSpec((tq, LANE), lambda g, i: (i, g)),
        ],
        out_specs=[
            pl.BlockSpec((tq, gw), lambda g, i: (i, g)),
            pl.BlockSpec((1, tq, nsb), lambda g, i: (g, i, 0)),
        ],
        out_shape=[jax.ShapeDtypeStruct((s, NSA_WIDTH), F32),
                   jax.ShapeDtypeStruct((N_KV_NSA, s, nsb), BF16)],
        scratch_shapes=[pltpu.VMEM((tq, ncp), F32), pltpu.VMEM((tq, nsb), F32)],
        compiler_params=_cparams(("arbitrary", "arbitrary")),
        name="nsa_cmp_select",
    )(proj, cmp_kv, cmp_kv, overlap, gates)


def _lane_chunks(s):
    return [s[:, c * LANE:(c + 1) * LANE] for c in range(s.shape[1] // LANE)]


def _row_max(chunks):
    m = functools.reduce(jnp.maximum, chunks)
    return jnp.broadcast_to(jnp.max(m, axis=-1, keepdims=True), m.shape)


def _softmax_weights(chunks, m):
    return jnp.concatenate([jnp.exp2((c - m) * EXP2_SCALE) for c in chunks], axis=1).astype(BF16)


def _flash_init(s, v_aug, m_scr, acc_scr, idx):
    chunks = _lane_chunks(s)
    m = _row_max(chunks)
    acc_scr[idx] = _dot(_softmax_weights(chunks, m), v_aug)
    m_scr[idx] = m


def _flash_step(s, v_aug, m_scr, acc_scr, idx):
    chunks = _lane_chunks(s)
    m_old = m_scr[idx]
    m_new = jnp.maximum(m_old, _row_max(chunks))
    alpha = jnp.exp2((m_old - m_new) * EXP2_SCALE)
    pv = _dot(_softmax_weights(chunks, m_new), v_aug)
    acc = acc_scr[idx]
    acc_scr[idx] = jnp.concatenate([alpha * a for a in _lane_chunks(acc)], axis=1) + pv
    m_scr[idx] = m_new


def _with_ones(v):
    return jnp.concatenate([v, jnp.ones(v.shape, v.dtype)], axis=1)


def _block_onehot(k0, tk, block, width):
    pos = k0 + lax.broadcasted_iota(jnp.int32, (tk, 1), 0)
    blk = lax.shift_right_logical(pos, block.bit_length() - 1) & (width - 1)
    return (blk == lax.broadcasted_iota(jnp.int32, (1, width), 1)).astype(BF16)


def _nsa_slc_kernel(q_ref, ks_ref, vs_ref, kw0_ref, kw1_ref, kw2_ref, vw0_ref, vw1_ref, vw2_ref,
                    sel_ref, gate_ref, ocmp_ref, o_ref, ka_scr, va_scr, qa_scr, m_scr, acc_scr, sa_scr, sb_scr):
    i = pl.program_id(1)
    tq = q_ref.shape[0]
    nsb = sel_ref.shape[2]
    s_len = ks_ref.shape[0]
    tk = min(SLC_KEY_TILE, s_len)
    n_kt = s_len // tk
    width = min(nsb, LANE)
    nhalf = max(nsb // LANE, 1)
    kt_per_half = (LANE * SLC_BLOCK) // tk
    t = i * tq + lax.broadcasted_iota(jnp.int32, (tq, 1), 0)
    kt_diag = (i * tq) // tk

    @pl.when(i == 0)
    def _():
        def build(n, carry):
            k0 = pl.multiple_of(n * tk, tk)
            ka_scr[pl.ds(k0, tk), :] = jnp.concatenate(
                [ks_ref[pl.ds(k0, tk), :], _block_onehot(k0, tk, SLC_BLOCK, width)], axis=1)
            va_scr[pl.ds(k0, tk), :] = _with_ones(vs_ref[pl.ds(k0, tk), :])
            return carry

        lax.fori_loop(0, n_kt, build, 0)
        ka_scr[pl.ds(s_len, tk), :] = jnp.concatenate(
            [jnp.zeros((tk, HEAD_DIM), BF16), jnp.ones((tk, width), BF16)], axis=1)
        va_scr[pl.ds(s_len, tk), :] = jnp.zeros((tk, 2 * HEAD_DIM), BF16)

    bias = ((1.0 - sel_ref[0].astype(F32)) * NEG_INF).astype(BF16)
    for half in range(nhalf):
        for r in range(NSA_REP):
            qa_scr[half * NSA_REP + r] = jnp.concatenate(
                [q_ref[:, r * HEAD_DIM:(r + 1) * HEAD_DIM], bias[:, half * width:(half + 1) * width]], axis=1)

    def scores(kt, r):
        half = jnp.where(kt < n_kt, kt // kt_per_half, 0)
        return _dot_nt(qa_scr[half * NSA_REP + r], ka_scr[pl.ds(pl.multiple_of(kt * tk, tk), tk), :])

    def values(kt):
        return va_scr[pl.ds(pl.multiple_of(kt * tk, tk), tk), :]

    kpos = kt_diag * tk + lax.broadcasted_iota(jnp.int32, (1, tk), 1)
    for r in range(NSA_REP):
        _flash_init(jnp.where(kpos <= t, scores(kt_diag, r), NEG_INF), values(kt_diag), m_scr, acc_scr, r)

    last = jnp.maximum(kt_diag - 1, 0)
    for r in range(NSA_REP):
        sa_scr[r] = scores(0, r)

    def tile_pair(j, carry):
        kt_b = jnp.where(2 * j + 1 < kt_diag, 2 * j + 1, n_kt)
        for r in range(NSA_REP):
            sb_scr[r] = scores(kt_b, r)
        for r in range(NSA_REP):
            _flash_step(sa_scr[r], values(2 * j), m_scr, acc_scr, r)
        for r in range(NSA_REP):
            sa_scr[r] = scores(jnp.minimum(2 * j + 2, last), r)
        for r in range(NSA_REP):
            _flash_step(sb_scr[r], values(kt_b), m_scr, acc_scr, r)
        return carry

    lax.fori_loop(0, (kt_diag + 1) // 2, tile_pair, 0)

    kwin = jnp.concatenate([kw0_ref[...], kw1_ref[...], kw2_ref[...]], axis=0)
    vwin = _with_ones(jnp.concatenate([vw0_ref[...], vw1_ref[...], vw2_ref[...]], axis=0))
    nw = kwin.shape[0]
    wpos = (i - 2) * tq + lax.broadcasted_iota(jnp.int32, (1, nw), 1)
    wvalid = (wpos <= t) & (wpos > t - WINDOW) & (wpos >= 0)
    gates = jax.nn.sigmoid(gate_ref[...])
    for r in range(NSA_REP):
        q = q_ref[:, r * HEAD_DIM:(r + 1) * HEAD_DIM]
        s = jnp.where(wvalid, _dot_nt(q, kwin), NEG_INF)
        m = jnp.max(s, axis=-1, keepdims=True)
        ow = _dot(jnp.exp2((s - m) * EXP2_SCALE).astype(BF16), vwin)
        o_win = ow[:, :HEAD_DIM] / ow[:, HEAD_DIM:]
        acc = acc_scr[r]
        o_slc = acc[:, :HEAD_DIM] / acc[:, HEAD_DIM:]
        o = (ocmp_ref[:, r * HEAD_DIM:(r + 1) * HEAD_DIM]
             + gates[:, 3 * r + 1:3 * r + 2] * o_slc + gates[:, 3 * r + 2:3 * r + 3] * o_win)
        o_ref[:, r * HEAD_DIM:(r + 1) * HEAD_DIM] = o.astype(BF16)


def _nsa_slc(proj, sel, gates, ocmp):
    s = proj.shape[0]
    nsb = sel.shape[2]
    tq = min(Q_TILE, s)
    tk = min(SLC_KEY_TILE, s)
    gw = NSA_REP * HEAD_DIM
    assert WINDOW == 2 * tq and (nsb <= LANE or nsb % LANE == 0)

    def win_spec(col, d):
        return pl.BlockSpec((tq, HEAD_DIM), lambda g, i: (jnp.maximum(i - 2 + d, 0), col + g))

    return pl.pallas_call(
        _nsa_slc_kernel,
        grid=(N_KV_NSA, s // tq),
        in_specs=[
            pl.BlockSpec((tq, gw), lambda g, i: (i, g)),
            pl.BlockSpec((s, HEAD_DIM), lambda g, i: (0, COL_KS + g)),
            pl.BlockSpec((s, HEAD_DIM), lambda g, i: (0, COL_VS + g)),
            win_spec(COL_KW, 0), win_spec(COL_KW, 1), win_spec(COL_KW, 2),
            win_spec(COL_VW, 0), win_spec(COL_VW, 1), win_spec(COL_VW, 2),
            pl.BlockSpec((1, tq, nsb), lambda g, i: (g, i, 0)),
            pl.BlockSpec((tq, LANE), lambda g, i: (i, g)),
            pl.BlockSpec((tq, gw), lambda g, i: (i, g)),
        ],
        out_specs=pl.BlockSpec((tq, gw), lambda g, i: (i, g)),
        out_shape=jax.ShapeDtypeStruct((s, NSA_WIDTH), BF16),
        scratch_shapes=[pltpu.VMEM((s + tk, HEAD_DIM + min(nsb, LANE)), BF16),
                        pltpu.VMEM((s + tk, 2 * HEAD_DIM), BF16),
                        pltpu.VMEM((max(nsb // LANE, 1) * NSA_REP, tq, HEAD_DIM + min(nsb, LANE)), BF16),
                        pltpu.VMEM((NSA_REP, tq, LANE), F32),
                        pltpu.VMEM((NSA_REP, tq, 2 * HEAD_DIM), F32),
                        pltpu.VMEM((NSA_REP, tq, tk), F32), pltpu.VMEM((NSA_REP, tq, tk), F32)],
        compiler_params=_cparams(("arbitrary", "arbitrary")),
        name="nsa_slc_win",
    )(proj, proj, proj, proj, proj, proj, proj, proj, proj, sel, gates, ocmp)


def _moba_kernel(q_ref, k_ref, v_ref, o_ref, ka_scr, va_scr, qa_scr, km_scr, m_scr, acc_scr, sa_scr, sb_scr):
    i = pl.program_id(1)
    tq = q_ref.shape[0]
    s_len = k_ref.shape[0]
    nbp = km_scr.shape[0]
    bpt = tq // MOBA_BLOCK
    n_kt = s_len // tq

    @pl.when(i == 0)
    def _():
        km_scr[...] = jnp.zeros(km_scr.shape, F32)

        def build(n, carry):
            k0 = pl.multiple_of(n * tq, tq)
            kt = k_ref[pl.ds(k0, tq), :]
            ka_scr[pl.ds(k0, tq), :] = jnp.concatenate([kt, _block_onehot(k0, tq, MOBA_BLOCK, nbp)], axis=1)
            va_scr[pl.ds(k0, tq), :] = _with_ones(v_ref[pl.ds(k0, tq), :])
            for j in range(bpt):
                kb = kt[j * MOBA_BLOCK:(j + 1) * MOBA_BLOCK].astype(F32)
                km_scr[pl.ds(n * bpt + j, 1), :] = jnp.mean(kb, axis=0, keepdims=True)
            return carry

        lax.fori_loop(0, n_kt, build, 0)
        ka_scr[pl.ds(s_len, tq), :] = jnp.concatenate(
            [jnp.zeros((tq, HEAD_DIM), BF16), jnp.ones((tq, nbp), BF16)], axis=1)
        va_scr[pl.ds(s_len, tq), :] = jnp.zeros((tq, 2 * HEAD_DIM), BF16)

    q = q_ref[...]
    km_hi, km_lo = _split_bf16(km_scr[...])
    gate = _dot_nt(q, km_hi) + _dot_nt(q, km_lo)
    t = i * tq + lax.broadcasted_iota(jnp.int32, (tq, 1), 0)
    own = lax.shift_right_logical(t, MOBA_BLOCK.bit_length() - 1)
    nidx = lax.broadcasted_iota(jnp.int32, (1, nbp), 1)
    nf = nidx.astype(F32)
    work = jnp.where(nidx < own, gate, LOWEST)
    sel = nidx == own
    for _ in range(MOBA_TOPK):
        m = jnp.max(work, axis=-1, keepdims=True)
        first = jnp.min(jnp.where((work == m) & (m > 0.5 * LOWEST), nf, 1e9), axis=-1, keepdims=True)
        hit = nf == first
        sel = sel | hit
        work = jnp.where(hit, LOWEST, work)
    qa_scr[...] = jnp.concatenate([q, jnp.where(sel, 0.0, NEG_INF).astype(BF16)], axis=1)

    def scores(kt):
        return _dot_nt(qa_scr[...], ka_scr[pl.ds(pl.multiple_of(kt * tq, tq), tq), :])

    def values(kt):
        return va_scr[pl.ds(pl.multiple_of(kt * tq, tq), tq), :]

    kpos = i * tq + lax.broadcasted_iota(jnp.int32, (1, tq), 1)
    _flash_init(jnp.where(kpos <= t, scores(i), NEG_INF), values(i), m_scr, acc_scr, Ellipsis)

    last = jnp.maximum(i - 1, 0)
    sa_scr[...] = scores(0)

    def tile_pair(j, carry):
        kt_b = jnp.where(2 * j + 1 < i, 2 * j + 1, n_kt)
        sb_scr[...] = scores(kt_b)
        _flash_step(sa_scr[...], values(2 * j), m_scr, acc_scr, Ellipsis)
        sa_scr[...] = scores(jnp.minimum(2 * j + 2, last))
        _flash_step(sb_scr[...], values(kt_b), m_scr, acc_scr, Ellipsis)
        return carry

    lax.fori_loop(0, (i + 1) // 2, tile_pair, 0)
    acc = acc_scr[...]
    o_ref[...] = (acc[:, :HEAD_DIM] / acc[:, HEAD_DIM:]).astype(BF16)


def _moba(proj):
    s = proj.shape[0]
    tq = min(MOBA_Q_TILE, s)
    nbp = max(LANE, s // MOBA_BLOCK)
    assert nbp & (nbp - 1) == 0
    return pl.pallas_call(
        _moba_kernel,
        grid=(N_HEADS_MOBA, s // tq),
        in_specs=[
            pl.BlockSpec((tq, HEAD_DIM), lambda h, i: (i, COL_QM + h)),
            pl.BlockSpec((s, HEAD_DIM), lambda h, i: (0, COL_KM + h)),
            pl.BlockSpec((s, HEAD_DIM), lambda h, i: (0, COL_VM + h)),
        ],
        out_specs=pl.BlockSpec((tq, HEAD_DIM), lambda h, i: (i, h)),
        out_shape=jax.ShapeDtypeStruct((s, MOBA_WIDTH), BF16),
        scratch_shapes=[pltpu.VMEM((s + tq, HEAD_DIM + nbp), BF16), pltpu.VMEM((s + tq, 2 * HEAD_DIM), BF16),
                        pltpu.VMEM((tq, HEAD_DIM + nbp), BF16),
                        pltpu.VMEM((nbp, HEAD_DIM), F32), pltpu.VMEM((tq, LANE), F32),
                        pltpu.VMEM((tq, 2 * HEAD_DIM), F32),
                        pltpu.VMEM((tq, tq), F32), pltpu.VMEM((tq, tq), F32)],
        compiler_params=_cparams(("arbitrary", "arbitrary")),
        name="moba",
    )(proj, proj, proj)


def _out_kernel(on_ref, om_ref, x_ref, wn_ref, wm_ref, gt_ref, g_ref, sc_ref, sh_ref, wrh_ref, wrl_ref,
                x1_ref, h_ref, lg_ref):
    mix = _dot(on_ref[...], wn_ref[...]) + _dot(om_ref[...], wm_ref[...])
    x1 = x_ref[...] + gt_ref[...] * mix
    x1_ref[...] = x1
    h = _rms_mod(x1, g_ref[...], sc_ref[...], sh_ref[...])
    h_hi, h_lo = _split_bf16(h)
    h_ref[...] = h
    wrh = wrh_ref[...]
    lg_ref[...] = _dot(h_hi, wrh) + _dot(h_lo, wrh) + _dot(h_hi, wrl_ref[...])


def _out_proj(o_nsa, o_moba, x, w_n, w_m, gt, g, sc, sh, wr_hi, wr_lo):
    s, d = x.shape
    tm = min(OUT_TM, s)
    row = lambda i: (i, 0)
    fix = lambda i: (0, 0)
    return pl.pallas_call(
        _out_kernel,
        grid=(s // tm,),
        in_specs=[
            pl.BlockSpec((tm, NSA_WIDTH), row),
            pl.BlockSpec((tm, MOBA_WIDTH), row),
            pl.BlockSpec((tm, d), row),
            pl.BlockSpec((NSA_WIDTH, d), fix),
            pl.BlockSpec((MOBA_WIDTH, d), fix),
            pl.BlockSpec((1, d), fix),
            pl.BlockSpec((1, d), fix),
            pl.BlockSpec((1, d), fix),
            pl.BlockSpec((1, d), fix),
            pl.BlockSpec((d, LANE), fix),
            pl.BlockSpec((d, LANE), fix),
        ],
        out_specs=[pl.BlockSpec((tm, d), row), pl.BlockSpec((tm, d), row), pl.BlockSpec((tm, LANE), row)],
        out_shape=[jax.ShapeDtypeStruct((s, d), F32), jax.ShapeDtypeStruct((s, d), F32),
                   jax.ShapeDtypeStruct((s, LANE), F32)],
        compiler_params=_cparams(("arbitrary",)),
        name="out_proj_norm_router",
    )(o_nsa, o_moba, x, w_n, w_m, gt, g, sc, sh, wr_hi, wr_lo)


def _row_copy(src_ref, dst_ref, src_row, dst_row, sem):
    return pltpu.make_async_copy(src_ref.at[pl.ds(src_row, 1)], dst_ref.at[pl.ds(dst_row, 1)], sem)


def _expert_kernel(be_ref, nu_ref, tok_ref, h_ref, wg_ref, wu_ref, wd_ref, sw_ref, y_ref,
                   x_buf, wg_s, wu_s, wd_s, sems):
    b = pl.program_id(0)
    n_used = nu_ref[0]

    def gather(blk, buf, wait):
        def row(r, carry):
            cp = _row_copy(h_ref, x_buf.at[buf], tok_ref[blk * MOE_ROWS + r], r, sems.at[buf])
            if wait:
                cp.wait()
            else:
                cp.start()
            return carry

        lax.fori_loop(0, MOE_ROWS, row, 0, unroll=DMA_UNROLL)

    @pl.when((b == 0) & (n_used > 0))
    def _():
        gather(0, 0, False)

    @pl.when(b + 1 < n_used)
    def _():
        gather(b + 1, (b + 1) % 2, False)

    prev = be_ref[jnp.maximum(b - 1, 0)]

    @pl.when((b == 0) | (be_ref[b] != prev))
    def _():
        wg_s[...] = wg_ref[0, 0].astype(BF16)
        wu_s[...] = wu_ref[0, 0].astype(BF16)
        wd_s[...] = wd_ref[0, 0].astype(BF16)

    @pl.when(b < n_used)
    def _():
        gather(b, b % 2, True)
        x = x_buf[b % 2].astype(BF16)
        a = _dot(x, wg_s[...])
        u = _dot(x, wu_s[...])
        hid = (a * jax.nn.sigmoid(a) * u).astype(BF16)
        y_ref[...] = _dot(hid, wd_s[...]) * sw_ref[...]

    @pl.when(b >= n_used)
    def _():
        y_ref[...] = jnp.zeros(y_ref.shape, F32)


def _experts(h, slot_tok, blk_exp, n_used, w_gate, w_up, w_down, slot_w, layer):
    n_slots = slot_tok.shape[0]
    d = h.shape[1]
    n_blk = n_slots // MOE_ROWS
    grid_spec = pltpu.PrefetchScalarGridSpec(
        num_scalar_prefetch=3,
        grid=(n_blk,),
        in_specs=[
            pl.BlockSpec(memory_space=pl.ANY),
            pl.BlockSpec((1, 1, d, D_EXPERT), lambda b, be, nu, tok: (layer, be[b], 0, 0)),
            pl.BlockSpec((1, 1, d, D_EXPERT), lambda b, be, nu, tok: (layer, be[b], 0, 0)),
            pl.BlockSpec((1, 1, D_EXPERT, d), lambda b, be, nu, tok: (layer, be[b], 0, 0)),
            pl.BlockSpec((MOE_ROWS, 1), lambda b, be, nu, tok: (b, 0)),
        ],
        out_specs=pl.BlockSpec((MOE_ROWS, d), lambda b, be, nu, tok: (b, 0)),
        scratch_shapes=[pltpu.VMEM((2, MOE_ROWS, d), F32),
                        pltpu.VMEM((d, D_EXPERT), BF16), pltpu.VMEM((d, D_EXPERT), BF16),
                        pltpu.VMEM((D_EXPERT, d), BF16), pltpu.SemaphoreType.DMA((2,))],
    )
    return pl.pallas_call(
        _expert_kernel,
        grid_spec=grid_spec,
        out_shape=jax.ShapeDtypeStruct((n_slots, d), F32),
        compiler_params=_cparams(("arbitrary",)),
        name="moe_experts",
    )(blk_exp, n_used, slot_tok, h, w_gate, w_up, w_down, slot_w.reshape(n_slots, 1))


def _combine_kernel(sa_ref, sb_ref, na_ref, nb_ref, x_ref, gt_ref, gf_ref, y_ref, o_ref, buf, sems, *, final):
    b = pl.program_id(0)
    tt = x_ref.shape[0]

    def gather(ia_ref, ib_ref, slot, wait):
        def row(r, carry):
            for k, idx_ref in enumerate((ia_ref, ib_ref)):
                cp = _row_copy(y_ref, buf.at[slot].at[k], idx_ref[0, 0, r], r, sems.at[slot].at[k])
                if wait:
                    cp.wait()
                else:
                    cp.start()
            return carry

        lax.fori_loop(0, tt, row, 0, unroll=DMA_UNROLL)

    @pl.when(b == 0)
    def _():
        gather(sa_ref, sb_ref, 0, False)

    @pl.when(b + 1 < pl.num_programs(0))
    def _():
        gather(na_ref, nb_ref, (b + 1) % 2, False)

    gather(sa_ref, sb_ref, b % 2, True)
    x2 = x_ref[...] + gt_ref[...] * (buf[b % 2, 0] + buf[b % 2, 1])
    if final:
        x2 = x2 * lax.rsqrt(jnp.mean(x2 * x2, axis=-1, keepdims=True) + RMS_EPS) * gf_ref[...]
    o_ref[...] = x2


def _combine(x1, ys, slot_a, slot_b, gt, g_final, final):
    s, d = x1.shape
    tt = min(GATHER_ROWS, s)
    nt = s // tt
    cur_spec = pl.BlockSpec((1, 1, tt), lambda b: (b, 0, 0), memory_space=pltpu.SMEM)
    nxt_spec = pl.BlockSpec((1, 1, tt), lambda b: (jnp.minimum(b + 1, nt - 1), 0, 0), memory_space=pltpu.SMEM)
    slot_a = slot_a.reshape(nt, 1, tt)
    slot_b = slot_b.reshape(nt, 1, tt)
    return pl.pallas_call(
        functools.partial(_combine_kernel, final=final),
        grid=(nt,),
        in_specs=[
            cur_spec, cur_spec, nxt_spec, nxt_spec,
            pl.BlockSpec((tt, d), lambda b: (b, 0)),
            pl.BlockSpec((1, d), lambda b: (0, 0)),
            pl.BlockSpec((1, d), lambda b: (0, 0)),
            pl.BlockSpec(memory_space=pl.ANY),
        ],
        out_specs=pl.BlockSpec((tt, d), lambda b: (b, 0)),
        out_shape=jax.ShapeDtypeStruct((s, d), F32),
        scratch_shapes=[pltpu.VMEM((2, 2, tt, d), F32), pltpu.SemaphoreType.DMA((2, 2))],
        compiler_params=_cparams(("arbitrary",)),
        name="moe_combine",
    )(slot_a, slot_b, slot_a, slot_b, x1, gt, g_final, ys)


def _route(logits, b_router):
    n = logits.shape[0]
    scores = jax.nn.sigmoid(logits[:, :N_EXPERTS])
    biased = (scores + b_router.astype(F32)).reshape(n, N_GROUPS, EXPERTS_PER_GROUP)

    def top2(v):
        idx = lax.broadcasted_iota(jnp.int32, v.shape, v.ndim - 1)
        v1 = jnp.max(v, axis=-1, keepdims=True)
        i1 = jnp.min(jnp.where(v == v1, idx, v.shape[-1]), axis=-1, keepdims=True)
        rest = jnp.where(idx == i1, -jnp.inf, v)
        v2 = jnp.max(rest, axis=-1, keepdims=True)
        i2 = jnp.min(jnp.where(rest == v2, idx, v.shape[-1]), axis=-1, keepdims=True)
        return v1, v2, i1, i2

    g1, g2, _, _ = top2(biased)
    grp = jnp.argmax((g1 + g2)[..., 0], axis=-1)
    grp_hot = grp[:, None] == jnp.arange(N_GROUPS, dtype=grp.dtype)[None, :]
    in_grp = jnp.sum(jnp.where(grp_hot[:, :, None], biased, 0.0), axis=1)
    _, _, l1, l2 = top2(in_grp)
    expert = (grp[:, None] * EXPERTS_PER_GROUP + jnp.concatenate([l1, l2], axis=1)).astype(jnp.int32)
    e_hot = expert[:, :, None] == jnp.arange(N_EXPERTS, dtype=jnp.int32)[None, None, :]
    wsel = jnp.sum(jnp.where(e_hot, scores[:, None, :], 0.0), axis=-1)
    wsel = wsel / jnp.sum(wsel, axis=-1, keepdims=True)

    n_pair = n * TOP_K_EXPERTS
    flat_e = expert.reshape(-1)
    flat_t = jnp.repeat(jnp.arange(n, dtype=jnp.int32), TOP_K_EXPERTS)
    onehot = (flat_e[:, None] == jnp.arange(N_EXPERTS, dtype=jnp.int32)[None, :]).astype(jnp.int32)
    running = jnp.cumsum(onehot, axis=0)
    rank = jnp.sum(running * onehot, axis=1) - 1
    counts = running[-1]
    padded = (counts + MOE_ROWS - 1) // MOE_ROWS * MOE_ROWS
    pend = jnp.cumsum(padded)
    pstart = pend - padded
    slot = (jnp.sum(pstart[None, :] * onehot, axis=1) + rank).astype(jnp.int32)
    n_slots = n_pair + N_EXPERTS * MOE_ROWS
    slot_tok = jnp.zeros((n_slots,), jnp.int32).at[slot].set(flat_t)
    slot_w = jnp.zeros((n_slots,), F32).at[slot].set(wsel.reshape(-1))
    pair_slot = slot.reshape(n, TOP_K_EXPERTS)
    n_blk = n_slots // MOE_ROWS
    blk_exp = jnp.minimum(jnp.searchsorted(pend, jnp.arange(n_blk, dtype=jnp.int32) * MOE_ROWS, side='right'),
                          N_EXPERTS - 1).astype(jnp.int32)
    n_used = (pend[-1:] // MOE_ROWS).astype(jnp.int32)
    return slot_tok, slot_w, pair_slot, blk_exp, n_used


def _rope_tables(pos):
    inv = 1.0 / (ROPE_THETA ** (jnp.arange(0, HEAD_DIM, 2, dtype=F32) / HEAD_DIM))
    ang = pos.astype(F32)[:, None] * inv[None, :]
    cos = jnp.cos(ang)
    sin = jnp.sin(ang)
    return jnp.concatenate([cos, cos], axis=-1), jnp.concatenate([-sin, sin], axis=-1)


def _layout_w_in(w_in):
    o_g = NSA_WIDTH + 6 * NSA_KV_WIDTH
    main = jnp.concatenate([w_in[:, :o_g], w_in[:, o_g + N_GATES:]], axis=1).astype(BF16)
    wg = w_in[:, o_g:o_g + N_GATES].reshape(-1, N_KV_NSA, 3 * NSA_REP)
    wg = jnp.pad(wg, ((0, 0), (0, 0), (0, LANE - 3 * NSA_REP))).reshape(-1, N_KV_NSA * LANE)
    return main, wg.astype(BF16)


def _overlap(ncp, nsb):
    cs = jnp.arange(ncp, dtype=jnp.int32)[:, None] * CMP_STRIDE
    ss = jnp.arange(nsb, dtype=jnp.int32)[None, :] * SLC_BLOCK
    return ((cs + CMP_LEN - 1 >= ss) & (cs < ss + SLC_BLOCK)).astype(BF16)


def kernel(x, c, w_ada, b_ada, g_attn, g_ffn, w_in, w_out, cmp_pe, cmp_w1, cmp_b1, cmp_w2,
           w_router, b_router, w_gate, w_up, w_down, g_final):
    b, s, d = x.shape
    assert b == 1 and d == D_MODEL and s % max(Q_TILE, SLC_KEY_TILE, MOBA_Q_TILE) == 0
    depth = w_ada.shape[0]
    ncp = s // CMP_STRIDE
    nsb = s // SLC_BLOCK

    cos, sin = _rope_tables(jnp.arange(s))
    cos_c, sin_c = _rope_tables(jnp.arange(ncp) * CMP_STRIDE + CMP_LEN - 1)
    overlap = _overlap(ncp, nsb)
    flags = jnp.asarray(ROPE_FLAGS, jnp.int32)
    wr = jnp.pad(w_router.astype(F32), ((0, 0), (0, LANE - N_EXPERTS)))
    wr_hi, wr_lo = _split_bf16(wr)

    mod = _ada_mod(c.astype(F32), w_ada, b_ada)
    xs = x.reshape(s, d)
    for l in range(depth):
        sh1, sc1, gt1, sh2, sc2, gt2 = [mod[l, :, k * d:(k + 1) * d] for k in range(6)]
        w_main, w_gates = _layout_w_in(w_in[l])
        proj, gates = _project(xs, g_attn[l].reshape(1, d), sc1, sh1, w_main, w_gates, cos, sin, flags)

        chunks = proj[:, COL_KC * LANE:COL_KS * LANE].reshape(s, 2 * N_KV_NSA, HEAD_DIM)
        chunks = chunks.transpose(1, 0, 2).reshape(2 * N_KV_NSA, ncp, CMP_STRIDE * HEAD_DIM)
        pe8 = jnp.pad(cmp_pe[l].reshape(2, 1, CMP_LEN * HEAD_DIM), ((0, 0), (0, 7), (0, 0)))
        cmp_kv = _compress(chunks, cmp_w1[l], cmp_b1[l].reshape(2, 1, CMP_HIDDEN), cmp_w2[l], pe8, cos_c, sin_c)

        ocmp, sel = _nsa_cmp(proj, cmp_kv, overlap, gates)
        o_nsa = _nsa_slc(proj, sel, gates, ocmp)
        o_moba = _moba(proj)

        w_o = w_out[l].astype(BF16)
        x1, h2, logits = _out_proj(o_nsa, o_moba, xs, w_o[:NSA_WIDTH], w_o[NSA_WIDTH:], gt1,
                                   g_ffn[l].reshape(1, d), sc2, sh2, wr_hi, wr_lo)

        slot_tok, slot_w, pair_slot, blk_exp, n_used = _route(logits, b_router)
        ys = _experts(h2, slot_tok, blk_exp, n_used, w_gate, w_up, w_down, slot_w, l)
        xs = _combine(x1, ys, pair_slot[:, 0], pair_slot[:, 1], gt2, g_final.reshape(1, d), l == depth - 1)
    return xs.reshape(b, s, d)
```

```python
import functools

import jax
import jax.numpy as jnp
from jax import lax
from jax.experimental import pallas as pl
from jax.experimental.pallas import tpu as pltpu

F32 = jnp.float32
BF16 = jnp.bfloat16

D_MODEL = 2048
HEAD_DIM = 128
N_HEADS_NSA = 8
N_KV_NSA = 2
NSA_REP = N_HEADS_NSA // N_KV_NSA
N_HEADS_MOBA = 8
NSA_WIDTH = N_HEADS_NSA * HEAD_DIM
NSA_KV_WIDTH = N_KV_NSA * HEAD_DIM
MOBA_WIDTH = N_HEADS_MOBA * HEAD_DIM
N_GATES = 3 * N_HEADS_NSA
ROPE_THETA = 10000.0
CMP_LEN = 32
CMP_STRIDE = 16
CMP_HIDDEN = 256
SLC_BLOCK = 64
SLC_TOPN = 16
N_FORCED = 3
WINDOW = 512
MOBA_BLOCK = 256
MOBA_TOPK = 3
N_EXPERTS = 32
N_GROUPS = 4
EXPERTS_PER_GROUP = N_EXPERTS // N_GROUPS
TOP_K_EXPERTS = 2
D_EXPERT = 512
RMS_EPS = 1e-6
NEG_INF = -1e30
LOWEST = -3e38
SCALE = HEAD_DIM ** -0.5
EXP2_SCALE = SCALE * 1.4426950408889634

LANE = 128
Q_TILE = 256
MOBA_Q_TILE = 512
SLC_KEY_TILE = 512
CMP_BAND = 256
CMP_ROW_TILE = 256
MOE_ROWS = 256
GATHER_ROWS = 256
DMA_UNROLL = 8
PROJ_TM = 1024
PROJ_TN = 512
OUT_TM = 256
VMEM_LIMIT = 56 * 1024 * 1024

COL_QN = 0
COL_KC = 8
COL_VC = 10
COL_KS = 12
COL_VS = 14
COL_KW = 16
COL_VW = 18
COL_QM = 20
COL_KM = 28
COL_VM = 36
PROJ_COLS = 44
ROPE_FLAGS = tuple(
    1 if (c < 8 or c in (12, 13, 16, 17) or 20 <= c < 36) else 0 for c in range(PROJ_COLS))


def _cparams(sem):
    return pltpu.CompilerParams(dimension_semantics=sem, vmem_limit_bytes=VMEM_LIMIT)


def _dot(a, b):
    return jnp.dot(a, b, preferred_element_type=F32)


def _dot_nt(a, b):
    return lax.dot_general(a, b, (((1,), (1,)), ((), ())), preferred_element_type=F32)


def _split_bf16(v):
    hi = v.astype(BF16)
    lo = (v - hi.astype(F32)).astype(BF16)
    return hi, lo


def _mod_kernel(c_ref, w_ref, b_ref, o_ref):
    c = c_ref[...]
    cond = c * jax.nn.sigmoid(c)
    o_ref[0] = jnp.sum(w_ref[0] * cond, axis=0, keepdims=True) + b_ref[0]


def _ada_mod(c, w_ada, b_ada):
    depth, d, n = w_ada.shape
    tn = 1024
    return pl.pallas_call(
        _mod_kernel,
        grid=(depth, n // tn),
        in_specs=[
            pl.BlockSpec((d, 1), lambda l, j: (0, 0)),
            pl.BlockSpec((1, d, tn), lambda l, j: (l, 0, j)),
            pl.BlockSpec((1, 1, tn), lambda l, j: (l, 0, j)),
        ],
        out_specs=pl.BlockSpec((1, 1, tn), lambda l, j: (l, 0, j)),
        out_shape=jax.ShapeDtypeStruct((depth, 1, n), F32),
        compiler_params=_cparams(("arbitrary", "arbitrary")),
        name="ada_mod",
    )(c.reshape(d, 1), w_ada, b_ada.reshape(depth, 1, n))


def _rms_mod(x, g, sc, sh):
    y = x * lax.rsqrt(jnp.mean(x * x, axis=-1, keepdims=True) + RMS_EPS)
    return (y * g) * (1.0 + sc) + sh


def _proj_kernel(flags_ref, x_ref, g_ref, sc_ref, sh_ref, w_ref, wg_ref, cos_ref, sin_ref,
                 o_ref, gate_ref, h_scr):
    j = pl.program_id(1)

    @pl.when(j == 0)
    def _():
        hb = _rms_mod(x_ref[...], g_ref[...], sc_ref[...], sh_ref[...]).astype(BF16)
        h_scr[...] = hb
        gate_ref[...] = _dot(hb, wg_ref[...])

    acc = _dot(h_scr[...], w_ref[...])
    cos = cos_ref[...]
    sin = sin_ref[...]
    nch = PROJ_TN // LANE
    for c in range(nch):
        a = acc[:, c * LANE:(c + 1) * LANE]
        roped = a * cos + pltpu.roll(a, HEAD_DIM // 2, 1) * sin
        f = flags_ref[j * nch + c]
        o_ref[:, c * LANE:(c + 1) * LANE] = jnp.where(f > 0, roped, a).astype(BF16)


def _project(x, g, sc, sh, w_main, w_gate, cos, sin, flags):
    s, d = x.shape
    n = w_main.shape[1]
    tm = min(PROJ_TM, s)
    grid_spec = pltpu.PrefetchScalarGridSpec(
        num_scalar_prefetch=1,
        grid=(s // tm, n // PROJ_TN),
        in_specs=[
            pl.BlockSpec((tm, d), lambda i, j, f: (i, 0)),
            pl.BlockSpec((1, d), lambda i, j, f: (0, 0)),
            pl.BlockSpec((1, d), lambda i, j, f: (0, 0)),
            pl.BlockSpec((1, d), lambda i, j, f: (0, 0)),
            pl.BlockSpec((d, PROJ_TN), lambda i, j, f: (0, j)),
            pl.BlockSpec((d, 2 * LANE), lambda i, j, f: (0, 0)),
            pl.BlockSpec((tm, LANE), lambda i, j, f: (i, 0)),
            pl.BlockSpec((tm, LANE), lambda i, j, f: (i, 0)),
        ],
        out_specs=[
            pl.BlockSpec((tm, PROJ_TN), lambda i, j, f: (i, j)),
            pl.BlockSpec((tm, 2 * LANE), lambda i, j, f: (i, 0)),
        ],
        scratch_shapes=[pltpu.VMEM((tm, d), BF16)],
    )
    return pl.pallas_call(
        _proj_kernel,
        grid_spec=grid_spec,
        out_shape=[jax.ShapeDtypeStruct((s, n), BF16), jax.ShapeDtypeStruct((s, 2 * LANE), F32)],
        compiler_params=_cparams(("arbitrary", "arbitrary")),
        name="norm_proj_rope",
    )(flags, x, g, sc, sh, w_main, w_gate, cos, sin)


def _compress_kernel(ch_ref, w1_ref, b1_ref, w2_ref, pe_ref, cos_ref, sin_ref, o_ref):
    a = pl.program_id(0)
    ch = ch_ref[0]
    nc = ch.shape[0]
    half = CMP_STRIDE * HEAD_DIM
    w1 = w1_ref[0].astype(BF16)
    top = _dot(ch, w1[:half])
    bot = _dot(ch, w1[half:])
    bot = pltpu.roll(bot, nc - 1, 0)
    pe_term = _dot(pe_ref[0].astype(BF16), w1)[0:1]
    hid = jax.nn.gelu(top + bot + pe_term + b1_ref[0])
    out = _dot(hid.astype(BF16), w2_ref[0].astype(BF16))
    roped = out * cos_ref[...] + pltpu.roll(out, HEAD_DIM // 2, 1) * sin_ref[...]
    out = jnp.where(a < N_KV_NSA, roped, out)
    row = lax.broadcasted_iota(jnp.int32, out.shape, 0)
    o_ref[0] = jnp.where(row < nc - 1, out, 0.0).astype(BF16)


def _compress(chunks, w1, b1, w2, pe8, cos_c, sin_c):
    na, nc, kw = chunks.shape
    return pl.pallas_call(
        _compress_kernel,
        grid=(na,),
        in_specs=[
            pl.BlockSpec((1, nc, kw), lambda a: (a, 0, 0)),
            pl.BlockSpec((1, CMP_LEN * HEAD_DIM, CMP_HIDDEN), lambda a: (a // N_KV_NSA, 0, 0)),
            pl.BlockSpec((1, 1, CMP_HIDDEN), lambda a: (a // N_KV_NSA, 0, 0)),
            pl.BlockSpec((1, CMP_HIDDEN, HEAD_DIM), lambda a: (a // N_KV_NSA, 0, 0)),
            pl.BlockSpec((1, 8, CMP_LEN * HEAD_DIM), lambda a: (a // N_KV_NSA, 0, 0)),
            pl.BlockSpec((nc, HEAD_DIM), lambda a: (0, 0)),
            pl.BlockSpec((nc, HEAD_DIM), lambda a: (0, 0)),
        ],
        out_specs=pl.BlockSpec((1, nc, HEAD_DIM), lambda a: (a, 0, 0)),
        out_shape=jax.ShapeDtypeStruct((na, nc, HEAD_DIM), BF16),
        compiler_params=_cparams(("arbitrary",)),
        name="nsa_compress",
    )(chunks, w1, b1, w2, pe8, cos_c, sin_c)


def _nsa_cmp_band(q_ref, kc_ref, vc_ref, ov_ref, gate_ref, ocmp_ref, sel_ref, psum_scr, imp_scr, cw, sw):
    i = pl.program_id(1)
    tq = q_ref.shape[0]
    nsb = sel_ref.shape[2]
    t = i * tq + lax.broadcasted_iota(jnp.int32, (tq, 1), 0)
    cend = lax.broadcasted_iota(jnp.int32, (1, cw), 1) * CMP_STRIDE + (CMP_LEN - 1)
    mask = cend <= t
    kc = kc_ref[0, :cw, :]
    vc = vc_ref[0, :cw, :]
    gates = jax.nn.sigmoid(gate_ref[...])
    for r in range(NSA_REP):
        q = q_ref[:, r * HEAD_DIM:(r + 1) * HEAD_DIM]
        s = jnp.where(mask, _dot_nt(q, kc) * SCALE, NEG_INF)
        m = jnp.max(s, axis=-1, keepdims=True)
        e = jnp.where(mask, jnp.exp(s - m), 0.0)
        l = jnp.sum(e, axis=-1, keepdims=True)
        p = e / jnp.where(l > 0.0, l, 1.0)
        psum_scr[:, :cw] = p if r == 0 else psum_scr[:, :cw] + p
        o = _dot(p.astype(BF16), vc)
        ocmp_ref[:, r * HEAD_DIM:(r + 1) * HEAD_DIM] = gates[:, 3 * r:3 * r + 1] * o

    p_hi, p_lo = _split_bf16(psum_scr[:, :cw])
    ov = ov_ref[:cw, :sw]
    imp_scr[:, :sw] = _dot(p_hi, ov) + _dot(p_lo, ov)

    rt = min(CMP_ROW_TILE, tq)
    jblk = lax.broadcasted_iota(jnp.int32, (1, sw), 1)
    jf = jblk.astype(F32)

    def choose(rr, carry):
        r0 = pl.multiple_of(rr * rt, rt)
        tb = lax.shift_right_logical(i * tq + r0 + lax.broadcasted_iota(jnp.int32, (rt, 1), 0),
                                     SLC_BLOCK.bit_length() - 1)
        valid = jblk <= tb
        forced = valid & ((jblk == 0) | (jblk == tb) | (jblk == tb - 1))
        work = jnp.where(valid & jnp.logical_not(forced), imp_scr[pl.ds(r0, rt), :sw], LOWEST)
        sel = jnp.where(forced, 1.0, 0.0)
        for _ in range(min(SLC_TOPN, nsb) - N_FORCED):
            m = jnp.max(work, axis=-1, keepdims=True)
            first = jnp.min(jnp.where(work == m, jf, 1e9), axis=-1, keepdims=True)
            hit = jf == first
            sel = jnp.where(hit, 1.0, sel)
            work = jnp.where(hit, LOWEST, work)
        sel_ref[0, pl.ds(r0, rt), :sw] = sel.astype(BF16)
        return carry

    lax.fori_loop(0, tq // rt, choose, 0)
    if sw < nsb:
        sel_ref[0, :, sw:] = jnp.zeros((tq, nsb - sw), BF16)


def _nsa_cmp_kernel(q_ref, kc_ref, vc_ref, ov_ref, gate_ref, ocmp_ref, sel_ref, psum_scr, imp_scr):
    i = pl.program_id(1)
    tq = q_ref.shape[0]
    ncp = kc_ref.shape[1]
    nsb = sel_ref.shape[2]
    band_w = min(CMP_BAND, ncp)
    n_band = ncp // band_w
    band = jnp.minimum(((i + 1) * (tq // CMP_STRIDE) + band_w - 1) // band_w, n_band) - 1
    for b in range(n_band):
        cw = band_w * (b + 1)
        sw = min(nsb, -(-(cw * CMP_STRIDE // SLC_BLOCK) // LANE) * LANE)

        @pl.when(band == b)
        def _():
            _nsa_cmp_band(q_ref, kc_ref, vc_ref, ov_ref, gate_ref, ocmp_ref, sel_ref, psum_scr, imp_scr, cw, sw)


def _nsa_cmp(proj, cmp_kv, overlap, gates):
    s = proj.shape[0]
    ncp = cmp_kv.shape[1]
    nsb = overlap.shape[1]
    tq = min(Q_TILE, s)
    gw = NSA_REP * HEAD_DIM
    assert ncp % min(CMP_BAND, ncp) == 0
    return pl.pallas_call(
        _nsa_cmp_kernel,
        grid=(N_KV_NSA, s // tq),
        in_specs=[
            pl.BlockSpec((tq, gw), lambda g, i: (i, g)),
            pl.BlockSpec((1, ncp, HEAD_DIM), lambda g, i: (g, 0, 0)),
            pl.BlockSpec((1, ncp, HEAD_DIM), lambda g, i: (N_KV_NSA + g, 0, 0)),
            pl.BlockSpec((ncp, nsb), lambda g, i: (0, 0)),
            pl.BlockSpec((tq, LANE), lambda g, i: (i, g)),
        ],
        out_specs=[
            pl.BlockSpec((tq, gw), lambda g, i: (i, g)),
            pl.BlockSpec((1, tq, nsb), lambda g, i: (g, i, 0)),
        ],
        out_shape=[jax.ShapeDtypeStruct((s, NSA_WIDTH), F32),
                   jax.ShapeDtypeStruct((N_KV_NSA, s, nsb), BF16)],
        scratch_shapes=[pltpu.VMEM((tq, ncp), F32), pltpu.VMEM((tq, nsb), F32)],
        compiler_params=_cparams(("arbitrary", "arbitrary")),
        name="nsa_cmp_select",
    )(proj, cmp_kv, cmp_kv, overlap, gates)


def _lane_chunks(s):
    return [s[:, c * LANE:(c + 1) * LANE] for c in range(s.shape[1] // LANE)]


def _row_max(chunks):
    m = functools.reduce(jnp.maximum, chunks)
    return jnp.broadcast_to(jnp.max(m, axis=-1, keepdims=True), m.shape)


def _softmax_weights(chunks, m):
    return jnp.concatenate([jnp.exp2((c - m) * EXP2_SCALE) for c in chunks], axis=1).astype(BF16)


def _flash_init(s, v_aug, m_scr, acc_scr, idx):
    chunks = _lane_chunks(s)
    m = _row_max(chunks)
    acc_scr[idx] = _dot(_softmax_weights(chunks, m), v_aug)
    m_scr[idx] = m


def _flash_step(s, v_aug, m_scr, acc_scr, idx):
    chunks = _lane_chunks(s)
    m_old = m_scr[idx]
    m_new = jnp.maximum(m_old, _row_max(chunks))
    alpha = jnp.exp2((m_old - m_new) * EXP2_SCALE)
    pv = _dot(_softmax_weights(chunks, m_new), v_aug)
    acc = acc_scr[idx]
    acc_scr[idx] = jnp.concatenate([alpha * a for a in _lane_chunks(acc)], axis=1) + pv
    m_scr[idx] = m_new


def _with_ones(v):
    return jnp.concatenate([v, jnp.ones(v.shape, v.dtype)], axis=1)


def _block_onehot(k0, tk, block, width):
    pos = k0 + lax.broadcasted_iota(jnp.int32, (tk, 1), 0)
    blk = lax.shift_right_logical(pos, block.bit_length() - 1) & (width - 1)
    return (blk == lax.broadcasted_iota(jnp.int32, (1, width), 1)).astype(BF16)


def _nsa_slc_kernel(q_ref, ks_ref, vs_ref, kw0_ref, kw1_ref, kw2_ref, vw0_ref, vw1_ref, vw2_ref,
                    sel_ref, gate_ref, ocmp_ref, o_ref, ka_scr, va_scr, qa_scr, m_scr, acc_scr, sa_scr, sb_scr):
    i = pl.program_id(1)
    tq = q_ref.shape[0]
    nsb = sel_ref.shape[2]
    s_len = ks_ref.shape[0]
    tk = min(SLC_KEY_TILE, s_len)
    n_kt = s_len // tk
    width = min(nsb, LANE)
    nhalf = max(nsb // LANE, 1)
    kt_per_half = (LANE * SLC_BLOCK) // tk
    t = i * tq + lax.broadcasted_iota(jnp.int32, (tq, 1), 0)
    kt_diag = (i * tq) // tk

    @pl.when(i == 0)
    def _():
        def build(n, carry):
            k0 = pl.multiple_of(n * tk, tk)
            ka_scr[pl.ds(k0, tk), :] = jnp.concatenate(
                [ks_ref[pl.ds(k0, tk), :], _block_onehot(k0, tk, SLC_BLOCK, width)], axis=1)
            va_scr[pl.ds(k0, tk), :] = _with_ones(vs_ref[pl.ds(k0, tk), :])
            return carry

        lax.fori_loop(0, n_kt, build, 0)
        ka_scr[pl.ds(s_len, tk), :] = jnp.concatenate(
            [jnp.zeros((tk, HEAD_DIM), BF16), jnp.ones((tk, width), BF16)], axis=1)
        va_scr[pl.ds(s_len, tk), :] = jnp.zeros((tk, 2 * HEAD_DIM), BF16)

    bias = ((1.0 - sel_ref[0].astype(F32)) * NEG_INF).astype(BF16)
    for half in range(nhalf):
        for r in range(NSA_REP):
            qa_scr[half, r * tq:(r + 1) * tq, :] = jnp.concatenate(
                [q_ref[:, r * HEAD_DIM:(r + 1) * HEAD_DIM], bias[:, half * width:(half + 1) * width]], axis=1)
    t_all = jnp.concatenate([t] * NSA_REP, axis=0)

    def scores(kt):
        half = jnp.where(kt < n_kt, kt // kt_per_half, 0)
        return _dot_nt(qa_scr[half], ka_scr[pl.ds(pl.multiple_of(kt * tk, tk), tk), :])

    def values(kt):
        return va_scr[pl.ds(pl.multiple_of(kt * tk, tk), tk), :]

    kpos = kt_diag * tk + lax.broadcasted_iota(jnp.int32, (1, tk), 1)
    _flash_init(jnp.where(kpos <= t_all, scores(kt_diag), NEG_INF), values(kt_diag), m_scr, acc_scr, Ellipsis)

    last = jnp.maximum(kt_diag - 1, 0)
    sa_scr[...] = scores(0)

    def tile_pair(j, carry):
        kt_b = jnp.where(2 * j + 1 < kt_diag, 2 * j + 1, n_kt)
        sb_scr[...] = scores(kt_b)
        _flash_step(sa_scr[...], values(2 * j), m_scr, acc_scr, Ellipsis)
        sa_scr[...] = scores(jnp.minimum(2 * j + 2, last))
        _flash_step(sb_scr[...], values(kt_b), m_scr, acc_scr, Ellipsis)
        return carry

    lax.fori_loop(0, (kt_diag + 1) // 2, tile_pair, 0)

    kwin = jnp.concatenate([kw0_ref[...], kw1_ref[...], kw2_ref[...]], axis=0)
    vwin = _with_ones(jnp.concatenate([vw0_ref[...], vw1_ref[...], vw2_ref[...]], axis=0))
    nw = kwin.shape[0]
    wpos = (i - 2) * tq + lax.broadcasted_iota(jnp.int32, (1, nw), 1)
    wvalid = (wpos <= t_all) & (wpos > t_all - WINDOW) & (wpos >= 0)
    s = jnp.where(wvalid, _dot_nt(qa_scr[0, :, :HEAD_DIM], kwin), NEG_INF)
    m = jnp.max(s, axis=-1, keepdims=True)
    ow = _dot(jnp.exp2((s - m) * EXP2_SCALE).astype(BF16), vwin)
    o_win = ow[:, :HEAD_DIM] / ow[:, HEAD_DIM:]
    acc = acc_scr[...]
    o_slc = acc[:, :HEAD_DIM] / acc[:, HEAD_DIM:]
    gates = jax.nn.sigmoid(gate_ref[...])
    for r in range(NSA_REP):
        rows = slice(r * tq, (r + 1) * tq)
        o = (ocmp_ref[:, r * HEAD_DIM:(r + 1) * HEAD_DIM]
             + gates[:, 3 * r + 1:3 * r + 2] * o_slc[rows] + gates[:, 3 * r + 2:3 * r + 3] * o_win[rows])
        o_ref[:, r * HEAD_DIM:(r + 1) * HEAD_DIM] = o.astype(BF16)


def _nsa_slc(proj, sel, gates, ocmp):
    s = proj.shape[0]
    nsb = sel.shape[2]
    tq = min(Q_TILE, s)
    tk = min(SLC_KEY_TILE, s)
    gw = NSA_REP * HEAD_DIM
    assert WINDOW == 2 * tq and (nsb <= LANE or nsb % LANE == 0)

    def win_spec(col, d):
        return pl.BlockSpec((tq, HEAD_DIM), lambda g, i: (jnp.maximum(i - 2 + d, 0), col + g))

    return pl.pallas_call(
        _nsa_slc_kernel,
        grid=(N_KV_NSA, s // tq),
        in_specs=[
            pl.BlockSpec((tq, gw), lambda g, i: (i, g)),
            pl.BlockSpec((s, HEAD_DIM), lambda g, i: (0, COL_KS + g)),
            pl.BlockSpec((s, HEAD_DIM), lambda g, i: (0, COL_VS + g)),
            win_spec(COL_KW, 0), win_spec(COL_KW, 1), win_spec(COL_KW, 2),
            win_spec(COL_VW, 0), win_spec(COL_VW, 1), win_spec(COL_VW, 2),
            pl.BlockSpec((1, tq, nsb), lambda g, i: (g, i, 0)),
            pl.BlockSpec((tq, LANE), lambda g, i: (i, g)),
            pl.BlockSpec((tq, gw), lambda g, i: (i, g)),
        ],
        out_specs=pl.BlockSpec((tq, gw), lambda g, i: (i, g)),
        out_shape=jax.ShapeDtypeStruct((s, NSA_WIDTH), BF16),
        scratch_shapes=[pltpu.VMEM((s + tk, HEAD_DIM + min(nsb, LANE)), BF16),
                        pltpu.VMEM((s + tk, 2 * HEAD_DIM), BF16),
                        pltpu.VMEM((max(nsb // LANE, 1), NSA_REP * tq, HEAD_DIM + min(nsb, LANE)), BF16),
                        pltpu.VMEM((NSA_REP * tq, LANE), F32),
                        pltpu.VMEM((NSA_REP * tq, 2 * HEAD_DIM), F32),
                        pltpu.VMEM((NSA_REP * tq, tk), F32), pltpu.VMEM((NSA_REP * tq, tk), F32)],
        compiler_params=_cparams(("arbitrary", "arbitrary")),
        name="nsa_slc_win",
    )(proj, proj, proj, proj, proj, proj, proj, proj, proj, sel, gates, ocmp)


def _moba_kernel(q_ref, k_ref, v_ref, o_ref, ka_scr, va_scr, qa_scr, km_scr, m_scr, acc_scr, sa_scr, sb_scr):
    i = pl.program_id(1)
    tq = q_ref.shape[0]
    s_len = k_ref.shape[0]
    nbp = km_scr.shape[0]
    bpt = tq // MOBA_BLOCK
    n_kt = s_len // tq

    @pl.when(i == 0)
    def _():
        km_scr[...] = jnp.zeros(km_scr.shape, F32)

        def build(n, carry):
            k0 = pl.multiple_of(n * tq, tq)
            kt = k_ref[pl.ds(k0, tq), :]
            ka_scr[pl.ds(k0, tq), :] = jnp.concatenate([kt, _block_onehot(k0, tq, MOBA_BLOCK, nbp)], axis=1)
            va_scr[pl.ds(k0, tq), :] = _with_ones(v_ref[pl.ds(k0, tq), :])
            for j in range(bpt):
                kb = kt[j * MOBA_BLOCK:(j + 1) * MOBA_BLOCK].astype(F32)
                km_scr[pl.ds(n * bpt + j, 1), :] = jnp.mean(kb, axis=0, keepdims=True)
            return carry

        lax.fori_loop(0, n_kt, build, 0)
        ka_scr[pl.ds(s_len, tq), :] = jnp.concatenate(
            [jnp.zeros((tq, HEAD_DIM), BF16), jnp.ones((tq, nbp), BF16)], axis=1)
        va_scr[pl.ds(s_len, tq), :] = jnp.zeros((tq, 2 * HEAD_DIM), BF16)

    q = q_ref[...]
    km_hi, km_lo = _split_bf16(km_scr[...])
    gate = _dot_nt(q, km_hi) + _dot_nt(q, km_lo)
    t = i * tq + lax.broadcasted_iota(jnp.int32, (tq, 1), 0)
    own = lax.shift_right_logical(t, MOBA_BLOCK.bit_length() - 1)
    nidx = lax.broadcasted_iota(jnp.int32, (1, nbp), 1)
    nf = nidx.astype(F32)
    work = jnp.where(nidx < own, gate, LOWEST)
    sel = nidx == own
    for _ in range(MOBA_TOPK):
        m = jnp.max(work, axis=-1, keepdims=True)
        first = jnp.min(jnp.where((work == m) & (m > 0.5 * LOWEST), nf, 1e9), axis=-1, keepdims=True)
        hit = nf == first
        sel = sel | hit
        work = jnp.where(hit, LOWEST, work)
    qa_scr[...] = jnp.concatenate([q, jnp.where(sel, 0.0, NEG_INF).astype(BF16)], axis=1)

    def scores(kt):
        return _dot_nt(qa_scr[...], ka_scr[pl.ds(pl.multiple_of(kt * tq, tq), tq), :])

    def values(kt):
        return va_scr[pl.ds(pl.multiple_of(kt * tq, tq), tq), :]

    kpos = i * tq + lax.broadcasted_iota(jnp.int32, (1, tq), 1)
    _flash_init(jnp.where(kpos <= t, scores(i), NEG_INF), values(i), m_scr, acc_scr, Ellipsis)

    last = jnp.maximum(i - 1, 0)
    sa_scr[...] = scores(0)

    def tile_pair(j, carry):
        kt_b = jnp.where(2 * j + 1 < i, 2 * j + 1, n_kt)
        sb_scr[...] = scores(kt_b)
        _flash_step(sa_scr[...], values(2 * j), m_scr, acc_scr, Ellipsis)
        sa_scr[...] = scores(jnp.minimum(2 * j + 2, last))
        _flash_step(sb_scr[...], values(kt_b), m_scr, acc_scr, Ellipsis)
        return carry

    lax.fori_loop(0, (i + 1) // 2, tile_pair, 0)
    acc = acc_scr[...]
    o_ref[...] = (acc[:, :HEAD_DIM] / acc[:, HEAD_DIM:]).astype(BF16)


def _moba(proj):
    s = proj.shape[0]
    tq = min(MOBA_Q_TILE, s)
    nbp = max(LANE, s // MOBA_BLOCK)
    assert nbp & (nbp - 1) == 0
    return pl.pallas_call(
        _moba_kernel,
        grid=(N_HEADS_MOBA, s // tq),
        in_specs=[
            pl.BlockSpec((tq, HEAD_DIM), lambda h, i: (i, COL_QM + h)),
            pl.BlockSpec((s, HEAD_DIM), lambda h, i: (0, COL_KM + h)),
            pl.BlockSpec((s, HEAD_DIM), lambda h, i: (0, COL_VM + h)),
        ],
        out_specs=pl.BlockSpec((tq, HEAD_DIM), lambda h, i: (i, h)),
        out_shape=jax.ShapeDtypeStruct((s, MOBA_WIDTH), BF16),
        scratch_shapes=[pltpu.VMEM((s + tq, HEAD_DIM + nbp), BF16), pltpu.VMEM((s + tq, 2 * HEAD_DIM), BF16),
                        pltpu.VMEM((tq, HEAD_DIM + nbp), BF16),
                        pltpu.VMEM((nbp, HEAD_DIM), F32), pltpu.VMEM((tq, LANE), F32),
                        pltpu.VMEM((tq, 2 * HEAD_DIM), F32),
                        pltpu.VMEM((tq, tq), F32), pltpu.VMEM((tq, tq), F32)],
        compiler_params=_cparams(("arbitrary", "arbitrary")),
        name="moba",
    )(proj, proj, proj)


def _out_kernel(on_ref, om_ref, x_ref, wn_ref, wm_ref, gt_ref, g_ref, sc_ref, sh_ref, wrh_ref, wrl_ref,
                x1_ref, h_ref, lg_ref):
    mix = _dot(on_ref[...], wn_ref[...]) + _dot(om_ref[...], wm_ref[...])
    x1 = x_ref[...] + gt_ref[...] * mix
    x1_ref[...] = x1
    h = _rms_mod(x1, g_ref[...], sc_ref[...], sh_ref[...])
    h_hi, h_lo = _split_bf16(h)
    h_ref[...] = h
    wrh = wrh_ref[...]
    lg_ref[...] = _dot(h_hi, wrh) + _dot(h_lo, wrh) + _dot(h_hi, wrl_ref[...])


def _out_proj(o_nsa, o_moba, x, w_n, w_m, gt, g, sc, sh, wr_hi, wr_lo):
    s, d = x.shape
    tm = min(OUT_TM, s)
    row = lambda i: (i, 0)
    fix = lambda i: (0, 0)
    return pl.pallas_call(
        _out_kernel,
        grid=(s // tm,),
        in_specs=[
            pl.BlockSpec((tm, NSA_WIDTH), row),
            pl.BlockSpec((tm, MOBA_WIDTH), row),
            pl.BlockSpec((tm, d), row),
            pl.BlockSpec((NSA_WIDTH, d), fix),
            pl.BlockSpec((MOBA_WIDTH, d), fix),
            pl.BlockSpec((1, d), fix),
            pl.BlockSpec((1, d), fix),
            pl.BlockSpec((1, d), fix),
            pl.BlockSpec((1, d), fix),
            pl.BlockSpec((d, LANE), fix),
            pl.BlockSpec((d, LANE), fix),
        ],
        out_specs=[pl.BlockSpec((tm, d), row), pl.BlockSpec((tm, d), row), pl.BlockSpec((tm, LANE), row)],
        out_shape=[jax.ShapeDtypeStruct((s, d), F32), jax.ShapeDtypeStruct((s, d), F32),
                   jax.ShapeDtypeStruct((s, LANE), F32)],
        compiler_params=_cparams(("arbitrary",)),
        name="out_proj_norm_router",
    )(o_nsa, o_moba, x, w_n, w_m, gt, g, sc, sh, wr_hi, wr_lo)


def _row_copy(src_ref, dst_ref, src_row, dst_row, sem):
    return pltpu.make_async_copy(src_ref.at[pl.ds(src_row, 1)], dst_ref.at[pl.ds(dst_row, 1)], sem)


def _expert_kernel(be_ref, nu_ref, tok_ref, h_ref, wg_ref, wu_ref, wd_ref, y_ref,
                   x_buf, wg_s, wu_s, wd_s, sems):
    b = pl.program_id(0)
    n_used = nu_ref[0]

    def gather(blk, buf, wait):
        def row(r, carry):
            cp = _row_copy(h_ref, x_buf.at[buf], tok_ref[blk * MOE_ROWS + r], r, sems.at[buf])
            if wait:
                cp.wait()
            else:
                cp.start()
            return carry

        lax.fori_loop(0, MOE_ROWS, row, 0, unroll=DMA_UNROLL)

    @pl.when((b == 0) & (n_used > 0))
    def _():
        gather(0, 0, False)

    prev = be_ref[jnp.maximum(b - 1, 0)]

    @pl.when((b == 0) | (be_ref[b] != prev))
    def _():
        wg_s[...] = wg_ref[0, 0].astype(BF16)
        wu_s[...] = wu_ref[0, 0].astype(BF16)
        wd_s[...] = wd_ref[0, 0].astype(BF16)

    @pl.when(b < n_used)
    def _():
        gather(b, b % 2, True)
        nxt = jnp.minimum(b + 1, n_used - 1)
        for r in range(MOE_ROWS):
            _row_copy(h_ref, x_buf.at[(b + 1) % 2], tok_ref[nxt * MOE_ROWS + r], r, sems.at[(b + 1) % 2]).start()
        x = x_buf[b % 2].astype(BF16)
        a = _dot(x, wg_s[...])
        u = _dot(x, wu_s[...])
        hid = (a * jax.nn.sigmoid(a) * u).astype(BF16)
        y_ref[...] = _dot(hid, wd_s[...])

    @pl.when(b + 1 == n_used)
    def _():
        gather(b, (b + 1) % 2, True)

    @pl.when(b >= n_used)
    def _():
        y_ref[...] = jnp.zeros(y_ref.shape, F32)


def _experts(h, slot_tok, blk_exp, n_used, w_gate, w_up, w_down, layer):
    n_slots = slot_tok.shape[0]
    d = h.shape[1]
    n_blk = n_slots // MOE_ROWS
    grid_spec = pltpu.PrefetchScalarGridSpec(
        num_scalar_prefetch=3,
        grid=(n_blk,),
        in_specs=[
            pl.BlockSpec(memory_space=pl.ANY),
            pl.BlockSpec((1, 1, d, D_EXPERT), lambda b, be, nu, tok: (layer, be[b], 0, 0)),
            pl.BlockSpec((1, 1, d, D_EXPERT), lambda b, be, nu, tok: (layer, be[b], 0, 0)),
            pl.BlockSpec((1, 1, D_EXPERT, d), lambda b, be, nu, tok: (layer, be[b], 0, 0)),
        ],
        out_specs=pl.BlockSpec((MOE_ROWS, d), lambda b, be, nu, tok: (b, 0)),
        scratch_shapes=[pltpu.VMEM((2, MOE_ROWS, d), F32),
                        pltpu.VMEM((d, D_EXPERT), BF16), pltpu.VMEM((d, D_EXPERT), BF16),
                        pltpu.VMEM((D_EXPERT, d), BF16), pltpu.SemaphoreType.DMA((2,))],
    )
    return pl.pallas_call(
        _expert_kernel,
        grid_spec=grid_spec,
        out_shape=jax.ShapeDtypeStruct((n_slots, d), F32),
        compiler_params=_cparams(("arbitrary",)),
        name="moe_experts",
    )(blk_exp, n_used, slot_tok, h, w_gate, w_up, w_down)


def _combine_kernel(sa_ref, sb_ref, na_ref, nb_ref, x_ref, w_ref, gt_ref, gf_ref, y_ref, o_ref, buf, sems, *,
                    final):
    b = pl.program_id(0)
    tt = x_ref.shape[0]

    def gather(ia_ref, ib_ref, slot, wait):
        def row(r, carry):
            for k, idx_ref in enumerate((ia_ref, ib_ref)):
                cp = _row_copy(y_ref, buf.at[slot].at[k], idx_ref[0, 0, r], r, sems.at[slot].at[k])
                if wait:
                    cp.wait()
                else:
                    cp.start()
            return carry

        lax.fori_loop(0, tt, row, 0, unroll=DMA_UNROLL)

    @pl.when(b == 0)
    def _():
        gather(sa_ref, sb_ref, 0, False)

    @pl.when(b + 1 < pl.num_programs(0))
    def _():
        gather(na_ref, nb_ref, (b + 1) % 2, False)

    gather(sa_ref, sb_ref, b % 2, True)
    w = w_ref[...]
    x2 = x_ref[...] + gt_ref[...] * (w[:, 0:1] * buf[b % 2, 0] + w[:, 1:2] * buf[b % 2, 1])
    if final:
        x2 = x2 * lax.rsqrt(jnp.mean(x2 * x2, axis=-1, keepdims=True) + RMS_EPS) * gf_ref[...]
    o_ref[...] = x2


def _combine(x1, ys, slot_a, slot_b, pair_w, gt, g_final, final):
    s, d = x1.shape
    tt = min(GATHER_ROWS, s)
    nt = s // tt
    cur_spec = pl.BlockSpec((1, 1, tt), lambda b: (b, 0, 0), memory_space=pltpu.SMEM)
    nxt_spec = pl.BlockSpec((1, 1, tt), lambda b: (jnp.minimum(b + 1, nt - 1), 0, 0), memory_space=pltpu.SMEM)
    slot_a = slot_a.reshape(nt, 1, tt)
    slot_b = slot_b.reshape(nt, 1, tt)
    return pl.pallas_call(
        functools.partial(_combine_kernel, final=final),
        grid=(nt,),
        in_specs=[
            cur_spec, cur_spec, nxt_spec, nxt_spec,
            pl.BlockSpec((tt, d), lambda b: (b, 0)),
            pl.BlockSpec((tt, LANE), lambda b: (b, 0)),
            pl.BlockSpec((1, d), lambda b: (0, 0)),
            pl.BlockSpec((1, d), lambda b: (0, 0)),
            pl.BlockSpec(memory_space=pl.ANY),
        ],
        out_specs=pl.BlockSpec((tt, d), lambda b: (b, 0)),
        out_shape=jax.ShapeDtypeStruct((s, d), F32),
        scratch_shapes=[pltpu.VMEM((2, 2, tt, d), F32), pltpu.SemaphoreType.DMA((2, 2))],
        compiler_params=_cparams(("arbitrary",)),
        name="moe_combine",
    )(slot_a, slot_b, slot_a, slot_b, x1, jnp.pad(pair_w, ((0, 0), (0, LANE - TOP_K_EXPERTS))), gt, g_final, ys)


def _route(logits, b_router):
    n = logits.shape[0]
    scores = jax.nn.sigmoid(logits[:, :N_EXPERTS])
    biased = (scores + b_router.astype(F32)).reshape(n, N_GROUPS, EXPERTS_PER_GROUP)

    def top2(v):
        idx = lax.broadcasted_iota(jnp.int32, v.shape, v.ndim - 1)
        v1 = jnp.max(v, axis=-1, keepdims=True)
        i1 = jnp.min(jnp.where(v == v1, idx, v.shape[-1]), axis=-1, keepdims=True)
        rest = jnp.where(idx == i1, -jnp.inf, v)
        v2 = jnp.max(rest, axis=-1, keepdims=True)
        i2 = jnp.min(jnp.where(rest == v2, idx, v.shape[-1]), axis=-1, keepdims=True)
        return v1, v2, i1, i2

    g1, g2, _, _ = top2(biased)
    grp = jnp.argmax((g1 + g2)[..., 0], axis=-1)
    grp_hot = grp[:, None] == jnp.arange(N_GROUPS, dtype=grp.dtype)[None, :]
    in_grp = jnp.sum(jnp.where(grp_hot[:, :, None], biased, 0.0), axis=1)
    _, _, l1, l2 = top2(in_grp)
    expert = (grp[:, None] * EXPERTS_PER_GROUP + jnp.concatenate([l1, l2], axis=1)).astype(jnp.int32)
    e_hot = expert[:, :, None] == jnp.arange(N_EXPERTS, dtype=jnp.int32)[None, None, :]
    wsel = jnp.sum(jnp.where(e_hot, scores[:, None, :], 0.0), axis=-1)
    wsel = wsel / jnp.sum(wsel, axis=-1, keepdims=True)

    n_pair = n * TOP_K_EXPERTS
    flat_e = expert.reshape(-1)
    flat_t = jnp.repeat(jnp.arange(n, dtype=jnp.int32), TOP_K_EXPERTS)
    hot = flat_e[:, None] == jnp.arange(N_EXPERTS, dtype=jnp.int32)[None, :]
    onehot = hot.astype(jnp.int32)
    chunk = min(LANE, n_pair)
    n_chunk = n_pair // chunk
    tri = (jnp.arange(chunk)[:, None] >= jnp.arange(chunk)[None, :]).astype(BF16)
    within = jnp.einsum('ij,cjk->cik', tri, hot.astype(BF16).reshape(n_chunk, chunk, N_EXPERTS),
                        preferred_element_type=F32)
    totals = within[:, -1, :]
    before = jnp.cumsum(totals, axis=0) - totals
    running = (within + before[:, None, :]).astype(jnp.int32).reshape(n_pair, N_EXPERTS)
    rank = jnp.sum(running * onehot, axis=1) - 1
    counts = running[-1]
    padded = (counts + MOE_ROWS - 1) // MOE_ROWS * MOE_ROWS
    pend = jnp.cumsum(padded)
    pstart = pend - padded
    slot = (jnp.sum(pstart[None, :] * onehot, axis=1) + rank).astype(jnp.int32)
    n_slots = n_pair + N_EXPERTS * MOE_ROWS
    slot_tok = jnp.zeros((n_slots,), jnp.int32).at[slot].set(flat_t)
    pair_slot = slot.reshape(n, TOP_K_EXPERTS)
    n_blk = n_slots // MOE_ROWS
    blk_exp = jnp.minimum(jnp.searchsorted(pend, jnp.arange(n_blk, dtype=jnp.int32) * MOE_ROWS, side='right'),
                          N_EXPERTS - 1).astype(jnp.int32)
    n_used = (pend[-1:] // MOE_ROWS).astype(jnp.int32)
    return slot_tok, wsel, pair_slot, blk_exp, n_used


def _rope_tables(pos):
    inv = 1.0 / (ROPE_THETA ** (jnp.arange(0, HEAD_DIM, 2, dtype=F32) / HEAD_DIM))
    ang = pos.astype(F32)[:, None] * inv[None, :]
    cos = jnp.cos(ang)
    sin = jnp.sin(ang)
    return jnp.concatenate([cos, cos], axis=-1), jnp.concatenate([-sin, sin], axis=-1)


def _layout_w_in(w_in):
    o_g = NSA_WIDTH + 6 * NSA_KV_WIDTH
    main = jnp.concatenate([w_in[:, :o_g], w_in[:, o_g + N_GATES:]], axis=1).astype(BF16)
    wg = w_in[:, o_g:o_g + N_GATES].reshape(-1, N_KV_NSA, 3 * NSA_REP)
    wg = jnp.pad(wg, ((0, 0), (0, 0), (0, LANE - 3 * NSA_REP))).reshape(-1, N_KV_NSA * LANE)
    return main, wg.astype(BF16)


def _overlap(ncp, nsb):
    cs = jnp.arange(ncp, dtype=jnp.int32)[:, None] * CMP_STRIDE
    ss = jnp.arange(nsb, dtype=jnp.int32)[None, :] * SLC_BLOCK
    return ((cs + CMP_LEN - 1 >= ss) & (cs < ss + SLC_BLOCK)).astype(BF16)


def kernel(x, c, w_ada, b_ada, g_attn, g_ffn, w_in, w_out, cmp_pe, cmp_w1, cmp_b1, cmp_w2,
           w_router, b_router, w_gate, w_up, w_down, g_final):
    b, s, d = x.shape
    assert b == 1 and d == D_MODEL and s % max(Q_TILE, SLC_KEY_TILE, MOBA_Q_TILE) == 0
    depth = w_ada.shape[0]
    ncp = s // CMP_STRIDE
    nsb = s // SLC_BLOCK

    cos, sin = _rope_tables(jnp.arange(s))
    cos_c, sin_c = _rope_tables(jnp.arange(ncp) * CMP_STRIDE + CMP_LEN - 1)
    overlap = _overlap(ncp, nsb)
    flags = jnp.asarray(ROPE_FLAGS, jnp.int32)
    wr = jnp.pad(w_router.astype(F32), ((0, 0), (0, LANE - N_EXPERTS)))
    wr_hi, wr_lo = _split_bf16(wr)

    mod = _ada_mod(c.astype(F32), w_ada, b_ada)
    xs = x.reshape(s, d)
    for l in range(depth):
        sh1, sc1, gt1, sh2, sc2, gt2 = [mod[l, :, k * d:(k + 1) * d] for k in range(6)]
        w_main, w_gates = _layout_w_in(w_in[l])
        proj, gates = _project(xs, g_attn[l].reshape(1, d), sc1, sh1, w_main, w_gates, cos, sin, flags)

        chunks = proj[:, COL_KC * LANE:COL_KS * LANE].reshape(s, 2 * N_KV_NSA, HEAD_DIM)
        chunks = chunks.transpose(1, 0, 2).reshape(2 * N_KV_NSA, ncp, CMP_STRIDE * HEAD_DIM)
        pe8 = jnp.pad(cmp_pe[l].reshape(2, 1, CMP_LEN * HEAD_DIM), ((0, 0), (0, 7), (0, 0)))
        cmp_kv = _compress(chunks, cmp_w1[l], cmp_b1[l].reshape(2, 1, CMP_HIDDEN), cmp_w2[l], pe8, cos_c, sin_c)

        ocmp, sel = _nsa_cmp(proj, cmp_kv, overlap, gates)
        o_nsa = _nsa_slc(proj, sel, gates, ocmp)
        o_moba = _moba(proj)

        w_o = w_out[l].astype(BF16)
        x1, h2, logits = _out_proj(o_nsa, o_moba, xs, w_o[:NSA_WIDTH], w_o[NSA_WIDTH:], gt1,
                                   g_ffn[l].reshape(1, d), sc2, sh2, wr_hi, wr_lo)

        slot_tok, pair_w, pair_slot, blk_exp, n_used = _route(logits, b_router)
        ys = _experts(h2, slot_tok, blk_exp, n_used, w_gate, w_up, w_down, l)
        xs = _combine(x1, ys, pair_slot[:, 0], pair_slot[:, 1], pair_w, gt2, g_final.reshape(1, d),
                      l == depth - 1)
    return xs.reshape(b, s, d)
```

```python
import functools

import jax
import jax.numpy as jnp
from jax import lax
from jax.experimental import pallas as pl
from jax.experimental.pallas import tpu as pltpu

F32 = jnp.float32
BF16 = jnp.bfloat16

D_MODEL = 2048
HEAD_DIM = 128
N_HEADS_NSA = 8
N_KV_NSA = 2
NSA_REP = N_HEADS_NSA // N_KV_NSA
N_HEADS_MOBA = 8
NSA_WIDTH = N_HEADS_NSA * HEAD_DIM
NSA_KV_WIDTH = N_KV_NSA * HEAD_DIM
MOBA_WIDTH = N_HEADS_MOBA * HEAD_DIM
N_GATES = 3 * N_HEADS_NSA
ROPE_THETA = 10000.0
CMP_LEN = 32
CMP_STRIDE = 16
CMP_HIDDEN = 256
SLC_BLOCK = 64
SLC_TOPN = 16
N_FORCED = 3
WINDOW = 512
MOBA_BLOCK = 256
MOBA_TOPK = 3
N_EXPERTS = 32
N_GROUPS = 4
EXPERTS_PER_GROUP = N_EXPERTS // N_GROUPS
TOP_K_EXPERTS = 2
D_EXPERT = 512
RMS_EPS = 1e-6
NEG_INF = -1e30
LOWEST = -3e38
SCALE = HEAD_DIM ** -0.5
EXP2_SCALE = SCALE * 1.4426950408889634

LANE = 128
Q_TILE = 256
MOBA_Q_TILE = 1024
MOBA_KEY_TILE = 512
SLC_KEY_TILE = 512
CMP_BAND = 256
CMP_ROW_TILE = 256
MOE_ROWS = 256
GATHER_ROWS = 256
DMA_UNROLL = 8
PROJ_TM = 1024
PROJ_TN = 512
OUT_TM = 256
VMEM_LIMIT = 56 * 1024 * 1024

COL_QN = 0
COL_KC = 8
COL_VC = 10
COL_KS = 12
COL_VS = 14
COL_KW = 16
COL_VW = 18
COL_QM = 20
COL_KM = 28
COL_VM = 36
PROJ_COLS = 44
ROPE_FLAGS = tuple(
    1 if (c < 8 or c in (12, 13, 16, 17) or 20 <= c < 36) else 0 for c in range(PROJ_COLS))


def _cparams(sem):
    return pltpu.CompilerParams(dimension_semantics=sem, vmem_limit_bytes=VMEM_LIMIT)


def _dot(a, b):
    return jnp.dot(a, b, preferred_element_type=F32)


def _dot_nt(a, b):
    return lax.dot_general(a, b, (((1,), (1,)), ((), ())), preferred_element_type=F32)


def _split_bf16(v):
    hi = v.astype(BF16)
    lo = (v - hi.astype(F32)).astype(BF16)
    return hi, lo


def _mod_kernel(c_ref, w_ref, b_ref, o_ref):
    c = c_ref[...]
    cond = c * jax.nn.sigmoid(c)
    o_ref[0] = jnp.sum(w_ref[0] * cond, axis=0, keepdims=True) + b_ref[0]


def _ada_mod(c, w_ada, b_ada):
    depth, d, n = w_ada.shape
    tn = 1024
    return pl.pallas_call(
        _mod_kernel,
        grid=(depth, n // tn),
        in_specs=[
            pl.BlockSpec((d, 1), lambda l, j: (0, 0)),
            pl.BlockSpec((1, d, tn), lambda l, j: (l, 0, j)),
            pl.BlockSpec((1, 1, tn), lambda l, j: (l, 0, j)),
        ],
        out_specs=pl.BlockSpec((1, 1, tn), lambda l, j: (l, 0, j)),
        out_shape=jax.ShapeDtypeStruct((depth, 1, n), F32),
        compiler_params=_cparams(("arbitrary", "arbitrary")),
        name="ada_mod",
    )(c.reshape(d, 1), w_ada, b_ada.reshape(depth, 1, n))


def _rms_mod(x, g, sc, sh):
    y = x * lax.rsqrt(jnp.mean(x * x, axis=-1, keepdims=True) + RMS_EPS)
    return (y * g) * (1.0 + sc) + sh


def _proj_kernel(flags_ref, x_ref, g_ref, sc_ref, sh_ref, w_ref, wg_ref, cos_ref, sin_ref,
                 o_ref, gate_ref, h_scr):
    j = pl.program_id(1)

    @pl.when(j == 0)
    def _():
        hb = _rms_mod(x_ref[...], g_ref[...], sc_ref[...], sh_ref[...]).astype(BF16)
        h_scr[...] = hb
        gate_ref[...] = _dot(hb, wg_ref[...])

    acc = _dot(h_scr[...], w_ref[...])
    cos = cos_ref[...]
    sin = sin_ref[...]
    nch = PROJ_TN // LANE
    for c in range(nch):
        a = acc[:, c * LANE:(c + 1) * LANE]
        roped = a * cos + pltpu.roll(a, HEAD_DIM // 2, 1) * sin
        f = flags_ref[j * nch + c]
        o_ref[:, c * LANE:(c + 1) * LANE] = jnp.where(f > 0, roped, a).astype(BF16)


def _project(x, g, sc, sh, w_main, w_gate, cos, sin, flags):
    s, d = x.shape
    n = w_main.shape[1]
    tm = min(PROJ_TM, s)
    grid_spec = pltpu.PrefetchScalarGridSpec(
        num_scalar_prefetch=1,
        grid=(s // tm, n // PROJ_TN),
        in_specs=[
            pl.BlockSpec((tm, d), lambda i, j, f: (i, 0)),
            pl.BlockSpec((1, d), lambda i, j, f: (0, 0)),
            pl.BlockSpec((1, d), lambda i, j, f: (0, 0)),
            pl.BlockSpec((1, d), lambda i, j, f: (0, 0)),
            pl.BlockSpec((d, PROJ_TN), lambda i, j, f: (0, j)),
            pl.BlockSpec((d, 2 * LANE), lambda i, j, f: (0, 0)),
            pl.BlockSpec((tm, LANE), lambda i, j, f: (i, 0)),
            pl.BlockSpec((tm, LANE), lambda i, j, f: (i, 0)),
        ],
        out_specs=[
            pl.BlockSpec((tm, PROJ_TN), lambda i, j, f: (i, j)),
            pl.BlockSpec((tm, 2 * LANE), lambda i, j, f: (i, 0)),
        ],
        scratch_shapes=[pltpu.VMEM((tm, d), BF16)],
    )
    return pl.pallas_call(
        _proj_kernel,
        grid_spec=grid_spec,
        out_shape=[jax.ShapeDtypeStruct((s, n), BF16), jax.ShapeDtypeStruct((s, 2 * LANE), F32)],
        compiler_params=_cparams(("arbitrary", "arbitrary")),
        name="norm_proj_rope",
    )(flags, x, g, sc, sh, w_main, w_gate, cos, sin)


def _compress_kernel(ch_ref, w1_ref, b1_ref, w2_ref, pe_ref, cos_ref, sin_ref, o_ref):
    a = pl.program_id(0)
    ch = ch_ref[0]
    nc = ch.shape[0]
    half = CMP_STRIDE * HEAD_DIM
    w1 = w1_ref[0].astype(BF16)
    top = _dot(ch, w1[:half])
    bot = _dot(ch, w1[half:])
    bot = pltpu.roll(bot, nc - 1, 0)
    pe_term = _dot(pe_ref[0].astype(BF16), w1)[0:1]
    hid = jax.nn.gelu(top + bot + pe_term + b1_ref[0])
    out = _dot(hid.astype(BF16), w2_ref[0].astype(BF16))
    roped = out * cos_ref[...] + pltpu.roll(out, HEAD_DIM // 2, 1) * sin_ref[...]
    out = jnp.where(a < N_KV_NSA, roped, out)
    row = lax.broadcasted_iota(jnp.int32, out.shape, 0)
    o_ref[0] = jnp.where(row < nc - 1, out, 0.0).astype(BF16)


def _compress(chunks, w1, b1, w2, pe8, cos_c, sin_c):
    na, nc, kw = chunks.shape
    return pl.pallas_call(
        _compress_kernel,
        grid=(na,),
        in_specs=[
            pl.BlockSpec((1, nc, kw), lambda a: (a, 0, 0)),
            pl.BlockSpec((1, CMP_LEN * HEAD_DIM, CMP_HIDDEN), lambda a: (a // N_KV_NSA, 0, 0)),
            pl.BlockSpec((1, 1, CMP_HIDDEN), lambda a: (a // N_KV_NSA, 0, 0)),
            pl.BlockSpec((1, CMP_HIDDEN, HEAD_DIM), lambda a: (a // N_KV_NSA, 0, 0)),
            pl.BlockSpec((1, 8, CMP_LEN * HEAD_DIM), lambda a: (a // N_KV_NSA, 0, 0)),
            pl.BlockSpec((nc, HEAD_DIM), lambda a: (0, 0)),
            pl.BlockSpec((nc, HEAD_DIM), lambda a: (0, 0)),
        ],
        out_specs=pl.BlockSpec((1, nc, HEAD_DIM), lambda a: (a, 0, 0)),
        out_shape=jax.ShapeDtypeStruct((na, nc, HEAD_DIM), BF16),
        compiler_params=_cparams(("arbitrary",)),
        name="nsa_compress",
    )(chunks, w1, b1, w2, pe8, cos_c, sin_c)


def _nsa_cmp_band(q_ref, kc_ref, vc_ref, ov_ref, gate_ref, ocmp_ref, sel_ref, psum_scr, imp_scr, cw, sw):
    i = pl.program_id(1)
    tq = q_ref.shape[0]
    nsb = sel_ref.shape[2]
    t = i * tq + lax.broadcasted_iota(jnp.int32, (tq, 1), 0)
    cend = lax.broadcasted_iota(jnp.int32, (1, cw), 1) * CMP_STRIDE + (CMP_LEN - 1)
    mask = cend <= t
    kc = kc_ref[0, :cw, :]
    vc = vc_ref[0, :cw, :]
    gates = jax.nn.sigmoid(gate_ref[...])
    for r in range(NSA_REP):
        q = q_ref[:, r * HEAD_DIM:(r + 1) * HEAD_DIM]
        s = jnp.where(mask, _dot_nt(q, kc) * SCALE, NEG_INF)
        m = jnp.max(s, axis=-1, keepdims=True)
        e = jnp.where(mask, jnp.exp(s - m), 0.0)
        l = jnp.sum(e, axis=-1, keepdims=True)
        p = e / jnp.where(l > 0.0, l, 1.0)
        psum_scr[:, :cw] = p if r == 0 else psum_scr[:, :cw] + p
        o = _dot(p.astype(BF16), vc)
        ocmp_ref[:, r * HEAD_DIM:(r + 1) * HEAD_DIM] = gates[:, 3 * r:3 * r + 1] * o

    p_hi, p_lo = _split_bf16(psum_scr[:, :cw])
    ov = ov_ref[:cw, :sw]
    imp_scr[:, :sw] = _dot(p_hi, ov) + _dot(p_lo, ov)

    rt = min(CMP_ROW_TILE, tq)
    jblk = lax.broadcasted_iota(jnp.int32, (1, sw), 1)
    jf = jblk.astype(F32)

    def choose(rr, carry):
        r0 = pl.multiple_of(rr * rt, rt)
        tb = lax.shift_right_logical(i * tq + r0 + lax.broadcasted_iota(jnp.int32, (rt, 1), 0),
                                     SLC_BLOCK.bit_length() - 1)
        valid = jblk <= tb
        forced = valid & ((jblk == 0) | (jblk == tb) | (jblk == tb - 1))
        work = jnp.where(valid & jnp.logical_not(forced), imp_scr[pl.ds(r0, rt), :sw], LOWEST)
        sel = jnp.where(forced, 1.0, 0.0)
        for _ in range(min(SLC_TOPN, nsb) - N_FORCED):
            m = jnp.max(work, axis=-1, keepdims=True)
            first = jnp.min(jnp.where(work == m, jf, 1e9), axis=-1, keepdims=True)
            hit = jf == first
            sel = jnp.where(hit, 1.0, sel)
            work = jnp.where(hit, LOWEST, work)
        sel_ref[0, pl.ds(r0, rt), :sw] = sel.astype(BF16)
        return carry

    lax.fori_loop(0, tq // rt, choose, 0)
    if sw < nsb:
        sel_ref[0, :, sw:] = jnp.zeros((tq, nsb - sw), BF16)


def _nsa_cmp_kernel(q_ref, kc_ref, vc_ref, ov_ref, gate_ref, ocmp_ref, sel_ref, psum_scr, imp_scr):
    i = pl.program_id(1)
    tq = q_ref.shape[0]
    ncp = kc_ref.shape[1]
    nsb = sel_ref.shape[2]
    band_w = min(CMP_BAND, ncp)
    n_band = ncp // band_w
    band = jnp.minimum(((i + 1) * (tq // CMP_STRIDE) + band_w - 1) // band_w, n_band) - 1
    for b in range(n_band):
        cw = band_w * (b + 1)
        sw = min(nsb, -(-(cw * CMP_STRIDE // SLC_BLOCK) // LANE) * LANE)

        @pl.when(band == b)
        def _():
            _nsa_cmp_band(q_ref, kc_ref, vc_ref, ov_ref, gate_ref, ocmp_ref, sel_ref, psum_scr, imp_scr, cw, sw)


def _nsa_cmp(proj, cmp_kv, overlap, gates):
    s = proj.shape[0]
    ncp = cmp_kv.shape[1]
    nsb = overlap.shape[1]
    tq = min(Q_TILE, s)
    gw = NSA_REP * HEAD_DIM
    assert ncp % min(CMP_BAND, ncp) == 0
    return pl.pallas_call(
        _nsa_cmp_kernel,
        grid=(N_KV_NSA, s // tq),
        in_specs=[
            pl.BlockSpec((tq, gw), lambda g, i: (i, g)),
            pl.BlockSpec((1, ncp, HEAD_DIM), lambda g, i: (g, 0, 0)),
            pl.BlockSpec((1, ncp, HEAD_DIM), lambda g, i: (N_KV_NSA + g, 0, 0)),
            pl.BlockSpec((ncp, nsb), lambda g, i: (0, 0)),
            pl.BlockSpec((tq, LANE), lambda g, i: (i, g)),
        ],
        out_specs=[
            pl.BlockSpec((tq, gw), lambda g, i: (i, g)),
            pl.BlockSpec((1, tq, nsb), lambda g, i: (g, i, 0)),
        ],
        out_shape=[jax.ShapeDtypeStruct((s, NSA_WIDTH), F32),
                   jax.ShapeDtypeStruct((N_KV_NSA, s, nsb), BF16)],
        scratch_shapes=[pltpu.VMEM((tq, ncp), F32), pltpu.VMEM((tq, nsb), F32)],
        compiler_params=_cparams(("arbitrary", "arbitrary")),
        name="nsa_cmp_select",
    )(proj, cmp_kv, cmp_kv, overlap, gates)


def _lane_chunks(s):
    return [s[:, c * LANE:(c + 1) * LANE] for c in range(s.shape[1] // LANE)]


def _row_max(chunks):
    m = functools.reduce(jnp.maximum, chunks)
    return jnp.broadcast_to(jnp.max(m, axis=-1, keepdims=True), m.shape)


def _softmax_weights(chunks, m):
    return jnp.concatenate([jnp.exp2((c - m) * EXP2_SCALE) for c in chunks], axis=1).astype(BF16)


def _flash_init(s, v_aug, m_scr, acc_scr, idx):
    chunks = _lane_chunks(s)
    m = _row_max(chunks)
    acc_scr[idx] = _dot(_softmax_weights(chunks, m), v_aug)
    m_scr[idx] = m


def _flash_step(s, v_aug, m_scr, acc_scr, idx):
    chunks = _lane_chunks(s)
    m_old = m_scr[idx]
    m_new = jnp.maximum(m_old, _row_max(chunks))
    alpha = jnp.exp2((m_old - m_new) * EXP2_SCALE)
    pv = _dot(_softmax_weights(chunks, m_new), v_aug)
    acc = acc_scr[idx]
    acc_scr[idx] = jnp.concatenate([alpha * a for a in _lane_chunks(acc)], axis=1) + pv
    m_scr[idx] = m_new


def _with_ones(v):
    return jnp.concatenate([v, jnp.ones(v.shape, v.dtype)], axis=1)


def _block_onehot(k0, tk, block, width):
    pos = k0 + lax.broadcasted_iota(jnp.int32, (tk, 1), 0)
    blk = lax.shift_right_logical(pos, block.bit_length() - 1) & (width - 1)
    return (blk == lax.broadcasted_iota(jnp.int32, (1, width), 1)).astype(BF16)


def _nsa_slc_kernel(q_ref, ks_ref, vs_ref, kw0_ref, kw1_ref, kw2_ref, vw0_ref, vw1_ref, vw2_ref,
                    sel_ref, gate_ref, ocmp_ref, o_ref, ka_scr, va_scr, qa_scr, m_scr, acc_scr, sa_scr, sb_scr):
    i = pl.program_id(1)
    tq = q_ref.shape[0]
    nsb = sel_ref.shape[2]
    s_len = ks_ref.shape[0]
    tk = min(SLC_KEY_TILE, s_len)
    n_kt = s_len // tk
    width = min(nsb, LANE)
    nhalf = max(nsb // LANE, 1)
    kt_per_half = (LANE * SLC_BLOCK) // tk
    t = i * tq + lax.broadcasted_iota(jnp.int32, (tq, 1), 0)
    kt_diag = (i * tq) // tk

    @pl.when(i == 0)
    def _():
        def build(n, carry):
            k0 = pl.multiple_of(n * tk, tk)
            ka_scr[pl.ds(k0, tk), :] = jnp.concatenate(
                [ks_ref[pl.ds(k0, tk), :], _block_onehot(k0, tk, SLC_BLOCK, width)], axis=1)
            va_scr[pl.ds(k0, tk), :] = _with_ones(vs_ref[pl.ds(k0, tk), :])
            return carry

        lax.fori_loop(0, n_kt, build, 0)
        ka_scr[pl.ds(s_len, tk), :] = jnp.concatenate(
            [jnp.zeros((tk, HEAD_DIM), BF16), jnp.ones((tk, width), BF16)], axis=1)
        va_scr[pl.ds(s_len, tk), :] = jnp.zeros((tk, 2 * HEAD_DIM), BF16)

    bias = ((1.0 - sel_ref[0].astype(F32)) * NEG_INF).astype(BF16)
    for half in range(nhalf):
        for r in range(NSA_REP):
            qa_scr[half, r * tq:(r + 1) * tq, :] = jnp.concatenate(
                [q_ref[:, r * HEAD_DIM:(r + 1) * HEAD_DIM], bias[:, half * width:(half + 1) * width]], axis=1)
    t_all = jnp.concatenate([t] * NSA_REP, axis=0)

    def scores(kt):
        half = jnp.where(kt < n_kt, kt // kt_per_half, 0)
        return _dot_nt(qa_scr[half], ka_scr[pl.ds(pl.multiple_of(kt * tk, tk), tk), :])

    def values(kt):
        return va_scr[pl.ds(pl.multiple_of(kt * tk, tk), tk), :]

    kpos = kt_diag * tk + lax.broadcasted_iota(jnp.int32, (1, tk), 1)
    _flash_init(jnp.where(kpos <= t_all, scores(kt_diag), NEG_INF), values(kt_diag), m_scr, acc_scr, Ellipsis)

    last = jnp.maximum(kt_diag - 1, 0)
    sa_scr[...] = scores(0)

    def tile_pair(j, carry):
        kt_b = jnp.where(2 * j + 1 < kt_diag, 2 * j + 1, n_kt)
        sb_scr[...] = scores(kt_b)
        _flash_step(sa_scr[...], values(2 * j), m_scr, acc_scr, Ellipsis)
        sa_scr[...] = scores(jnp.minimum(2 * j + 2, last))
        _flash_step(sb_scr[...], values(kt_b), m_scr, acc_scr, Ellipsis)
        return carry

    lax.fori_loop(0, (kt_diag + 1) // 2, tile_pair, 0)

    kwin = jnp.concatenate([kw0_ref[...], kw1_ref[...], kw2_ref[...]], axis=0)
    vwin = _with_ones(jnp.concatenate([vw0_ref[...], vw1_ref[...], vw2_ref[...]], axis=0))
    nw = kwin.shape[0]
    wpos = (i - 2) * tq + lax.broadcasted_iota(jnp.int32, (1, nw), 1)
    wvalid = (wpos <= t_all) & (wpos > t_all - WINDOW) & (wpos >= 0)
    s = jnp.where(wvalid, _dot_nt(qa_scr[0, :, :HEAD_DIM], kwin), NEG_INF)
    m = jnp.max(s, axis=-1, keepdims=True)
    ow = _dot(jnp.exp2((s - m) * EXP2_SCALE).astype(BF16), vwin)
    o_win = ow[:, :HEAD_DIM] / ow[:, HEAD_DIM:]
    acc = acc_scr[...]
    o_slc = acc[:, :HEAD_DIM] / acc[:, HEAD_DIM:]
    gates = jax.nn.sigmoid(gate_ref[...])
    for r in range(NSA_REP):
        rows = slice(r * tq, (r + 1) * tq)
        o = (ocmp_ref[:, r * HEAD_DIM:(r + 1) * HEAD_DIM]
             + gates[:, 3 * r + 1:3 * r + 2] * o_slc[rows] + gates[:, 3 * r + 2:3 * r + 3] * o_win[rows])
        o_ref[:, r * HEAD_DIM:(r + 1) * HEAD_DIM] = o.astype(BF16)


def _nsa_slc(proj, sel, gates, ocmp):
    s = proj.shape[0]
    nsb = sel.shape[2]
    tq = min(Q_TILE, s)
    tk = min(SLC_KEY_TILE, s)
    gw = NSA_REP * HEAD_DIM
    assert WINDOW == 2 * tq and (nsb <= LANE or nsb % LANE == 0)

    def win_spec(col, d):
        return pl.BlockSpec((tq, HEAD_DIM), lambda g, i: (jnp.maximum(i - 2 + d, 0), col + g))

    return pl.pallas_call(
        _nsa_slc_kernel,
        grid=(N_KV_NSA, s // tq),
        in_specs=[
            pl.BlockSpec((tq, gw), lambda g, i: (i, g)),
            pl.BlockSpec((s, HEAD_DIM), lambda g, i: (0, COL_KS + g)),
            pl.BlockSpec((s, HEAD_DIM), lambda g, i: (0, COL_VS + g)),
            win_spec(COL_KW, 0), win_spec(COL_KW, 1), win_spec(COL_KW, 2),
            win_spec(COL_VW, 0), win_spec(COL_VW, 1), win_spec(COL_VW, 2),
            pl.BlockSpec((1, tq, nsb), lambda g, i: (g, i, 0)),
            pl.BlockSpec((tq, LANE), lambda g, i: (i, g)),
            pl.BlockSpec((tq, gw), lambda g, i: (i, g)),
        ],
        out_specs=pl.BlockSpec((tq, gw), lambda g, i: (i, g)),
        out_shape=jax.ShapeDtypeStruct((s, NSA_WIDTH), BF16),
        scratch_shapes=[pltpu.VMEM((s + tk, HEAD_DIM + min(nsb, LANE)), BF16),
                        pltpu.VMEM((s + tk, 2 * HEAD_DIM), BF16),
                        pltpu.VMEM((max(nsb // LANE, 1), NSA_REP * tq, HEAD_DIM + min(nsb, LANE)), BF16),
                        pltpu.VMEM((NSA_REP * tq, LANE), F32),
                        pltpu.VMEM((NSA_REP * tq, 2 * HEAD_DIM), F32),
                        pltpu.VMEM((NSA_REP * tq, tk), F32), pltpu.VMEM((NSA_REP * tq, tk), F32)],
        compiler_params=_cparams(("arbitrary", "arbitrary")),
        name="nsa_slc_win",
    )(proj, proj, proj, proj, proj, proj, proj, proj, proj, sel, gates, ocmp)


def _moba_kernel(q_ref, k_ref, v_ref, o_ref, ka_scr, va_scr, qa_scr, km_scr, m_scr, acc_scr, sa_scr, sb_scr):
    i = pl.program_id(1)
    tq = q_ref.shape[0]
    s_len = k_ref.shape[0]
    nbp = km_scr.shape[0]
    tk = min(MOBA_KEY_TILE, tq)
    bpt = tk // MOBA_BLOCK
    n_kt = s_len // tk

    @pl.when(i == 0)
    def _():
        km_scr[...] = jnp.zeros(km_scr.shape, F32)

        def build(n, carry):
            k0 = pl.multiple_of(n * tk, tk)
            kt = k_ref[pl.ds(k0, tk), :]
            ka_scr[pl.ds(k0, tk), :] = jnp.concatenate([kt, _block_onehot(k0, tk, MOBA_BLOCK, nbp)], axis=1)
            va_scr[pl.ds(k0, tk), :] = _with_ones(v_ref[pl.ds(k0, tk), :])
            for j in range(bpt):
                kb = kt[j * MOBA_BLOCK:(j + 1) * MOBA_BLOCK].astype(F32)
                km_scr[pl.ds(n * bpt + j, 1), :] = jnp.mean(kb, axis=0, keepdims=True)
            return carry

        lax.fori_loop(0, n_kt, build, 0)
        ka_scr[pl.ds(s_len, tk), :] = jnp.concatenate(
            [jnp.zeros((tk, HEAD_DIM), BF16), jnp.ones((tk, nbp), BF16)], axis=1)
        va_scr[pl.ds(s_len, tk), :] = jnp.zeros((tk, 2 * HEAD_DIM), BF16)

    q = q_ref[...]
    km_hi, km_lo = _split_bf16(km_scr[...])
    gate = _dot_nt(q, km_hi) + _dot_nt(q, km_lo)
    t = i * tq + lax.broadcasted_iota(jnp.int32, (tq, 1), 0)
    own = lax.shift_right_logical(t, MOBA_BLOCK.bit_length() - 1)
    nidx = lax.broadcasted_iota(jnp.int32, (1, nbp), 1)
    nf = nidx.astype(F32)
    work = jnp.where(nidx < own, gate, LOWEST)
    sel = nidx == own
    for _ in range(MOBA_TOPK):
        m = jnp.max(work, axis=-1, keepdims=True)
        first = jnp.min(jnp.where((work == m) & (m > 0.5 * LOWEST), nf, 1e9), axis=-1, keepdims=True)
        hit = nf == first
        sel = sel | hit
        work = jnp.where(hit, LOWEST, work)
    qa_scr[...] = jnp.concatenate([q, jnp.where(sel, 0.0, NEG_INF).astype(BF16)], axis=1)

    def scores(kt):
        return _dot_nt(qa_scr[...], ka_scr[pl.ds(pl.multiple_of(kt * tk, tk), tk), :])

    def values(kt):
        return va_scr[pl.ds(pl.multiple_of(kt * tk, tk), tk), :]

    d0 = pl.multiple_of(i * tq, tq)
    kpos = d0 + lax.broadcasted_iota(jnp.int32, (1, tq), 1)
    s_diag = _dot_nt(qa_scr[...], ka_scr[pl.ds(d0, tq), :])
    _flash_init(jnp.where(kpos <= t, s_diag, NEG_INF), va_scr[pl.ds(d0, tq), :], m_scr, acc_scr, Ellipsis)

    n_past = i * (tq // tk)
    last = jnp.maximum(n_past - 1, 0)
    sa_scr[...] = scores(0)

    def tile_pair(j, carry):
        kt_b = jnp.where(2 * j + 1 < n_past, 2 * j + 1, n_kt)
        sb_scr[...] = scores(kt_b)
        _flash_step(sa_scr[...], values(2 * j), m_scr, acc_scr, Ellipsis)
        sa_scr[...] = scores(jnp.minimum(2 * j + 2, last))
        _flash_step(sb_scr[...], values(kt_b), m_scr, acc_scr, Ellipsis)
        return carry

    lax.fori_loop(0, (n_past + 1) // 2, tile_pair, 0)
    acc = acc_scr[...]
    o_ref[...] = (acc[:, :HEAD_DIM] / acc[:, HEAD_DIM:]).astype(BF16)


def _moba(proj):
    s = proj.shape[0]
    tq = min(MOBA_Q_TILE, s)
    tk = min(MOBA_KEY_TILE, tq)
    nbp = max(LANE, s // MOBA_BLOCK)
    assert nbp & (nbp - 1) == 0 and tq % tk == 0
    return pl.pallas_call(
        _moba_kernel,
        grid=(N_HEADS_MOBA, s // tq),
        in_specs=[
            pl.BlockSpec((tq, HEAD_DIM), lambda h, i: (i, COL_QM + h)),
            pl.BlockSpec((s, HEAD_DIM), lambda h, i: (0, COL_KM + h)),
            pl.BlockSpec((s, HEAD_DIM), lambda h, i: (0, COL_VM + h)),
        ],
        out_specs=pl.BlockSpec((tq, HEAD_DIM), lambda h, i: (i, h)),
        out_shape=jax.ShapeDtypeStruct((s, MOBA_WIDTH), BF16),
        scratch_shapes=[pltpu.VMEM((s + tk, HEAD_DIM + nbp), BF16), pltpu.VMEM((s + tk, 2 * HEAD_DIM), BF16),
                        pltpu.VMEM((tq, HEAD_DIM + nbp), BF16),
                        pltpu.VMEM((nbp, HEAD_DIM), F32), pltpu.VMEM((tq, LANE), F32),
                        pltpu.VMEM((tq, 2 * HEAD_DIM), F32),
                        pltpu.VMEM((tq, tk), F32), pltpu.VMEM((tq, tk), F32)],
        compiler_params=_cparams(("arbitrary", "arbitrary")),
        name="moba",
    )(proj, proj, proj)


def _out_kernel(on_ref, om_ref, x_ref, wn_ref, wm_ref, gt_ref, g_ref, sc_ref, sh_ref, wrh_ref, wrl_ref,
                x1_ref, h_ref, lg_ref):
    mix = _dot(on_ref[...], wn_ref[...]) + _dot(om_ref[...], wm_ref[...])
    x1 = x_ref[...] + gt_ref[...] * mix
    x1_ref[...] = x1
    h = _rms_mod(x1, g_ref[...], sc_ref[...], sh_ref[...])
    h_hi, h_lo = _split_bf16(h)
    h_ref[...] = h
    wrh = wrh_ref[...]
    lg_ref[...] = _dot(h_hi, wrh) + _dot(h_lo, wrh) + _dot(h_hi, wrl_ref[...])


def _out_proj(o_nsa, o_moba, x, w_n, w_m, gt, g, sc, sh, wr_hi, wr_lo):
    s, d = x.shape
    tm = min(OUT_TM, s)
    row = lambda i: (i, 0)
    fix = lambda i: (0, 0)
    return pl.pallas_call(
        _out_kernel,
        grid=(s // tm,),
        in_specs=[
            pl.BlockSpec((tm, NSA_WIDTH), row),
            pl.BlockSpec((tm, MOBA_WIDTH), row),
            pl.BlockSpec((tm, d), row),
            pl.BlockSpec((NSA_WIDTH, d), fix),
            pl.BlockSpec((MOBA_WIDTH, d), fix),
            pl.BlockSpec((1, d), fix),
            pl.BlockSpec((1, d), fix),
            pl.BlockSpec((1, d), fix),
            pl.BlockSpec((1, d), fix),
            pl.BlockSpec((d, LANE), fix),
            pl.BlockSpec((d, LANE), fix),
        ],
        out_specs=[pl.BlockSpec((tm, d), row), pl.BlockSpec((tm, d), row), pl.BlockSpec((tm, LANE), row)],
        out_shape=[jax.ShapeDtypeStruct((s, d), F32), jax.ShapeDtypeStruct((s, d), F32),
                   jax.ShapeDtypeStruct((s, LANE), F32)],
        compiler_params=_cparams(("arbitrary",)),
        name="out_proj_norm_router",
    )(o_nsa, o_moba, x, w_n, w_m, gt, g, sc, sh, wr_hi, wr_lo)


def _row_copy(src_ref, dst_ref, src_row, dst_row, sem):
    return pltpu.make_async_copy(src_ref.at[pl.ds(src_row, 1)], dst_ref.at[pl.ds(dst_row, 1)], sem)


def _expert_kernel(be_ref, nu_ref, tok_ref, h_ref, wg_ref, wu_ref, wd_ref, y_ref,
                   x_buf, wg_s, wu_s, wd_s, sems):
    b = pl.program_id(0)
    n_used = nu_ref[0]

    def gather(blk, buf, wait):
        def row(r, carry):
            cp = _row_copy(h_ref, x_buf.at[buf], tok_ref[blk * MOE_ROWS + r], r, sems.at[buf])
            if wait:
                cp.wait()
            else:
                cp.start()
            return carry

        lax.fori_loop(0, MOE_ROWS, row, 0, unroll=DMA_UNROLL)

    @pl.when((b == 0) & (n_used > 0))
    def _():
        gather(0, 0, False)

    prev = be_ref[jnp.maximum(b - 1, 0)]

    @pl.when((b == 0) | (be_ref[b] != prev))
    def _():
        wg_s[...] = wg_ref[0, 0].astype(BF16)
        wu_s[...] = wu_ref[0, 0].astype(BF16)
        wd_s[...] = wd_ref[0, 0].astype(BF16)

    @pl.when(b < n_used)
    def _():
        gather(b, b % 2, True)
        nxt = jnp.minimum(b + 1, n_used - 1)
        for r in range(MOE_ROWS):
            _row_copy(h_ref, x_buf.at[(b + 1) % 2], tok_ref[nxt * MOE_ROWS + r], r, sems.at[(b + 1) % 2]).start()
        x = x_buf[b % 2].astype(BF16)
        a = _dot(x, wg_s[...])
        u = _dot(x, wu_s[...])
        hid = (a * jax.nn.sigmoid(a) * u).astype(BF16)
        y_ref[...] = _dot(hid, wd_s[...])

    @pl.when(b + 1 == n_used)
    def _():
        gather(b, (b + 1) % 2, True)

    @pl.when(b >= n_used)
    def _():
        y_ref[...] = jnp.zeros(y_ref.shape, F32)


def _experts(h, slot_tok, blk_exp, n_used, w_gate, w_up, w_down, layer):
    n_slots = slot_tok.shape[0]
    d = h.shape[1]
    n_blk = n_slots // MOE_ROWS
    grid_spec = pltpu.PrefetchScalarGridSpec(
        num_scalar_prefetch=3,
        grid=(n_blk,),
        in_specs=[
            pl.BlockSpec(memory_space=pl.ANY),
            pl.BlockSpec((1, 1, d, D_EXPERT), lambda b, be, nu, tok: (layer, be[b], 0, 0)),
            pl.BlockSpec((1, 1, d, D_EXPERT), lambda b, be, nu, tok: (layer, be[b], 0, 0)),
            pl.BlockSpec((1, 1, D_EXPERT, d), lambda b, be, nu, tok: (layer, be[b], 0, 0)),
        ],
        out_specs=pl.BlockSpec((MOE_ROWS, d), lambda b, be, nu, tok: (b, 0)),
        scratch_shapes=[pltpu.VMEM((2, MOE_ROWS, d), F32),
                        pltpu.VMEM((d, D_EXPERT), BF16), pltpu.VMEM((d, D_EXPERT), BF16),
                        pltpu.VMEM((D_EXPERT, d), BF16), pltpu.SemaphoreType.DMA((2,))],
    )
    return pl.pallas_call(
        _expert_kernel,
        grid_spec=grid_spec,
        out_shape=jax.ShapeDtypeStruct((n_slots, d), F32),
        compiler_params=_cparams(("arbitrary",)),
        name="moe_experts",
    )(blk_exp, n_used, slot_tok, h, w_gate, w_up, w_down)


def _combine_kernel(sa_ref, sb_ref, na_ref, nb_ref, x_ref, w_ref, gt_ref, gf_ref, y_ref, o_ref, buf, sems, *,
                    final):
    b = pl.program_id(0)
    tt = x_ref.shape[0]

    def gather(ia_ref, ib_ref, slot, wait):
        def row(r, carry):
            for k, idx_ref in enumerate((ia_ref, ib_ref)):
                cp = _row_copy(y_ref, buf.at[slot].at[k], idx_ref[0, 0, r], r, sems.at[slot].at[k])
                if wait:
                    cp.wait()
                else:
                    cp.start()
            return carry

        lax.fori_loop(0, tt, row, 0, unroll=DMA_UNROLL)

    @pl.when(b == 0)
    def _():
        gather(sa_ref, sb_ref, 0, False)

    @pl.when(b + 1 < pl.num_programs(0))
    def _():
        gather(na_ref, nb_ref, (b + 1) % 2, False)

    gather(sa_ref, sb_ref, b % 2, True)
    w = w_ref[...]
    x2 = x_ref[...] + gt_ref[...] * (w[:, 0:1] * buf[b % 2, 0] + w[:, 1:2] * buf[b % 2, 1])
    if final:
        x2 = x2 * lax.rsqrt(jnp.mean(x2 * x2, axis=-1, keepdims=True) + RMS_EPS) * gf_ref[...]
    o_ref[...] = x2


def _combine(x1, ys, slot_a, slot_b, pair_w, gt, g_final, final):
    s, d = x1.shape
    tt = min(GATHER_ROWS, s)
    nt = s // tt
    cur_spec = pl.BlockSpec((1, 1, tt), lambda b: (b, 0, 0), memory_space=pltpu.SMEM)
    nxt_spec = pl.BlockSpec((1, 1, tt), lambda b: (jnp.minimum(b + 1, nt - 1), 0, 0), memory_space=pltpu.SMEM)
    slot_a = slot_a.reshape(nt, 1, tt)
    slot_b = slot_b.reshape(nt, 1, tt)
    return pl.pallas_call(
        functools.partial(_combine_kernel, final=final),
        grid=(nt,),
        in_specs=[
            cur_spec, cur_spec, nxt_spec, nxt_spec,
            pl.BlockSpec((tt, d), lambda b: (b, 0)),
            pl.BlockSpec((tt, LANE), lambda b: (b, 0)),
            pl.BlockSpec((1, d), lambda b: (0, 0)),
            pl.BlockSpec((1, d), lambda b: (0, 0)),
            pl.BlockSpec(memory_space=pl.ANY),
        ],
        out_specs=pl.BlockSpec((tt, d), lambda b: (b, 0)),
        out_shape=jax.ShapeDtypeStruct((s, d), F32),
        scratch_shapes=[pltpu.VMEM((2, 2, tt, d), F32), pltpu.SemaphoreType.DMA((2, 2))],
        compiler_params=_cparams(("arbitrary",)),
        name="moe_combine",
    )(slot_a, slot_b, slot_a, slot_b, x1, jnp.pad(pair_w, ((0, 0), (0, LANE - TOP_K_EXPERTS))), gt, g_final, ys)


def _route(logits, b_router):
    n = logits.shape[0]
    scores = jax.nn.sigmoid(logits[:, :N_EXPERTS])
    biased = (scores + b_router.astype(F32)).reshape(n, N_GROUPS, EXPERTS_PER_GROUP)

    def top2(v):
        idx = lax.broadcasted_iota(jnp.int32, v.shape, v.ndim - 1)
        v1 = jnp.max(v, axis=-1, keepdims=True)
        i1 = jnp.min(jnp.where(v == v1, idx, v.shape[-1]), axis=-1, keepdims=True)
        rest = jnp.where(idx == i1, -jnp.inf, v)
        v2 = jnp.max(rest, axis=-1, keepdims=True)
        i2 = jnp.min(jnp.where(rest == v2, idx, v.shape[-1]), axis=-1, keepdims=True)
        return v1, v2, i1, i2

    g1, g2, _, _ = top2(biased)
    g_sum = (g1 + g2)[..., 0]
    g_idx = lax.broadcasted_iota(jnp.int32, g_sum.shape, 1)
    grp = jnp.min(jnp.where(g_sum == jnp.max(g_sum, axis=-1, keepdims=True), g_idx, N_GROUPS), axis=-1)
    grp_hot = grp[:, None] == jnp.arange(N_GROUPS, dtype=grp.dtype)[None, :]
    in_grp = jnp.sum(jnp.where(grp_hot[:, :, None], biased, 0.0), axis=1)
    _, _, l1, l2 = top2(in_grp)
    expert = (grp[:, None] * EXPERTS_PER_GROUP + jnp.concatenate([l1, l2], axis=1)).astype(jnp.int32)
    e_hot = expert[:, :, None] == jnp.arange(N_EXPERTS, dtype=jnp.int32)[None, None, :]
    wsel = jnp.sum(jnp.where(e_hot, scores[:, None, :], 0.0), axis=-1)
    wsel = wsel / jnp.sum(wsel, axis=-1, keepdims=True)

    n_pair = n * TOP_K_EXPERTS
    flat_e = expert.reshape(-1)
    flat_t = jnp.repeat(jnp.arange(n, dtype=jnp.int32), TOP_K_EXPERTS)
    hot = flat_e[:, None] == jnp.arange(N_EXPERTS, dtype=jnp.int32)[None, :]
    onehot = hot.astype(jnp.int32)
    chunk = min(LANE, n_pair)
    n_chunk = n_pair // chunk
    tri = (jnp.arange(chunk)[:, None] >= jnp.arange(chunk)[None, :]).astype(BF16)
    within = jnp.einsum('ij,cjk->cik', tri, hot.astype(BF16).reshape(n_chunk, chunk, N_EXPERTS),
                        preferred_element_type=F32)
    totals = within[:, -1, :]
    before = jnp.cumsum(totals, axis=0) - totals
    running = (within + before[:, None, :]).astype(jnp.int32).reshape(n_pair, N_EXPERTS)
    rank = jnp.sum(running * onehot, axis=1) - 1
    counts = running[-1]
    padded = (counts + MOE_ROWS - 1) // MOE_ROWS * MOE_ROWS
    pend = jnp.cumsum(padded)
    pstart = pend - padded
    slot = (jnp.sum(pstart[None, :] * onehot, axis=1) + rank).astype(jnp.int32)
    n_slots = n_pair + N_EXPERTS * MOE_ROWS
    slot_tok = jnp.zeros((n_slots,), jnp.int32).at[slot].set(flat_t)
    pair_slot = slot.reshape(n, TOP_K_EXPERTS)
    n_blk = n_slots // MOE_ROWS
    blk_start = jnp.arange(n_blk, dtype=jnp.int32) * MOE_ROWS
    blk_exp = jnp.minimum(jnp.sum((pend[None, :] <= blk_start[:, None]).astype(jnp.int32), axis=1), N_EXPERTS - 1)
    n_used = (pend[-1:] // MOE_ROWS).astype(jnp.int32)
    return slot_tok, wsel, pair_slot, blk_exp, n_used


def _rope_tables(pos):
    inv = 1.0 / (ROPE_THETA ** (jnp.arange(0, HEAD_DIM, 2, dtype=F32) / HEAD_DIM))
    ang = pos.astype(F32)[:, None] * inv[None, :]
    cos = jnp.cos(ang)
    sin = jnp.sin(ang)
    return jnp.concatenate([cos, cos], axis=-1), jnp.concatenate([-sin, sin], axis=-1)


def _layout_w_in(w_in):
    o_g = NSA_WIDTH + 6 * NSA_KV_WIDTH
    main = jnp.concatenate([w_in[:, :o_g], w_in[:, o_g + N_GATES:]], axis=1).astype(BF16)
    wg = w_in[:, o_g:o_g + N_GATES].reshape(-1, N_KV_NSA, 3 * NSA_REP)
    wg = jnp.pad(wg, ((0, 0), (0, 0), (0, LANE - 3 * NSA_REP))).reshape(-1, N_KV_NSA * LANE)
    return main, wg.astype(BF16)


def _overlap(ncp, nsb):
    cs = jnp.arange(ncp, dtype=jnp.int32)[:, None] * CMP_STRIDE
    ss = jnp.arange(nsb, dtype=jnp.int32)[None, :] * SLC_BLOCK
    return ((cs + CMP_LEN - 1 >= ss) & (cs < ss + SLC_BLOCK)).astype(BF16)


def kernel(x, c, w_ada, b_ada, g_attn, g_ffn, w_in, w_out, cmp_pe, cmp_w1, cmp_b1, cmp_w2,
           w_router, b_router, w_gate, w_up, w_down, g_final):
    b, s, d = x.shape
    assert b == 1 and d == D_MODEL and s % max(Q_TILE, SLC_KEY_TILE, MOBA_Q_TILE) == 0
    depth = w_ada.shape[0]
    ncp = s // CMP_STRIDE
    nsb = s // SLC_BLOCK

    cos, sin = _rope_tables(jnp.arange(s))
    cos_c, sin_c = _rope_tables(jnp.arange(ncp) * CMP_STRIDE + CMP_LEN - 1)
    overlap = _overlap(ncp, nsb)
    flags = jnp.asarray(ROPE_FLAGS, jnp.int32)
    wr = jnp.pad(w_router.astype(F32), ((0, 0), (0, LANE - N_EXPERTS)))
    wr_hi, wr_lo = _split_bf16(wr)

    mod = _ada_mod(c.astype(F32), w_ada, b_ada)
    xs = x.reshape(s, d)
    for l in range(depth):
        sh1, sc1, gt1, sh2, sc2, gt2 = [mod[l, :, k * d:(k + 1) * d] for k in range(6)]
        w_main, w_gates = _layout_w_in(w_in[l])
        proj, gates = _project(xs, g_attn[l].reshape(1, d), sc1, sh1, w_main, w_gates, cos, sin, flags)

        chunks = proj[:, COL_KC * LANE:COL_KS * LANE].reshape(s, 2 * N_KV_NSA, HEAD_DIM)
        chunks = chunks.transpose(1, 0, 2).reshape(2 * N_KV_NSA, ncp, CMP_STRIDE * HEAD_DIM)
        pe8 = jnp.pad(cmp_pe[l].reshape(2, 1, CMP_LEN * HEAD_DIM), ((0, 0), (0, 7), (0, 0)))
        cmp_kv = _compress(chunks, cmp_w1[l], cmp_b1[l].reshape(2, 1, CMP_HIDDEN), cmp_w2[l], pe8, cos_c, sin_c)

        ocmp, sel = _nsa_cmp(proj, cmp_kv, overlap, gates)
        o_nsa = _nsa_slc(proj, sel, gates, ocmp)
        o_moba = _moba(proj)

        w_o = w_out[l].astype(BF16)
        x1, h2, logits = _out_proj(o_nsa, o_moba, xs, w_o[:NSA_WIDTH], w_o[NSA_WIDTH:], gt1,
                                   g_ffn[l].reshape(1, d), sc2, sh2, wr_hi, wr_lo)

        slot_tok, pair_w, pair_slot, blk_exp, n_used = _route(logits, b_router)
        ys = _experts(h2, slot_tok, blk_exp, n_used, w_gate, w_up, w_down, l)
        xs = _combine(x1, ys, pair_slot[:, 0], pair_slot[:, 1], pair_w, gt2, g_final.reshape(1, d),
                      l == depth - 1)
    return xs.reshape(b, s, d)
```

```python
import functools

import jax
import jax.numpy as jnp
from jax import lax
from jax.experimental import pallas as pl
from jax.experimental.pallas import tpu as pltpu

F32 = jnp.float32
BF16 = jnp.bfloat16

D_MODEL = 2048
HEAD_DIM = 128
N_HEADS_NSA = 8
N_KV_NSA = 2
NSA_REP = N_HEADS_NSA // N_KV_NSA
N_HEADS_MOBA = 8
NSA_WIDTH = N_HEADS_NSA * HEAD_DIM
NSA_KV_WIDTH = N_KV_NSA * HEAD_DIM
MOBA_WIDTH = N_HEADS_MOBA * HEAD_DIM
N_GATES = 3 * N_HEADS_NSA
ROPE_THETA = 10000.0
CMP_LEN = 32
CMP_STRIDE = 16
CMP_HIDDEN = 256
SLC_BLOCK = 64
SLC_TOPN = 16
N_FORCED = 3
WINDOW = 512
MOBA_BLOCK = 256
MOBA_TOPK = 3
N_EXPERTS = 32
N_GROUPS = 4
EXPERTS_PER_GROUP = N_EXPERTS // N_GROUPS
TOP_K_EXPERTS = 2
D_EXPERT = 512
RMS_EPS = 1e-6
NEG_INF = -1e30
LOWEST = -3e38
SCALE = HEAD_DIM ** -0.5
EXP2_SCALE = SCALE * 1.4426950408889634

LANE = 128
Q_TILE = 256
MOBA_Q_TILE = 1024
MOBA_KEY_TILE = 512
SLC_KEY_TILE = 512
CMP_BAND = 256
CMP_ROW_TILE = 256
MOE_ROWS = 256
GATHER_ROWS = 256
DMA_UNROLL = 8
EXPERT_BUFFERS = 3
PROJ_TM = 1024
PROJ_TN = 512
OUT_TM = 256
VMEM_LIMIT = 56 * 1024 * 1024

COL_QN = 0
COL_KC = 8
COL_VC = 10
COL_KS = 12
COL_VS = 14
COL_KW = 16
COL_VW = 18
COL_QM = 20
COL_KM = 28
COL_VM = 36
PROJ_COLS = 44
ROPE_FLAGS = tuple(
    1 if (c < 8 or c in (12, 13, 16, 17) or 20 <= c < 36) else 0 for c in range(PROJ_COLS))


def _cparams(sem):
    return pltpu.CompilerParams(dimension_semantics=sem, vmem_limit_bytes=VMEM_LIMIT)


def _dot(a, b):
    return jnp.dot(a, b, preferred_element_type=F32)


def _dot_nt(a, b):
    return lax.dot_general(a, b, (((1,), (1,)), ((), ())), preferred_element_type=F32)


def _split_bf16(v):
    hi = v.astype(BF16)
    lo = (v - hi.astype(F32)).astype(BF16)
    return hi, lo


def _mod_kernel(c_ref, w_ref, b_ref, o_ref):
    c = c_ref[...]
    cond = c * jax.nn.sigmoid(c)
    o_ref[0] = jnp.sum(w_ref[0] * cond, axis=0, keepdims=True) + b_ref[0]


def _ada_mod(c, w_ada, b_ada):
    depth, d, n = w_ada.shape
    tn = 1024
    return pl.pallas_call(
        _mod_kernel,
        grid=(depth, n // tn),
        in_specs=[
            pl.BlockSpec((d, 1), lambda l, j: (0, 0)),
            pl.BlockSpec((1, d, tn), lambda l, j: (l, 0, j)),
            pl.BlockSpec((1, 1, tn), lambda l, j: (l, 0, j)),
        ],
        out_specs=pl.BlockSpec((1, 1, tn), lambda l, j: (l, 0, j)),
        out_shape=jax.ShapeDtypeStruct((depth, 1, n), F32),
        compiler_params=_cparams(("arbitrary", "arbitrary")),
        name="ada_mod",
    )(c.reshape(d, 1), w_ada, b_ada.reshape(depth, 1, n))


def _rms_mod(x, g, sc, sh):
    y = x * lax.rsqrt(jnp.mean(x * x, axis=-1, keepdims=True) + RMS_EPS)
    return (y * g) * (1.0 + sc) + sh


def _proj_kernel(flags_ref, x_ref, g_ref, sc_ref, sh_ref, w_ref, wg_ref, cos_ref, sin_ref,
                 o_ref, gate_ref, h_scr):
    j = pl.program_id(1)

    @pl.when(j == 0)
    def _():
        hb = _rms_mod(x_ref[...], g_ref[...], sc_ref[...], sh_ref[...]).astype(BF16)
        h_scr[...] = hb
        gate_ref[...] = _dot(hb, wg_ref[...])

    acc = _dot(h_scr[...], w_ref[...])
    cos = cos_ref[...]
    sin = sin_ref[...]
    nch = PROJ_TN // LANE
    for c in range(nch):
        a = acc[:, c * LANE:(c + 1) * LANE]
        roped = a * cos + pltpu.roll(a, HEAD_DIM // 2, 1) * sin
        f = flags_ref[j * nch + c]
        o_ref[:, c * LANE:(c + 1) * LANE] = jnp.where(f > 0, roped, a).astype(BF16)


def _project(x, g, sc, sh, w_main, w_gate, cos, sin, flags):
    s, d = x.shape
    n = w_main.shape[1]
    tm = min(PROJ_TM, s)
    grid_spec = pltpu.PrefetchScalarGridSpec(
        num_scalar_prefetch=1,
        grid=(s // tm, n // PROJ_TN),
        in_specs=[
            pl.BlockSpec((tm, d), lambda i, j, f: (i, 0)),
            pl.BlockSpec((1, d), lambda i, j, f: (0, 0)),
            pl.BlockSpec((1, d), lambda i, j, f: (0, 0)),
            pl.BlockSpec((1, d), lambda i, j, f: (0, 0)),
            pl.BlockSpec((d, PROJ_TN), lambda i, j, f: (0, j)),
            pl.BlockSpec((d, 2 * LANE), lambda i, j, f: (0, 0)),
            pl.BlockSpec((tm, LANE), lambda i, j, f: (i, 0)),
            pl.BlockSpec((tm, LANE), lambda i, j, f: (i, 0)),
        ],
        out_specs=[
            pl.BlockSpec((tm, PROJ_TN), lambda i, j, f: (i, j)),
            pl.BlockSpec((tm, 2 * LANE), lambda i, j, f: (i, 0)),
        ],
        scratch_shapes=[pltpu.VMEM((tm, d), BF16)],
    )
    return pl.pallas_call(
        _proj_kernel,
        grid_spec=grid_spec,
        out_shape=[jax.ShapeDtypeStruct((s, n), BF16), jax.ShapeDtypeStruct((s, 2 * LANE), F32)],
        compiler_params=_cparams(("arbitrary", "arbitrary")),
        name="norm_proj_rope",
    )(flags, x, g, sc, sh, w_main, w_gate, cos, sin)


def _compress_kernel(ch_ref, w1_ref, b1_ref, w2_ref, pe_ref, cos_ref, sin_ref, o_ref):
    a = pl.program_id(0)
    ch = ch_ref[0]
    nc = ch.shape[0]
    half = CMP_STRIDE * HEAD_DIM
    w1 = w1_ref[0].astype(BF16)
    top = _dot(ch, w1[:half])
    bot = _dot(ch, w1[half:])
    bot = pltpu.roll(bot, nc - 1, 0)
    pe_term = _dot(pe_ref[0].astype(BF16), w1)[0:1]
    hid = jax.nn.gelu(top + bot + pe_term + b1_ref[0])
    out = _dot(hid.astype(BF16), w2_ref[0].astype(BF16))
    roped = out * cos_ref[...] + pltpu.roll(out, HEAD_DIM // 2, 1) * sin_ref[...]
    out = jnp.where(a < N_KV_NSA, roped, out)
    row = lax.broadcasted_iota(jnp.int32, out.shape, 0)
    o_ref[0] = jnp.where(row < nc - 1, out, 0.0).astype(BF16)


def _compress(chunks, w1, b1, w2, pe8, cos_c, sin_c):
    na, nc, kw = chunks.shape
    return pl.pallas_call(
        _compress_kernel,
        grid=(na,),
        in_specs=[
            pl.BlockSpec((1, nc, kw), lambda a: (a, 0, 0)),
            pl.BlockSpec((1, CMP_LEN * HEAD_DIM, CMP_HIDDEN), lambda a: (a // N_KV_NSA, 0, 0)),
            pl.BlockSpec((1, 1, CMP_HIDDEN), lambda a: (a // N_KV_NSA, 0, 0)),
            pl.BlockSpec((1, CMP_HIDDEN, HEAD_DIM), lambda a: (a // N_KV_NSA, 0, 0)),
            pl.BlockSpec((1, 8, CMP_LEN * HEAD_DIM), lambda a: (a // N_KV_NSA, 0, 0)),
            pl.BlockSpec((nc, HEAD_DIM), lambda a: (0, 0)),
            pl.BlockSpec((nc, HEAD_DIM), lambda a: (0, 0)),
        ],
        out_specs=pl.BlockSpec((1, nc, HEAD_DIM), lambda a: (a, 0, 0)),
        out_shape=jax.ShapeDtypeStruct((na, nc, HEAD_DIM), BF16),
        compiler_params=_cparams(("arbitrary",)),
        name="nsa_compress",
    )(chunks, w1, b1, w2, pe8, cos_c, sin_c)


def _nsa_cmp_band(q_ref, kc_ref, vc_ref, ov_ref, gate_ref, ocmp_ref, sel_ref, psum_scr, imp_scr, cw, sw):
    i = pl.program_id(1)
    tq = q_ref.shape[0]
    nsb = sel_ref.shape[2]
    t = i * tq + lax.broadcasted_iota(jnp.int32, (tq, 1), 0)
    cend = lax.broadcasted_iota(jnp.int32, (1, cw), 1) * CMP_STRIDE + (CMP_LEN - 1)
    mask = cend <= t
    kc = kc_ref[0, :cw, :]
    vc = vc_ref[0, :cw, :]
    gates = jax.nn.sigmoid(gate_ref[...])
    for r in range(NSA_REP):
        q = q_ref[:, r * HEAD_DIM:(r + 1) * HEAD_DIM]
        s = jnp.where(mask, _dot_nt(q, kc) * SCALE, NEG_INF)
        m = jnp.max(s, axis=-1, keepdims=True)
        e = jnp.where(mask, jnp.exp(s - m), 0.0)
        l = jnp.sum(e, axis=-1, keepdims=True)
        p = e / jnp.where(l > 0.0, l, 1.0)
        psum_scr[:, :cw] = p if r == 0 else psum_scr[:, :cw] + p
        o = _dot(p.astype(BF16), vc)
        ocmp_ref[:, r * HEAD_DIM:(r + 1) * HEAD_DIM] = gates[:, 3 * r:3 * r + 1] * o

    p_hi, p_lo = _split_bf16(psum_scr[:, :cw])
    ov = ov_ref[:cw, :sw]
    imp_scr[:, :sw] = _dot(p_hi, ov) + _dot(p_lo, ov)

    rt = min(CMP_ROW_TILE, tq)
    jblk = lax.broadcasted_iota(jnp.int32, (1, sw), 1)
    jf = jblk.astype(F32)

    def choose(rr, carry):
        r0 = pl.multiple_of(rr * rt, rt)
        tb = lax.shift_right_logical(i * tq + r0 + lax.broadcasted_iota(jnp.int32, (rt, 1), 0),
                                     SLC_BLOCK.bit_length() - 1)
        valid = jblk <= tb
        forced = valid & ((jblk == 0) | (jblk == tb) | (jblk == tb - 1))
        work = jnp.where(valid & jnp.logical_not(forced), imp_scr[pl.ds(r0, rt), :sw], LOWEST)
        sel = jnp.where(forced, 1.0, 0.0)
        for _ in range(min(SLC_TOPN, nsb) - N_FORCED):
            m = jnp.max(work, axis=-1, keepdims=True)
            first = jnp.min(jnp.where(work == m, jf, 1e9), axis=-1, keepdims=True)
            hit = jf == first
            sel = jnp.where(hit, 1.0, sel)
            work = jnp.where(hit, LOWEST, work)
        sel_ref[0, pl.ds(r0, rt), :sw] = sel.astype(BF16)
        return carry

    lax.fori_loop(0, tq // rt, choose, 0)
    if sw < nsb:
        sel_ref[0, :, sw:] = jnp.zeros((tq, nsb - sw), BF16)


def _nsa_cmp_kernel(q_ref, kc_ref, vc_ref, ov_ref, gate_ref, ocmp_ref, sel_ref, psum_scr, imp_scr):
    i = pl.program_id(1)
    tq = q_ref.shape[0]
    ncp = kc_ref.shape[1]
    nsb = sel_ref.shape[2]
    band_w = min(CMP_BAND, ncp)
    n_band = ncp // band_w
    band = jnp.minimum(((i + 1) * (tq // CMP_STRIDE) + band_w - 1) // band_w, n_band) - 1
    for b in range(n_band):
        cw = band_w * (b + 1)
        sw = min(nsb, -(-(cw * CMP_STRIDE // SLC_BLOCK) // LANE) * LANE)

        @pl.when(band == b)
        def _():
            _nsa_cmp_band(q_ref, kc_ref, vc_ref, ov_ref, gate_ref, ocmp_ref, sel_ref, psum_scr, imp_scr, cw, sw)


def _nsa_cmp(proj, cmp_kv, overlap, gates):
    s = proj.shape[0]
    ncp = cmp_kv.shape[1]
    nsb = overlap.shape[1]
    tq = min(Q_TILE, s)
    gw = NSA_REP * HEAD_DIM
    assert ncp % min(CMP_BAND, ncp) == 0
    return pl.pallas_call(
        _nsa_cmp_kernel,
        grid=(N_KV_NSA, s // tq),
        in_specs=[
            pl.BlockSpec((tq, gw), lambda g, i: (i, g)),
            pl.BlockSpec((1, ncp, HEAD_DIM), lambda g, i: (g, 0, 0)),
            pl.BlockSpec((1, ncp, HEAD_DIM), lambda g, i: (N_KV_NSA + g, 0, 0)),
            pl.BlockSpec((ncp, nsb), lambda g, i: (0, 0)),
            pl.BlockSpec((tq, LANE), lambda g, i: (i, g)),
        ],
        out_specs=[
            pl.BlockSpec((tq, gw), lambda g, i: (i, g)),
            pl.BlockSpec((1, tq, nsb), lambda g, i: (g, i, 0)),
        ],
        out_shape=[jax.ShapeDtypeStruct((s, NSA_WIDTH), F32),
                   jax.ShapeDtypeStruct((N_KV_NSA, s, nsb), BF16)],
        scratch_shapes=[pltpu.VMEM((tq, ncp), F32), pltpu.VMEM((tq, nsb), F32)],
        compiler_params=_cparams(("arbitrary", "arbitrary")),
        name="nsa_cmp_select",
    )(proj, cmp_kv, cmp_kv, overlap, gates)


def _lane_chunks(s):
    return [s[:, c * LANE:(c + 1) * LANE] for c in range(s.shape[1] // LANE)]


def _row_max(chunks):
    m = functools.reduce(jnp.maximum, chunks)
    return jnp.broadcast_to(jnp.max(m, axis=-1, keepdims=True), m.shape)


def _softmax_weights(chunks, m):
    return jnp.concatenate([jnp.exp2((c - m) * EXP2_SCALE) for c in chunks], axis=1).astype(BF16)


def _flash_init(s, v_aug, m_scr, acc_scr, idx):
    chunks = _lane_chunks(s)
    m = _row_max(chunks)
    acc_scr[idx] = _dot(_softmax_weights(chunks, m), v_aug)
    m_scr[idx] = m


def _flash_step(s, v_aug, m_scr, acc_scr, idx):
    chunks = _lane_chunks(s)
    m_old = m_scr[idx]
    m_new = jnp.maximum(m_old, _row_max(chunks))
    alpha = jnp.exp2((m_old - m_new) * EXP2_SCALE)
    pv = _dot(_softmax_weights(chunks, m_new), v_aug)
    acc = acc_scr[idx]
    acc_scr[idx] = jnp.concatenate([alpha * a for a in _lane_chunks(acc)], axis=1) + pv
    m_scr[idx] = m_new


def _with_ones(v):
    return jnp.concatenate([v, jnp.ones(v.shape, v.dtype)], axis=1)


def _block_onehot(k0, tk, block, width):
    pos = k0 + lax.broadcasted_iota(jnp.int32, (tk, 1), 0)
    blk = lax.shift_right_logical(pos, block.bit_length() - 1) & (width - 1)
    return (blk == lax.broadcasted_iota(jnp.int32, (1, width), 1)).astype(BF16)


def _nsa_slc_kernel(q_ref, ks_ref, vs_ref, kw0_ref, kw1_ref, kw2_ref, vw0_ref, vw1_ref, vw2_ref,
                    sel_ref, gate_ref, ocmp_ref, o_ref, ka_scr, va_scr, qa_scr, m_scr, acc_scr, sa_scr, sb_scr):
    i = pl.program_id(1)
    tq = q_ref.shape[0]
    nsb = sel_ref.shape[2]
    s_len = ks_ref.shape[0]
    tk = min(SLC_KEY_TILE, s_len)
    n_kt = s_len // tk
    width = min(nsb, LANE)
    nhalf = max(nsb // LANE, 1)
    kt_per_half = (LANE * SLC_BLOCK) // tk
    t = i * tq + lax.broadcasted_iota(jnp.int32, (tq, 1), 0)
    kt_diag = (i * tq) // tk

    @pl.when(i == 0)
    def _():
        def build(n, carry):
            k0 = pl.multiple_of(n * tk, tk)
            ka_scr[pl.ds(k0, tk), :] = jnp.concatenate(
                [ks_ref[pl.ds(k0, tk), :], _block_onehot(k0, tk, SLC_BLOCK, width)], axis=1)
            va_scr[pl.ds(k0, tk), :] = _with_ones(vs_ref[pl.ds(k0, tk), :])
            return carry

        lax.fori_loop(0, n_kt, build, 0)
        ka_scr[pl.ds(s_len, tk), :] = jnp.concatenate(
            [jnp.zeros((tk, HEAD_DIM), BF16), jnp.ones((tk, width), BF16)], axis=1)
        va_scr[pl.ds(s_len, tk), :] = jnp.zeros((tk, 2 * HEAD_DIM), BF16)

    bias = ((1.0 - sel_ref[0].astype(F32)) * NEG_INF).astype(BF16)
    for half in range(nhalf):
        for r in range(NSA_REP):
            qa_scr[half, r * tq:(r + 1) * tq, :] = jnp.concatenate(
                [q_ref[:, r * HEAD_DIM:(r + 1) * HEAD_DIM], bias[:, half * width:(half + 1) * width]], axis=1)
    t_all = jnp.concatenate([t] * NSA_REP, axis=0)

    def scores(kt):
        half = jnp.where(kt < n_kt, kt // kt_per_half, 0)
        return _dot_nt(qa_scr[half], ka_scr[pl.ds(pl.multiple_of(kt * tk, tk), tk), :])

    def values(kt):
        return va_scr[pl.ds(pl.multiple_of(kt * tk, tk), tk), :]

    kpos = kt_diag * tk + lax.broadcasted_iota(jnp.int32, (1, tk), 1)
    _flash_init(jnp.where(kpos <= t_all, scores(kt_diag), NEG_INF), values(kt_diag), m_scr, acc_scr, Ellipsis)

    last = jnp.maximum(kt_diag - 1, 0)
    sa_scr[...] = scores(0)

    def tile_pair(j, carry):
        kt_b = jnp.where(2 * j + 1 < kt_diag, 2 * j + 1, n_kt)
        sb_scr[...] = scores(kt_b)
        _flash_step(sa_scr[...], values(2 * j), m_scr, acc_scr, Ellipsis)
        sa_scr[...] = scores(jnp.minimum(2 * j + 2, last))
        _flash_step(sb_scr[...], values(kt_b), m_scr, acc_scr, Ellipsis)
        return carry

    lax.fori_loop(0, (kt_diag + 1) // 2, tile_pair, 0)

    kwin = jnp.concatenate([kw0_ref[...], kw1_ref[...], kw2_ref[...]], axis=0)
    vwin = _with_ones(jnp.concatenate([vw0_ref[...], vw1_ref[...], vw2_ref[...]], axis=0))
    nw = kwin.shape[0]
    wpos = (i - 2) * tq + lax.broadcasted_iota(jnp.int32, (1, nw), 1)
    wvalid = (wpos <= t) & (wpos > t - WINDOW) & (wpos >= 0)
    s = _dot_nt(qa_scr[0, :, :HEAD_DIM], kwin)
    s = jnp.concatenate([jnp.where(wvalid, s[r * tq:(r + 1) * tq], NEG_INF) for r in range(NSA_REP)], axis=0)
    chunks = _lane_chunks(s)
    ow = _dot(_softmax_weights(chunks, _row_max(chunks)), vwin)
    o_win = ow[:, :HEAD_DIM] / ow[:, HEAD_DIM:]
    acc = acc_scr[...]
    o_slc = acc[:, :HEAD_DIM] / acc[:, HEAD_DIM:]
    gates = jax.nn.sigmoid(gate_ref[...])
    for r in range(NSA_REP):
        rows = slice(r * tq, (r + 1) * tq)
        o = (ocmp_ref[:, r * HEAD_DIM:(r + 1) * HEAD_DIM]
             + gates[:, 3 * r + 1:3 * r + 2] * o_slc[rows] + gates[:, 3 * r + 2:3 * r + 3] * o_win[rows])
        o_ref[:, r * HEAD_DIM:(r + 1) * HEAD_DIM] = o.astype(BF16)


def _nsa_slc(proj, sel, gates, ocmp):
    s = proj.shape[0]
    nsb = sel.shape[2]
    tq = min(Q_TILE, s)
    tk = min(SLC_KEY_TILE, s)
    gw = NSA_REP * HEAD_DIM
    assert WINDOW == 2 * tq and (nsb <= LANE or nsb % LANE == 0)

    def win_spec(col, d):
        return pl.BlockSpec((tq, HEAD_DIM), lambda g, i: (jnp.maximum(i - 2 + d, 0), col + g))

    return pl.pallas_call(
        _nsa_slc_kernel,
        grid=(N_KV_NSA, s // tq),
        in_specs=[
            pl.BlockSpec((tq, gw), lambda g, i: (i, g)),
            pl.BlockSpec((s, HEAD_DIM), lambda g, i: (0, COL_KS + g)),
            pl.BlockSpec((s, HEAD_DIM), lambda g, i: (0, COL_VS + g)),
            win_spec(COL_KW, 0), win_spec(COL_KW, 1), win_spec(COL_KW, 2),
            win_spec(COL_VW, 0), win_spec(COL_VW, 1), win_spec(COL_VW, 2),
            pl.BlockSpec((1, tq, nsb), lambda g, i: (g, i, 0)),
            pl.BlockSpec((tq, LANE), lambda g, i: (i, g)),
            pl.BlockSpec((tq, gw), lambda g, i: (i, g)),
        ],
        out_specs=pl.BlockSpec((tq, gw), lambda g, i: (i, g)),
        out_shape=jax.ShapeDtypeStruct((s, NSA_WIDTH), BF16),
        scratch_shapes=[pltpu.VMEM((s + tk, HEAD_DIM + min(nsb, LANE)), BF16),
                        pltpu.VMEM((s + tk, 2 * HEAD_DIM), BF16),
                        pltpu.VMEM((max(nsb // LANE, 1), NSA_REP * tq, HEAD_DIM + min(nsb, LANE)), BF16),
                        pltpu.VMEM((NSA_REP * tq, LANE), F32),
                        pltpu.VMEM((NSA_REP * tq, 2 * HEAD_DIM), F32),
                        pltpu.VMEM((NSA_REP * tq, tk), F32), pltpu.VMEM((NSA_REP * tq, tk), F32)],
        compiler_params=_cparams(("arbitrary", "arbitrary")),
        name="nsa_slc_win",
    )(proj, proj, proj, proj, proj, proj, proj, proj, proj, sel, gates, ocmp)


def _moba_kernel(q_ref, k_ref, v_ref, o_ref, ka_scr, va_scr, qa_scr, km_scr, m_scr, acc_scr, sa_scr, sb_scr):
    i = pl.program_id(1)
    tq = q_ref.shape[0]
    s_len = k_ref.shape[0]
    nbp = km_scr.shape[0]
    tk = min(MOBA_KEY_TILE, tq)
    bpt = tk // MOBA_BLOCK
    n_kt = s_len // tk

    @pl.when(i == 0)
    def _():
        km_scr[...] = jnp.zeros(km_scr.shape, F32)

        def build(n, carry):
            k0 = pl.multiple_of(n * tk, tk)
            kt = k_ref[pl.ds(k0, tk), :]
            ka_scr[pl.ds(k0, tk), :] = jnp.concatenate([kt, _block_onehot(k0, tk, MOBA_BLOCK, nbp)], axis=1)
            va_scr[pl.ds(k0, tk), :] = _with_ones(v_ref[pl.ds(k0, tk), :])
            for j in range(bpt):
                kb = kt[j * MOBA_BLOCK:(j + 1) * MOBA_BLOCK].astype(F32)
                km_scr[pl.ds(n * bpt + j, 1), :] = jnp.mean(kb, axis=0, keepdims=True)
            return carry

        lax.fori_loop(0, n_kt, build, 0)
        ka_scr[pl.ds(s_len, tk), :] = jnp.concatenate(
            [jnp.zeros((tk, HEAD_DIM), BF16), jnp.ones((tk, nbp), BF16)], axis=1)
        va_scr[pl.ds(s_len, tk), :] = jnp.zeros((tk, 2 * HEAD_DIM), BF16)

    q = q_ref[...]
    km_hi, km_lo = _split_bf16(km_scr[...])
    gate = _dot_nt(q, km_hi) + _dot_nt(q, km_lo)
    t = i * tq + lax.broadcasted_iota(jnp.int32, (tq, 1), 0)
    own = lax.shift_right_logical(t, MOBA_BLOCK.bit_length() - 1)
    nidx = lax.broadcasted_iota(jnp.int32, (1, nbp), 1)
    nf = nidx.astype(F32)
    work = jnp.where(nidx < own, gate, LOWEST)
    sel = nidx == own
    for _ in range(MOBA_TOPK):
        m = jnp.max(work, axis=-1, keepdims=True)
        first = jnp.min(jnp.where((work == m) & (m > 0.5 * LOWEST), nf, 1e9), axis=-1, keepdims=True)
        hit = nf == first
        sel = sel | hit
        work = jnp.where(hit, LOWEST, work)
    qa_scr[...] = jnp.concatenate([q, jnp.where(sel, 0.0, NEG_INF).astype(BF16)], axis=1)

    def scores(kt):
        return _dot_nt(qa_scr[...], ka_scr[pl.ds(pl.multiple_of(kt * tk, tk), tk), :])

    def values(kt):
        return va_scr[pl.ds(pl.multiple_of(kt * tk, tk), tk), :]

    d0 = pl.multiple_of(i * tq, tq)
    kpos = d0 + lax.broadcasted_iota(jnp.int32, (1, tq), 1)
    s_diag = _dot_nt(qa_scr[...], ka_scr[pl.ds(d0, tq), :])
    _flash_init(jnp.where(kpos <= t, s_diag, NEG_INF), va_scr[pl.ds(d0, tq), :], m_scr, acc_scr, Ellipsis)

    n_past = i * (tq // tk)
    last = jnp.maximum(n_past - 1, 0)
    sa_scr[...] = scores(0)

    def tile_pair(j, carry):
        kt_b = jnp.where(2 * j + 1 < n_past, 2 * j + 1, n_kt)
        sb_scr[...] = scores(kt_b)
        _flash_step(sa_scr[...], values(2 * j), m_scr, acc_scr, Ellipsis)
        sa_scr[...] = scores(jnp.minimum(2 * j + 2, last))
        _flash_step(sb_scr[...], values(kt_b), m_scr, acc_scr, Ellipsis)
        return carry

    lax.fori_loop(0, (n_past + 1) // 2, tile_pair, 0)
    acc = acc_scr[...]
    o_ref[...] = (acc[:, :HEAD_DIM] / acc[:, HEAD_DIM:]).astype(BF16)


def _moba(proj):
    s = proj.shape[0]
    tq = min(MOBA_Q_TILE, s)
    tk = min(MOBA_KEY_TILE, tq)
    nbp = max(LANE, s // MOBA_BLOCK)
    assert nbp & (nbp - 1) == 0 and tq % tk == 0
    return pl.pallas_call(
        _moba_kernel,
        grid=(N_HEADS_MOBA, s // tq),
        in_specs=[
            pl.BlockSpec((tq, HEAD_DIM), lambda h, i: (i, COL_QM + h)),
            pl.BlockSpec((s, HEAD_DIM), lambda h, i: (0, COL_KM + h)),
            pl.BlockSpec((s, HEAD_DIM), lambda h, i: (0, COL_VM + h)),
        ],
        out_specs=pl.BlockSpec((tq, HEAD_DIM), lambda h, i: (i, h)),
        out_shape=jax.ShapeDtypeStruct((s, MOBA_WIDTH), BF16),
        scratch_shapes=[pltpu.VMEM((s + tk, HEAD_DIM + nbp), BF16), pltpu.VMEM((s + tk, 2 * HEAD_DIM), BF16),
                        pltpu.VMEM((tq, HEAD_DIM + nbp), BF16),
                        pltpu.VMEM((nbp, HEAD_DIM), F32), pltpu.VMEM((tq, LANE), F32),
                        pltpu.VMEM((tq, 2 * HEAD_DIM), F32),
                        pltpu.VMEM((tq, tk), F32), pltpu.VMEM((tq, tk), F32)],
        compiler_params=_cparams(("arbitrary", "arbitrary")),
        name="moba",
    )(proj, proj, proj)


def _out_kernel(on_ref, om_ref, x_ref, wn_ref, wm_ref, gt_ref, g_ref, sc_ref, sh_ref, wrh_ref, wrl_ref,
                x1_ref, h_ref, lg_ref):
    mix = _dot(on_ref[...], wn_ref[...]) + _dot(om_ref[...], wm_ref[...])
    x1 = x_ref[...] + gt_ref[...] * mix
    x1_ref[...] = x1
    h = _rms_mod(x1, g_ref[...], sc_ref[...], sh_ref[...])
    h_hi, h_lo = _split_bf16(h)
    h_ref[...] = h
    wrh = wrh_ref[...]
    lg_ref[...] = _dot(h_hi, wrh) + _dot(h_lo, wrh) + _dot(h_hi, wrl_ref[...])


def _out_proj(o_nsa, o_moba, x, w_n, w_m, gt, g, sc, sh, wr_hi, wr_lo):
    s, d = x.shape
    tm = min(OUT_TM, s)
    row = lambda i: (i, 0)
    fix = lambda i: (0, 0)
    return pl.pallas_call(
        _out_kernel,
        grid=(s // tm,),
        in_specs=[
            pl.BlockSpec((tm, NSA_WIDTH), row),
            pl.BlockSpec((tm, MOBA_WIDTH), row),
            pl.BlockSpec((tm, d), row),
            pl.BlockSpec((NSA_WIDTH, d), fix),
            pl.BlockSpec((MOBA_WIDTH, d), fix),
            pl.BlockSpec((1, d), fix),
            pl.BlockSpec((1, d), fix),
            pl.BlockSpec((1, d), fix),
            pl.BlockSpec((1, d), fix),
            pl.BlockSpec((d, LANE), fix),
            pl.BlockSpec((d, LANE), fix),
        ],
        out_specs=[pl.BlockSpec((tm, d), row), pl.BlockSpec((tm, d), row), pl.BlockSpec((tm, LANE), row)],
        out_shape=[jax.ShapeDtypeStruct((s, d), F32), jax.ShapeDtypeStruct((s, d), F32),
                   jax.ShapeDtypeStruct((s, LANE), F32)],
        compiler_params=_cparams(("arbitrary",)),
        name="out_proj_norm_router",
    )(o_nsa, o_moba, x, w_n, w_m, gt, g, sc, sh, wr_hi, wr_lo)


def _row_copy(src_ref, dst_ref, src_row, dst_row, sem):
    return pltpu.make_async_copy(src_ref.at[pl.ds(src_row, 1)], dst_ref.at[pl.ds(dst_row, 1)], sem)


def _expert_kernel(be_ref, nu_ref, tok_ref, h_ref, wg_ref, wu_ref, wd_ref, y_ref,
                   x_buf, wg_s, wu_s, wd_s, sems):
    b = pl.program_id(0)
    n_used = nu_ref[0]

    def row_copy(target, r):
        blk = jnp.minimum(target, n_used - 1)
        buf = target % EXPERT_BUFFERS
        return _row_copy(h_ref, x_buf.at[buf], tok_ref[blk * MOE_ROWS + r], r, sems.at[buf])

    def gather(target, wait):
        def row(r, carry):
            if wait:
                row_copy(target, r).wait()
            else:
                row_copy(target, r).start()
            return carry

        lax.fori_loop(0, MOE_ROWS, row, 0, unroll=DMA_UNROLL)

    @pl.when((b == 0) & (n_used > 0))
    def _():
        gather(0, False)
        gather(1, False)

    prev = be_ref[jnp.maximum(b - 1, 0)]

    @pl.when((b == 0) | (be_ref[b] != prev))
    def _():
        wg_s[...] = wg_ref[0, 0].astype(BF16)
        wu_s[...] = wu_ref[0, 0].astype(BF16)
        wd_s[...] = wd_ref[0, 0].astype(BF16)

    @pl.when(b < n_used)
    def _():
        gather(b, True)
        for r in range(MOE_ROWS):
            row_copy(b + EXPERT_BUFFERS - 1, r).start()
        x = x_buf[b % EXPERT_BUFFERS].astype(BF16)
        a = _dot(x, wg_s[...])
        u = _dot(x, wu_s[...])
        hid = (a * jax.nn.sigmoid(a) * u).astype(BF16)
        y_ref[...] = _dot(hid, wd_s[...])

    @pl.when((b < n_used) & (b + EXPERT_BUFFERS - 1 >= n_used))
    def _():
        gather(b + EXPERT_BUFFERS - 1, True)

    @pl.when((b == 0) & (n_used == 1))
    def _():
        gather(1, True)

    @pl.when(b >= n_used)
    def _():
        y_ref[...] = jnp.zeros(y_ref.shape, F32)


def _experts(h, slot_tok, blk_exp, n_used, w_gate, w_up, w_down, layer):
    n_slots = slot_tok.shape[0]
    d = h.shape[1]
    n_blk = n_slots // MOE_ROWS
    grid_spec = pltpu.PrefetchScalarGridSpec(
        num_scalar_prefetch=3,
        grid=(n_blk,),
        in_specs=[
            pl.BlockSpec(memory_space=pl.ANY),
            pl.BlockSpec((1, 1, d, D_EXPERT), lambda b, be, nu, tok: (layer, be[b], 0, 0)),
            pl.BlockSpec((1, 1, d, D_EXPERT), lambda b, be, nu, tok: (layer, be[b], 0, 0)),
            pl.BlockSpec((1, 1, D_EXPERT, d), lambda b, be, nu, tok: (layer, be[b], 0, 0)),
        ],
        out_specs=pl.BlockSpec((MOE_ROWS, d), lambda b, be, nu, tok: (b, 0)),
        scratch_shapes=[pltpu.VMEM((EXPERT_BUFFERS, MOE_ROWS, d), F32),
                        pltpu.VMEM((d, D_EXPERT), BF16), pltpu.VMEM((d, D_EXPERT), BF16),
                        pltpu.VMEM((D_EXPERT, d), BF16), pltpu.SemaphoreType.DMA((EXPERT_BUFFERS,))],
    )
    return pl.pallas_call(
        _expert_kernel,
        grid_spec=grid_spec,
        out_shape=jax.ShapeDtypeStruct((n_slots, d), F32),
        compiler_params=_cparams(("arbitrary",)),
        name="moe_experts",
    )(blk_exp, n_used, slot_tok, h, w_gate, w_up, w_down)


def _combine_kernel(sa_ref, sb_ref, na_ref, nb_ref, x_ref, w_ref, gt_ref, gf_ref, y_ref, o_ref, buf, sems, *,
                    final):
    b = pl.program_id(0)
    tt = x_ref.shape[0]

    def gather(ia_ref, ib_ref, slot, wait):
        def row(r, carry):
            for k, idx_ref in enumerate((ia_ref, ib_ref)):
                cp = _row_copy(y_ref, buf.at[slot].at[k], idx_ref[0, 0, r], r, sems.at[slot].at[k])
                if wait:
                    cp.wait()
                else:
                    cp.start()
            return carry

        lax.fori_loop(0, tt, row, 0, unroll=DMA_UNROLL)

    @pl.when(b == 0)
    def _():
        gather(sa_ref, sb_ref, 0, False)

    @pl.when(b + 1 < pl.num_programs(0))
    def _():
        gather(na_ref, nb_ref, (b + 1) % 2, False)

    gather(sa_ref, sb_ref, b % 2, True)
    w = w_ref[...]
    x2 = x_ref[...] + gt_ref[...] * (w[:, 0:1] * buf[b % 2, 0] + w[:, 1:2] * buf[b % 2, 1])
    if final:
        x2 = x2 * lax.rsqrt(jnp.mean(x2 * x2, axis=-1, keepdims=True) + RMS_EPS) * gf_ref[...]
    o_ref[...] = x2


def _combine(x1, ys, slot_a, slot_b, pair_w, gt, g_final, final):
    s, d = x1.shape
    tt = min(GATHER_ROWS, s)
    nt = s // tt
    cur_spec = pl.BlockSpec((1, 1, tt), lambda b: (b, 0, 0), memory_space=pltpu.SMEM)
    nxt_spec = pl.BlockSpec((1, 1, tt), lambda b: (jnp.minimum(b + 1, nt - 1), 0, 0), memory_space=pltpu.SMEM)
    slot_a = slot_a.reshape(nt, 1, tt)
    slot_b = slot_b.reshape(nt, 1, tt)
    return pl.pallas_call(
        functools.partial(_combine_kernel, final=final),
        grid=(nt,),
        in_specs=[
            cur_spec, cur_spec, nxt_spec, nxt_spec,
            pl.BlockSpec((tt, d), lambda b: (b, 0)),
            pl.BlockSpec((tt, LANE), lambda b: (b, 0)),
            pl.BlockSpec((1, d), lambda b: (0, 0)),
            pl.BlockSpec((1, d), lambda b: (0, 0)),
            pl.BlockSpec(memory_space=pl.ANY),
        ],
        out_specs=pl.BlockSpec((tt, d), lambda b: (b, 0)),
        out_shape=jax.ShapeDtypeStruct((s, d), F32),
        scratch_shapes=[pltpu.VMEM((2, 2, tt, d), F32), pltpu.SemaphoreType.DMA((2, 2))],
        compiler_params=_cparams(("arbitrary",)),
        name="moe_combine",
    )(slot_a, slot_b, slot_a, slot_b, x1, jnp.pad(pair_w, ((0, 0), (0, LANE - TOP_K_EXPERTS))), gt, g_final, ys)


def _route(logits, b_router):
    n = logits.shape[0]
    scores = jax.nn.sigmoid(logits[:, :N_EXPERTS])
    biased = (scores + b_router.astype(F32)).reshape(n, N_GROUPS, EXPERTS_PER_GROUP)

    def top2(v):
        idx = lax.broadcasted_iota(jnp.int32, v.shape, v.ndim - 1)
        v1 = jnp.max(v, axis=-1, keepdims=True)
        i1 = jnp.min(jnp.where(v == v1, idx, v.shape[-1]), axis=-1, keepdims=True)
        rest = jnp.where(idx == i1, -jnp.inf, v)
        v2 = jnp.max(rest, axis=-1, keepdims=True)
        i2 = jnp.min(jnp.where(rest == v2, idx, v.shape[-1]), axis=-1, keepdims=True)
        return v1, v2, i1, i2

    g1, g2, _, _ = top2(biased)
    g_sum = (g1 + g2)[..., 0]
    g_idx = lax.broadcasted_iota(jnp.int32, g_sum.shape, 1)
    grp = jnp.min(jnp.where(g_sum == jnp.max(g_sum, axis=-1, keepdims=True), g_idx, N_GROUPS), axis=-1)
    grp_hot = grp[:, None] == jnp.arange(N_GROUPS, dtype=grp.dtype)[None, :]
    in_grp = jnp.sum(jnp.where(grp_hot[:, :, None], biased, 0.0), axis=1)
    _, _, l1, l2 = top2(in_grp)
    expert = (grp[:, None] * EXPERTS_PER_GROUP + jnp.concatenate([l1, l2], axis=1)).astype(jnp.int32)
    e_hot = expert[:, :, None] == jnp.arange(N_EXPERTS, dtype=jnp.int32)[None, None, :]
    wsel = jnp.sum(jnp.where(e_hot, scores[:, None, :], 0.0), axis=-1)
    wsel = wsel / jnp.sum(wsel, axis=-1, keepdims=True)

    n_pair = n * TOP_K_EXPERTS
    flat_e = expert.reshape(-1)
    flat_t = jnp.repeat(jnp.arange(n, dtype=jnp.int32), TOP_K_EXPERTS)
    hot = flat_e[:, None] == jnp.arange(N_EXPERTS, dtype=jnp.int32)[None, :]
    onehot = hot.astype(jnp.int32)
    chunk = min(LANE, n_pair)
    n_chunk = n_pair // chunk
    tri = (jnp.arange(chunk)[:, None] >= jnp.arange(chunk)[None, :]).astype(BF16)
    within = jnp.einsum('ij,cjk->cik', tri, hot.astype(BF16).reshape(n_chunk, chunk, N_EXPERTS),
                        preferred_element_type=F32)
    totals = within[:, -1, :]
    before = jnp.cumsum(totals, axis=0) - totals
    running = (within + before[:, None, :]).astype(jnp.int32).reshape(n_pair, N_EXPERTS)
    rank = jnp.sum(running * onehot, axis=1) - 1
    counts = running[-1]
    padded = (counts + MOE_ROWS - 1) // MOE_ROWS * MOE_ROWS
    pend = jnp.cumsum(padded)
    pstart = pend - padded
    slot = (jnp.sum(pstart[None, :] * onehot, axis=1) + rank).astype(jnp.int32)
    n_slots = n_pair + N_EXPERTS * MOE_ROWS
    slot_tok = jnp.zeros((n_slots,), jnp.int32).at[slot].set(flat_t)
    pair_slot = slot.reshape(n, TOP_K_EXPERTS)
    n_blk = n_slots // MOE_ROWS
    blk_start = jnp.arange(n_blk, dtype=jnp.int32) * MOE_ROWS
    blk_exp = jnp.minimum(jnp.sum((pend[None, :] <= blk_start[:, None]).astype(jnp.int32), axis=1), N_EXPERTS - 1)
    n_used = (pend[-1:] // MOE_ROWS).astype(jnp.int32)
    return slot_tok, wsel, pair_slot, blk_exp, n_used


def _rope_tables(pos):
    inv = 1.0 / (ROPE_THETA ** (jnp.arange(0, HEAD_DIM, 2, dtype=F32) / HEAD_DIM))
    ang = pos.astype(F32)[:, None] * inv[None, :]
    cos = jnp.cos(ang)
    sin = jnp.sin(ang)
    return jnp.concatenate([cos, cos], axis=-1), jnp.concatenate([-sin, sin], axis=-1)


def _layout_w_in(w_in):
    o_g = NSA_WIDTH + 6 * NSA_KV_WIDTH
    main = jnp.concatenate([w_in[:, :o_g], w_in[:, o_g + N_GATES:]], axis=1).astype(BF16)
    wg = w_in[:, o_g:o_g + N_GATES].reshape(-1, N_KV_NSA, 3 * NSA_REP)
    wg = jnp.pad(wg, ((0, 0), (0, 0), (0, LANE - 3 * NSA_REP))).reshape(-1, N_KV_NSA * LANE)
    return main, wg.astype(BF16)


def _overlap(ncp, nsb):
    cs = jnp.arange(ncp, dtype=jnp.int32)[:, None] * CMP_STRIDE
    ss = jnp.arange(nsb, dtype=jnp.int32)[None, :] * SLC_BLOCK
    return ((cs + CMP_LEN - 1 >= ss) & (cs < ss + SLC_BLOCK)).astype(BF16)


def kernel(x, c, w_ada, b_ada, g_attn, g_ffn, w_in, w_out, cmp_pe, cmp_w1, cmp_b1, cmp_w2,
           w_router, b_router, w_gate, w_up, w_down, g_final):
    b, s, d = x.shape
    assert b == 1 and d == D_MODEL and s % max(Q_TILE, SLC_KEY_TILE, MOBA_Q_TILE) == 0
    depth = w_ada.shape[0]
    ncp = s // CMP_STRIDE
    nsb = s // SLC_BLOCK

    cos, sin = _rope_tables(jnp.arange(s))
    cos_c, sin_c = _rope_tables(jnp.arange(ncp) * CMP_STRIDE + CMP_LEN - 1)
    overlap = _overlap(ncp, nsb)
    flags = jnp.asarray(ROPE_FLAGS, jnp.int32)
    wr = jnp.pad(w_router.astype(F32), ((0, 0), (0, LANE - N_EXPERTS)))
    wr_hi, wr_lo = _split_bf16(wr)

    mod = _ada_mod(c.astype(F32), w_ada, b_ada)
    xs = x.reshape(s, d)
    for l in range(depth):
        sh1, sc1, gt1, sh2, sc2, gt2 = [mod[l, :, k * d:(k + 1) * d] for k in range(6)]
        w_main, w_gates = _layout_w_in(w_in[l])
        proj, gates = _project(xs, g_attn[l].reshape(1, d), sc1, sh1, w_main, w_gates, cos, sin, flags)

        chunks = proj[:, COL_KC * LANE:COL_KS * LANE].reshape(s, 2 * N_KV_NSA, HEAD_DIM)
        chunks = chunks.transpose(1, 0, 2).reshape(2 * N_KV_NSA, ncp, CMP_STRIDE * HEAD_DIM)
        pe8 = jnp.pad(cmp_pe[l].reshape(2, 1, CMP_LEN * HEAD_DIM), ((0, 0), (0, 7), (0, 0)))
        cmp_kv = _compress(chunks, cmp_w1[l], cmp_b1[l].reshape(2, 1, CMP_HIDDEN), cmp_w2[l], pe8, cos_c, sin_c)

        ocmp, sel = _nsa_cmp(proj, cmp_kv, overlap, gates)
        o_nsa = _nsa_slc(proj, sel, gates, ocmp)
        o_moba = _moba(proj)

        w_o = w_out[l].astype(BF16)
        x1, h2, logits = _out_proj(o_nsa, o_moba, xs, w_o[:NSA_WIDTH], w_o[NSA_WIDTH:], gt1,
                                   g_ffn[l].reshape(1, d), sc2, sh2, wr_hi, wr_lo)

        slot_tok, pair_w, pair_slot, blk_exp, n_used = _route(logits, b_router)
        ys = _experts(h2, slot_tok, blk_exp, n_used, w_gate, w_up, w_down, l)
        xs = _combine(x1, ys, pair_slot[:, 0], pair_slot[:, 1], pair_w, gt2, g_final.reshape(1, d),
                      l == depth - 1)
    return xs.reshape(b, s, d)
```

```python
import functools

import jax
import jax.numpy as jnp
from jax import lax
from jax.experimental import pallas as pl
from jax.experimental.pallas import tpu as pltpu

F32 = jnp.float32
BF16 = jnp.bfloat16

D_MODEL = 2048
HEAD_DIM = 128
N_HEADS_NSA = 8
N_KV_NSA = 2
NSA_REP = N_HEADS_NSA // N_KV_NSA
N_HEADS_MOBA = 8
NSA_WIDTH = N_HEADS_NSA * HEAD_DIM
NSA_KV_WIDTH = N_KV_NSA * HEAD_DIM
MOBA_WIDTH = N_HEADS_MOBA * HEAD_DIM
N_GATES = 3 * N_HEADS_NSA
ROPE_THETA = 10000.0
CMP_LEN = 32
CMP_STRIDE = 16
CMP_HIDDEN = 256
SLC_BLOCK = 64
SLC_TOPN = 16
N_FORCED = 3
WINDOW = 512
MOBA_BLOCK = 256
MOBA_TOPK = 3
N_EXPERTS = 32
N_GROUPS = 4
EXPERTS_PER_GROUP = N_EXPERTS // N_GROUPS
TOP_K_EXPERTS = 2
D_EXPERT = 512
RMS_EPS = 1e-6
NEG_INF = -1e30
LOWEST = -3e38
PICKED = -2e38
SCALE = HEAD_DIM ** -0.5
EXP2_SCALE = SCALE * 1.4426950408889634

LANE = 128
Q_TILE = 256
MOBA_Q_TILE = 1024
MOBA_KEY_TILE = 512
SLC_KEY_TILE = 512
CMP_BAND = 256
MOE_ROWS = 256
GATHER_ROWS = 256
DMA_UNROLL = 8
EXPERT_BUFFERS = 3
PROJ_TM = 1024
PROJ_TN = 512
OUT_TM = 256
VMEM_LIMIT = 56 * 1024 * 1024

COL_QN = 0
COL_KC = 8
COL_VC = 10
COL_KS = 12
COL_VS = 14
COL_KW = 16
COL_VW = 18
COL_QM = 20
COL_KM = 28
COL_VM = 36
PROJ_COLS = 44
ROPE_FLAGS = tuple(
    1 if (c < 8 or c in (12, 13, 16, 17) or 20 <= c < 36) else 0 for c in range(PROJ_COLS))


def _cparams(sem):
    return pltpu.CompilerParams(dimension_semantics=sem, vmem_limit_bytes=VMEM_LIMIT)


def _dot(a, b):
    return jnp.dot(a, b, preferred_element_type=F32)


def _dot_nt(a, b):
    return lax.dot_general(a, b, (((1,), (1,)), ((), ())), preferred_element_type=F32)


def _split_bf16(v):
    hi = v.astype(BF16)
    lo = (v - hi.astype(F32)).astype(BF16)
    return hi, lo


def _mod_kernel(c_ref, w_ref, b_ref, o_ref):
    c = c_ref[...]
    cond = c * jax.nn.sigmoid(c)
    o_ref[0] = jnp.sum(w_ref[0] * cond, axis=0, keepdims=True) + b_ref[0]


def _ada_mod(c, w_ada, b_ada):
    depth, d, n = w_ada.shape
    tn = 1024
    return pl.pallas_call(
        _mod_kernel,
        grid=(depth, n // tn),
        in_specs=[
            pl.BlockSpec((d, 1), lambda l, j: (0, 0)),
            pl.BlockSpec((1, d, tn), lambda l, j: (l, 0, j)),
            pl.BlockSpec((1, 1, tn), lambda l, j: (l, 0, j)),
        ],
        out_specs=pl.BlockSpec((1, 1, tn), lambda l, j: (l, 0, j)),
        out_shape=jax.ShapeDtypeStruct((depth, 1, n), F32),
        compiler_params=_cparams(("arbitrary", "arbitrary")),
        name="ada_mod",
    )(c.reshape(d, 1), w_ada, b_ada.reshape(depth, 1, n))


def _rms_mod(x, g, sc, sh):
    y = x * lax.rsqrt(jnp.mean(x * x, axis=-1, keepdims=True) + RMS_EPS)
    return (y * g) * (1.0 + sc) + sh


def _proj_kernel(flags_ref, x_ref, g_ref, sc_ref, sh_ref, w_ref, wg_ref, cos_ref, sin_ref,
                 o_ref, gate_ref, h_scr):
    j = pl.program_id(1)

    @pl.when(j == 0)
    def _():
        hb = _rms_mod(x_ref[...], g_ref[...], sc_ref[...], sh_ref[...]).astype(BF16)
        h_scr[...] = hb
        gate_ref[...] = _dot(hb, wg_ref[...])

    acc = _dot(h_scr[...], w_ref[...])
    cos = cos_ref[...]
    sin = sin_ref[...]
    nch = PROJ_TN // LANE
    for c in range(nch):
        a = acc[:, c * LANE:(c + 1) * LANE]
        roped = a * cos + pltpu.roll(a, HEAD_DIM // 2, 1) * sin
        f = flags_ref[j * nch + c]
        o_ref[:, c * LANE:(c + 1) * LANE] = jnp.where(f > 0, roped, a).astype(BF16)


def _project(x, g, sc, sh, w_main, w_gate, cos, sin, flags):
    s, d = x.shape
    n = w_main.shape[1]
    tm = min(PROJ_TM, s)
    grid_spec = pltpu.PrefetchScalarGridSpec(
        num_scalar_prefetch=1,
        grid=(s // tm, n // PROJ_TN),
        in_specs=[
            pl.BlockSpec((tm, d), lambda i, j, f: (i, 0)),
            pl.BlockSpec((1, d), lambda i, j, f: (0, 0)),
            pl.BlockSpec((1, d), lambda i, j, f: (0, 0)),
            pl.BlockSpec((1, d), lambda i, j, f: (0, 0)),
            pl.BlockSpec((d, PROJ_TN), lambda i, j, f: (0, j)),
            pl.BlockSpec((d, 2 * LANE), lambda i, j, f: (0, 0)),
            pl.BlockSpec((tm, LANE), lambda i, j, f: (i, 0)),
            pl.BlockSpec((tm, LANE), lambda i, j, f: (i, 0)),
        ],
        out_specs=[
            pl.BlockSpec((tm, PROJ_TN), lambda i, j, f: (i, j)),
            pl.BlockSpec((tm, 2 * LANE), lambda i, j, f: (i, 0)),
        ],
        scratch_shapes=[pltpu.VMEM((tm, d), BF16)],
    )
    return pl.pallas_call(
        _proj_kernel,
        grid_spec=grid_spec,
        out_shape=[jax.ShapeDtypeStruct((s, n), BF16), jax.ShapeDtypeStruct((s, 2 * LANE), F32)],
        compiler_params=_cparams(("arbitrary", "arbitrary")),
        name="norm_proj_rope",
    )(flags, x, g, sc, sh, w_main, w_gate, cos, sin)


def _compress_kernel(ch_ref, w1_ref, b1_ref, w2_ref, pe_ref, cos_ref, sin_ref, o_ref):
    a = pl.program_id(0)
    ch = ch_ref[0]
    nc = ch.shape[0]
    half = CMP_STRIDE * HEAD_DIM
    w1 = w1_ref[0].astype(BF16)
    top = _dot(ch, w1[:half])
    bot = _dot(ch, w1[half:])
    bot = pltpu.roll(bot, nc - 1, 0)
    pe_term = _dot(pe_ref[0].astype(BF16), w1)[0:1]
    hid = jax.nn.gelu(top + bot + pe_term + b1_ref[0])
    out = _dot(hid.astype(BF16), w2_ref[0].astype(BF16))
    roped = out * cos_ref[...] + pltpu.roll(out, HEAD_DIM // 2, 1) * sin_ref[...]
    out = jnp.where(a < N_KV_NSA, roped, out)
    row = lax.broadcasted_iota(jnp.int32, out.shape, 0)
    o_ref[0] = jnp.where(row < nc - 1, out, 0.0).astype(BF16)


def _compress(chunks, w1, b1, w2, pe8, cos_c, sin_c):
    na, nc, kw = chunks.shape
    return pl.pallas_call(
        _compress_kernel,
        grid=(na,),
        in_specs=[
            pl.BlockSpec((1, nc, kw), lambda a: (a, 0, 0)),
            pl.BlockSpec((1, CMP_LEN * HEAD_DIM, CMP_HIDDEN), lambda a: (a // N_KV_NSA, 0, 0)),
            pl.BlockSpec((1, 1, CMP_HIDDEN), lambda a: (a // N_KV_NSA, 0, 0)),
            pl.BlockSpec((1, CMP_HIDDEN, HEAD_DIM), lambda a: (a // N_KV_NSA, 0, 0)),
            pl.BlockSpec((1, 8, CMP_LEN * HEAD_DIM), lambda a: (a // N_KV_NSA, 0, 0)),
            pl.BlockSpec((nc, HEAD_DIM), lambda a: (0, 0)),
            pl.BlockSpec((nc, HEAD_DIM), lambda a: (0, 0)),
        ],
        out_specs=pl.BlockSpec((1, nc, HEAD_DIM), lambda a: (a, 0, 0)),
        out_shape=jax.ShapeDtypeStruct((na, nc, HEAD_DIM), BF16),
        compiler_params=_cparams(("arbitrary",)),
        name="nsa_compress",
    )(chunks, w1, b1, w2, pe8, cos_c, sin_c)


def _nsa_cmp_band(q_ref, kc_ref, vc_ref, ovt_ref, gate_ref, ocmp_ref, sel_ref, psum_scr, imp_scr, cw, sw):
    i = pl.program_id(1)
    tq = q_ref.shape[0]
    nsb = sel_ref.shape[2]
    t = i * tq + lax.broadcasted_iota(jnp.int32, (tq, 1), 0)
    cend = lax.broadcasted_iota(jnp.int32, (1, cw), 1) * CMP_STRIDE + (CMP_LEN - 1)
    mask = cend <= t
    kc = kc_ref[0, :cw, :]
    vc = vc_ref[0, :cw, :]
    gates = jax.nn.sigmoid(gate_ref[...])
    for r in range(NSA_REP):
        q = q_ref[:, r * HEAD_DIM:(r + 1) * HEAD_DIM]
        s = jnp.where(mask, _dot_nt(q, kc) * SCALE, NEG_INF)
        m = jnp.max(s, axis=-1, keepdims=True)
        e = jnp.where(mask, jnp.exp(s - m), 0.0)
        l = jnp.sum(e, axis=-1, keepdims=True)
        p = e / jnp.where(l > 0.0, l, 1.0)
        psum_scr[:, :cw] = p if r == 0 else psum_scr[:, :cw] + p
        o = _dot(p.astype(BF16), vc)
        ocmp_ref[:, r * HEAD_DIM:(r + 1) * HEAD_DIM] = gates[:, 3 * r:3 * r + 1] * o

    p_hi, p_lo = _split_bf16(psum_scr[:, :cw])
    ovt = ovt_ref[:sw, :cw]
    imp_scr[:sw, :] = _dot_nt(ovt, p_hi) + _dot_nt(ovt, p_lo)

    jblk = lax.broadcasted_iota(jnp.int32, (sw, 1), 0)
    jf = jblk.astype(F32)
    lt = min(LANE, tq)
    picks = []
    for c in range(tq // lt):
        tb = lax.shift_right_logical(i * tq + c * lt + lax.broadcasted_iota(jnp.int32, (1, lt), 1),
                                     SLC_BLOCK.bit_length() - 1)
        valid = jblk <= tb
        forced = valid & ((jblk == 0) | (jblk == tb) | (jblk == tb - 1))
        work = jnp.where(valid & jnp.logical_not(forced), imp_scr[:sw, c * lt:(c + 1) * lt], LOWEST)
        for _ in range(min(SLC_TOPN, nsb) - N_FORCED):
            m = jnp.max(work, axis=0, keepdims=True)
            first = jnp.min(jnp.where(work == m, jf, 1e9), axis=0, keepdims=True)
            work = jnp.where(jf == first, PICKED, work)
        picks.append(jnp.where(forced | (work == PICKED), 1.0, 0.0))
    sel_ref[0, :, :sw] = jnp.concatenate(picks, axis=1).T.astype(BF16)
    if sw < nsb:
        sel_ref[0, :, sw:] = jnp.zeros((tq, nsb - sw), BF16)


def _nsa_cmp_kernel(q_ref, kc_ref, vc_ref, ovt_ref, gate_ref, ocmp_ref, sel_ref, psum_scr, imp_scr):
    i = pl.program_id(1)
    tq = q_ref.shape[0]
    ncp = kc_ref.shape[1]
    nsb = sel_ref.shape[2]
    band_w = min(CMP_BAND, ncp)
    n_band = ncp // band_w
    band = jnp.minimum(((i + 1) * (tq // CMP_STRIDE) + band_w - 1) // band_w, n_band) - 1
    for b in range(n_band):
        cw = band_w * (b + 1)
        sw = min(nsb, -(-(cw * CMP_STRIDE // SLC_BLOCK) // LANE) * LANE)

        @pl.when(band == b)
        def _():
            _nsa_cmp_band(q_ref, kc_ref, vc_ref, ovt_ref, gate_ref, ocmp_ref, sel_ref, psum_scr, imp_scr, cw, sw)


def _nsa_cmp(proj, cmp_kv, overlap_t, gates):
    s = proj.shape[0]
    ncp = cmp_kv.shape[1]
    nsb = overlap_t.shape[0]
    tq = min(Q_TILE, s)
    gw = NSA_REP * HEAD_DIM
    assert ncp % min(CMP_BAND, ncp) == 0
    return pl.pallas_call(
        _nsa_cmp_kernel,
        grid=(N_KV_NSA, s // tq),
        in_specs=[
            pl.BlockSpec((tq, gw), lambda g, i: (i, g)),
            pl.BlockSpec((1, ncp, HEAD_DIM), lambda g, i: (g, 0, 0)),
            pl.BlockSpec((1, ncp, HEAD_DIM), lambda g, i: (N_KV_NSA + g, 0, 0)),
            pl.BlockSpec((nsb, ncp), lambda g, i: (0, 0)),
            pl.BlockSpec((tq, LANE), lambda g, i: (i, g)),
        ],
        out_specs=[
            pl.BlockSpec((tq, gw), lambda g, i: (i, g)),
            pl.BlockSpec((1, tq, nsb), lambda g, i: (g, i, 0)),
        ],
        out_shape=[jax.ShapeDtypeStruct((s, NSA_WIDTH), F32),
                   jax.ShapeDtypeStruct((N_KV_NSA, s, nsb), BF16)],
        scratch_shapes=[pltpu.VMEM((tq, ncp), F32), pltpu.VMEM((nsb, tq), F32)],
        compiler_params=_cparams(("arbitrary", "arbitrary")),
        name="nsa_cmp_select",
    )(proj, cmp_kv, cmp_kv, overlap_t, gates)


def _lane_chunks(s):
    return [s[:, c * LANE:(c + 1) * LANE] for c in range(s.shape[1] // LANE)]


def _row_max(chunks):
    m = functools.reduce(jnp.maximum, chunks)
    return jnp.broadcast_to(jnp.max(m, axis=-1, keepdims=True), m.shape)


def _softmax_weights(chunks, m):
    return jnp.concatenate([jnp.exp2((c - m) * EXP2_SCALE) for c in chunks], axis=1).astype(BF16)


def _flash_init(s, v_aug, m_scr, acc_scr, idx):
    chunks = _lane_chunks(s)
    m = _row_max(chunks)
    acc_scr[idx] = _dot(_softmax_weights(chunks, m), v_aug)
    m_scr[idx] = m


def _flash_step(s, v_aug, m_scr, acc_scr, idx):
    chunks = _lane_chunks(s)
    m_old = m_scr[idx]
    m_new = jnp.maximum(m_old, _row_max(chunks))
    alpha = jnp.exp2((m_old - m_new) * EXP2_SCALE)
    pv = _dot(_softmax_weights(chunks, m_new), v_aug)
    acc = acc_scr[idx]
    acc_scr[idx] = jnp.concatenate([alpha * a for a in _lane_chunks(acc)], axis=1) + pv
    m_scr[idx] = m_new


def _with_ones(v):
    return jnp.concatenate([v, jnp.ones(v.shape, v.dtype)], axis=1)


def _block_onehot(k0, tk, block, width):
    pos = k0 + lax.broadcasted_iota(jnp.int32, (tk, 1), 0)
    blk = lax.shift_right_logical(pos, block.bit_length() - 1) & (width - 1)
    return (blk == lax.broadcasted_iota(jnp.int32, (1, width), 1)).astype(BF16)


def _nsa_slc_kernel(q_ref, ks_ref, vs_ref, kw0_ref, kw1_ref, kw2_ref, vw0_ref, vw1_ref, vw2_ref,
                    sel_ref, gate_ref, ocmp_ref, o_ref, ka_scr, va_scr, qa_scr, m_scr, acc_scr, sa_scr, sb_scr):
    i = pl.program_id(1)
    tq = q_ref.shape[0]
    nsb = sel_ref.shape[2]
    s_len = ks_ref.shape[0]
    tk = min(SLC_KEY_TILE, s_len)
    n_kt = s_len // tk
    width = min(nsb, LANE)
    nhalf = max(nsb // LANE, 1)
    kt_per_half = (LANE * SLC_BLOCK) // tk
    t = i * tq + lax.broadcasted_iota(jnp.int32, (tq, 1), 0)
    kt_diag = (i * tq) // tk

    @pl.when(i == 0)
    def _():
        def build(n, carry):
            k0 = pl.multiple_of(n * tk, tk)
            ka_scr[pl.ds(k0, tk), :] = jnp.concatenate(
                [ks_ref[pl.ds(k0, tk), :], _block_onehot(k0, tk, SLC_BLOCK, width)], axis=1)
            va_scr[pl.ds(k0, tk), :] = _with_ones(vs_ref[pl.ds(k0, tk), :])
            return carry

        lax.fori_loop(0, n_kt, build, 0)
        ka_scr[pl.ds(s_len, tk), :] = jnp.concatenate(
            [jnp.zeros((tk, HEAD_DIM), BF16), jnp.ones((tk, width), BF16)], axis=1)
        va_scr[pl.ds(s_len, tk), :] = jnp.zeros((tk, 2 * HEAD_DIM), BF16)

    bias = ((1.0 - sel_ref[0].astype(F32)) * NEG_INF).astype(BF16)
    for half in range(nhalf):
        for r in range(NSA_REP):
            qa_scr[half, r * tq:(r + 1) * tq, :] = jnp.concatenate(
                [q_ref[:, r * HEAD_DIM:(r + 1) * HEAD_DIM], bias[:, half * width:(half + 1) * width]], axis=1)
    t_all = jnp.concatenate([t] * NSA_REP, axis=0)

    def scores(kt):
        half = jnp.where(kt < n_kt, kt // kt_per_half, 0)
        return _dot_nt(qa_scr[half], ka_scr[pl.ds(pl.multiple_of(kt * tk, tk), tk), :])

    def values(kt):
        return va_scr[pl.ds(pl.multiple_of(kt * tk, tk), tk), :]

    kpos = kt_diag * tk + lax.broadcasted_iota(jnp.int32, (1, tk), 1)
    _flash_init(jnp.where(kpos <= t_all, scores(kt_diag), NEG_INF), values(kt_diag), m_scr, acc_scr, Ellipsis)

    last = jnp.maximum(kt_diag - 1, 0)
    sa_scr[...] = scores(0)

    def tile_pair(j, carry):
        kt_b = jnp.where(2 * j + 1 < kt_diag, 2 * j + 1, n_kt)
        sb_scr[...] = scores(kt_b)
        _flash_step(sa_scr[...], values(2 * j), m_scr, acc_scr, Ellipsis)
        sa_scr[...] = scores(jnp.minimum(2 * j + 2, last))
        _flash_step(sb_scr[...], values(kt_b), m_scr, acc_scr, Ellipsis)
        return carry

    lax.fori_loop(0, (kt_diag + 1) // 2, tile_pair, 0)

    kwin = jnp.concatenate([kw0_ref[...], kw1_ref[...], kw2_ref[...]], axis=0)
    vwin = _with_ones(jnp.concatenate([vw0_ref[...], vw1_ref[...], vw2_ref[...]], axis=0))
    nw = kwin.shape[0]
    wpos = (i - 2) * tq + lax.broadcasted_iota(jnp.int32, (1, nw), 1)
    wvalid = (wpos <= t) & (wpos > t - WINDOW) & (wpos >= 0)
    s = _dot_nt(qa_scr[0, :, :HEAD_DIM], kwin)
    s = jnp.concatenate([jnp.where(wvalid, s[r * tq:(r + 1) * tq], NEG_INF) for r in range(NSA_REP)], axis=0)
    chunks = _lane_chunks(s)
    ow = _dot(_softmax_weights(chunks, _row_max(chunks)), vwin)
    o_win = ow[:, :HEAD_DIM] / ow[:, HEAD_DIM:]
    acc = acc_scr[...]
    o_slc = acc[:, :HEAD_DIM] / acc[:, HEAD_DIM:]
    gates = jax.nn.sigmoid(gate_ref[...])
    for r in range(NSA_REP):
        rows = slice(r * tq, (r + 1) * tq)
        o = (ocmp_ref[:, r * HEAD_DIM:(r + 1) * HEAD_DIM]
             + gates[:, 3 * r + 1:3 * r + 2] * o_slc[rows] + gates[:, 3 * r + 2:3 * r + 3] * o_win[rows])
        o_ref[:, r * HEAD_DIM:(r + 1) * HEAD_DIM] = o.astype(BF16)


def _nsa_slc(proj, sel, gates, ocmp):
    s = proj.shape[0]
    nsb = sel.shape[2]
    tq = min(Q_TILE, s)
    tk = min(SLC_KEY_TILE, s)
    gw = NSA_REP * HEAD_DIM
    assert WINDOW == 2 * tq and (nsb <= LANE or nsb % LANE == 0)

    def win_spec(col, d):
        return pl.BlockSpec((tq, HEAD_DIM), lambda g, i: (jnp.maximum(i - 2 + d, 0), col + g))

    return pl.pallas_call(
        _nsa_slc_kernel,
        grid=(N_KV_NSA, s // tq),
        in_specs=[
            pl.BlockSpec((tq, gw), lambda g, i: (i, g)),
            pl.BlockSpec((s, HEAD_DIM), lambda g, i: (0, COL_KS + g)),
            pl.BlockSpec((s, HEAD_DIM), lambda g, i: (0, COL_VS + g)),
            win_spec(COL_KW, 0), win_spec(COL_KW, 1), win_spec(COL_KW, 2),
            win_spec(COL_VW, 0), win_spec(COL_VW, 1), win_spec(COL_VW, 2),
            pl.BlockSpec((1, tq, nsb), lambda g, i: (g, i, 0)),
            pl.BlockSpec((tq, LANE), lambda g, i: (i, g)),
            pl.BlockSpec((tq, gw), lambda g, i: (i, g)),
        ],
        out_specs=pl.BlockSpec((tq, gw), lambda g, i: (i, g)),
        out_shape=jax.ShapeDtypeStruct((s, NSA_WIDTH), BF16),
        scratch_shapes=[pltpu.VMEM((s + tk, HEAD_DIM + min(nsb, LANE)), BF16),
                        pltpu.VMEM((s + tk, 2 * HEAD_DIM), BF16),
                        pltpu.VMEM((max(nsb // LANE, 1), NSA_REP * tq, HEAD_DIM + min(nsb, LANE)), BF16),
                        pltpu.VMEM((NSA_REP * tq, LANE), F32),
                        pltpu.VMEM((NSA_REP * tq, 2 * HEAD_DIM), F32),
                        pltpu.VMEM((NSA_REP * tq, tk), F32), pltpu.VMEM((NSA_REP * tq, tk), F32)],
        compiler_params=_cparams(("arbitrary", "arbitrary")),
        name="nsa_slc_win",
    )(proj, proj, proj, proj, proj, proj, proj, proj, proj, sel, gates, ocmp)


def _moba_kernel(q_ref, k_ref, v_ref, o_ref, ka_scr, va_scr, qa_scr, km_scr, m_scr, acc_scr, sa_scr, sb_scr):
    i = pl.program_id(1)
    tq = q_ref.shape[0]
    s_len = k_ref.shape[0]
    nbp = km_scr.shape[0]
    tk = min(MOBA_KEY_TILE, tq)
    bpt = tk // MOBA_BLOCK
    n_kt = s_len // tk

    @pl.when(i == 0)
    def _():
        km_scr[...] = jnp.zeros(km_scr.shape, F32)

        def build(n, carry):
            k0 = pl.multiple_of(n * tk, tk)
            kt = k_ref[pl.ds(k0, tk), :]
            ka_scr[pl.ds(k0, tk), :] = jnp.concatenate([kt, _block_onehot(k0, tk, MOBA_BLOCK, nbp)], axis=1)
            va_scr[pl.ds(k0, tk), :] = _with_ones(v_ref[pl.ds(k0, tk), :])
            for j in range(bpt):
                kb = kt[j * MOBA_BLOCK:(j + 1) * MOBA_BLOCK].astype(F32)
                km_scr[pl.ds(n * bpt + j, 1), :] = jnp.mean(kb, axis=0, keepdims=True)
            return carry

        lax.fori_loop(0, n_kt, build, 0)
        ka_scr[pl.ds(s_len, tk), :] = jnp.concatenate(
            [jnp.zeros((tk, HEAD_DIM), BF16), jnp.ones((tk, nbp), BF16)], axis=1)
        va_scr[pl.ds(s_len, tk), :] = jnp.zeros((tk, 2 * HEAD_DIM), BF16)

    q = q_ref[...]
    km_hi, km_lo = _split_bf16(km_scr[...])
    gate = _dot_nt(q, km_hi) + _dot_nt(q, km_lo)
    t = i * tq + lax.broadcasted_iota(jnp.int32, (tq, 1), 0)
    own = lax.shift_right_logical(t, MOBA_BLOCK.bit_length() - 1)
    nidx = lax.broadcasted_iota(jnp.int32, (1, nbp), 1)
    nf = nidx.astype(F32)
    work = jnp.where(nidx < own, gate, LOWEST)
    sel = nidx == own
    for _ in range(MOBA_TOPK):
        m = jnp.max(work, axis=-1, keepdims=True)
        first = jnp.min(jnp.where((work == m) & (m > 0.5 * LOWEST), nf, 1e9), axis=-1, keepdims=True)
        hit = nf == first
        sel = sel | hit
        work = jnp.where(hit, LOWEST, work)
    qa_scr[...] = jnp.concatenate([q, jnp.where(sel, 0.0, NEG_INF).astype(BF16)], axis=1)

    def scores(kt):
        return _dot_nt(qa_scr[...], ka_scr[pl.ds(pl.multiple_of(kt * tk, tk), tk), :])

    def values(kt):
        return va_scr[pl.ds(pl.multiple_of(kt * tk, tk), tk), :]

    d0 = pl.multiple_of(i * tq, tq)
    for c in range(tq // tk):
        rows = slice(c * tk, (c + 1) * tk)
        nk = (c + 1) * tk
        kpos = d0 + lax.broadcasted_iota(jnp.int32, (1, nk), 1)
        s_diag = _dot_nt(qa_scr[rows, :], ka_scr[pl.ds(d0, nk), :])
        _flash_init(jnp.where(kpos <= t[rows], s_diag, NEG_INF), va_scr[pl.ds(d0, nk), :], m_scr, acc_scr, rows)

    n_past = i * (tq // tk)
    last = jnp.maximum(n_past - 1, 0)
    sa_scr[...] = scores(0)

    def tile_pair(j, carry):
        kt_b = jnp.where(2 * j + 1 < n_past, 2 * j + 1, n_kt)
        sb_scr[...] = scores(kt_b)
        _flash_step(sa_scr[...], values(2 * j), m_scr, acc_scr, Ellipsis)
        sa_scr[...] = scores(jnp.minimum(2 * j + 2, last))
        _flash_step(sb_scr[...], values(kt_b), m_scr, acc_scr, Ellipsis)
        return carry

    lax.fori_loop(0, (n_past + 1) // 2, tile_pair, 0)
    acc = acc_scr[...]
    o_ref[...] = (acc[:, :HEAD_DIM] / acc[:, HEAD_DIM:]).astype(BF16)


def _moba(proj):
    s = proj.shape[0]
    tq = min(MOBA_Q_TILE, s)
    tk = min(MOBA_KEY_TILE, tq)
    nbp = max(LANE, s // MOBA_BLOCK)
    assert nbp & (nbp - 1) == 0 and tq % tk == 0
    return pl.pallas_call(
        _moba_kernel,
        grid=(N_HEADS_MOBA, s // tq),
        in_specs=[
            pl.BlockSpec((tq, HEAD_DIM), lambda h, i: (i, COL_QM + h)),
            pl.BlockSpec((s, HEAD_DIM), lambda h, i: (0, COL_KM + h)),
            pl.BlockSpec((s, HEAD_DIM), lambda h, i: (0, COL_VM + h)),
        ],
        out_specs=pl.BlockSpec((tq, HEAD_DIM), lambda h, i: (i, h)),
        out_shape=jax.ShapeDtypeStruct((s, MOBA_WIDTH), BF16),
        scratch_shapes=[pltpu.VMEM((s + tk, HEAD_DIM + nbp), BF16), pltpu.VMEM((s + tk, 2 * HEAD_DIM), BF16),
                        pltpu.VMEM((tq, HEAD_DIM + nbp), BF16),
                        pltpu.VMEM((nbp, HEAD_DIM), F32), pltpu.VMEM((tq, LANE), F32),
                        pltpu.VMEM((tq, 2 * HEAD_DIM), F32),
                        pltpu.VMEM((tq, tk), F32), pltpu.VMEM((tq, tk), F32)],
        compiler_params=_cparams(("arbitrary", "arbitrary")),
        name="moba",
    )(proj, proj, proj)


def _out_kernel(on_ref, om_ref, x_ref, wn_ref, wm_ref, gt_ref, g_ref, sc_ref, sh_ref, wrh_ref, wrl_ref,
                x1_ref, h_ref, lg_ref):
    mix = _dot(on_ref[...], wn_ref[...]) + _dot(om_ref[...], wm_ref[...])
    x1 = x_ref[...] + gt_ref[...] * mix
    x1_ref[...] = x1
    h = _rms_mod(x1, g_ref[...], sc_ref[...], sh_ref[...])
    h_hi, h_lo = _split_bf16(h)
    h_ref[...] = h
    wrh = wrh_ref[...]
    lg_ref[...] = _dot(h_hi, wrh) + _dot(h_lo, wrh) + _dot(h_hi, wrl_ref[...])


def _out_proj(o_nsa, o_moba, x, w_n, w_m, gt, g, sc, sh, wr_hi, wr_lo):
    s, d = x.shape
    tm = min(OUT_TM, s)
    row = lambda i: (i, 0)
    fix = lambda i: (0, 0)
    return pl.pallas_call(
        _out_kernel,
        grid=(s // tm,),
        in_specs=[
            pl.BlockSpec((tm, NSA_WIDTH), row),
            pl.BlockSpec((tm, MOBA_WIDTH), row),
            pl.BlockSpec((tm, d), row),
            pl.BlockSpec((NSA_WIDTH, d), fix),
            pl.BlockSpec((MOBA_WIDTH, d), fix),
            pl.BlockSpec((1, d), fix),
            pl.BlockSpec((1, d), fix),
            pl.BlockSpec((1, d), fix),
            pl.BlockSpec((1, d), fix),
            pl.BlockSpec((d, LANE), fix),
            pl.BlockSpec((d, LANE), fix),
        ],
        out_specs=[pl.BlockSpec((tm, d), row), pl.BlockSpec((tm, d), row), pl.BlockSpec((tm, LANE), row)],
        out_shape=[jax.ShapeDtypeStruct((s, d), F32), jax.ShapeDtypeStruct((s, d), F32),
                   jax.ShapeDtypeStruct((s, LANE), F32)],
        compiler_params=_cparams(("arbitrary",)),
        name="out_proj_norm_router",
    )(o_nsa, o_moba, x, w_n, w_m, gt, g, sc, sh, wr_hi, wr_lo)


def _row_copy(src_ref, dst_ref, src_row, dst_row, sem):
    return pltpu.make_async_copy(src_ref.at[pl.ds(src_row, 1)], dst_ref.at[pl.ds(dst_row, 1)], sem)


def _expert_kernel(be_ref, nu_ref, tok_ref, h_ref, wg_ref, wu_ref, wd_ref, y_ref,
                   x_buf, wg_s, wu_s, wd_s, sems):
    b = pl.program_id(0)
    n_used = nu_ref[0]

    def row_copy(target, r):
        blk = jnp.minimum(target, n_used - 1)
        buf = target % EXPERT_BUFFERS
        return _row_copy(h_ref, x_buf.at[buf], tok_ref[blk * MOE_ROWS + r], r, sems.at[buf])

    def gather(target, wait):
        def row(r, carry):
            if wait:
                row_copy(target, r).wait()
            else:
                row_copy(target, r).start()
            return carry

        lax.fori_loop(0, MOE_ROWS, row, 0, unroll=DMA_UNROLL)

    @pl.when((b == 0) & (n_used > 0))
    def _():
        gather(0, False)
        gather(1, False)

    prev = be_ref[jnp.maximum(b - 1, 0)]

    @pl.when((b == 0) | (be_ref[b] != prev))
    def _():
        wg_s[...] = wg_ref[0, 0].astype(BF16)
        wu_s[...] = wu_ref[0, 0].astype(BF16)
        wd_s[...] = wd_ref[0, 0].astype(BF16)

    @pl.when(b < n_used)
    def _():
        gather(b, True)
        for r in range(MOE_ROWS):
            row_copy(b + EXPERT_BUFFERS - 1, r).start()
        x = x_buf[b % EXPERT_BUFFERS].astype(BF16)
        a = _dot(x, wg_s[...])
        u = _dot(x, wu_s[...])
        hid = (a * jax.nn.sigmoid(a) * u).astype(BF16)
        y_ref[...] = _dot(hid, wd_s[...])

    @pl.when((b < n_used) & (b + EXPERT_BUFFERS - 1 >= n_used))
    def _():
        gather(b + EXPERT_BUFFERS - 1, True)

    @pl.when((b == 0) & (n_used == 1))
    def _():
        gather(1, True)

    @pl.when(b >= n_used)
    def _():
        y_ref[...] = jnp.zeros(y_ref.shape, F32)


def _experts(h, slot_tok, blk_exp, n_used, w_gate, w_up, w_down, layer):
    n_slots = slot_tok.shape[0]
    d = h.shape[1]
    n_blk = n_slots // MOE_ROWS
    grid_spec = pltpu.PrefetchScalarGridSpec(
        num_scalar_prefetch=3,
        grid=(n_blk,),
        in_specs=[
            pl.BlockSpec(memory_space=pl.ANY),
            pl.BlockSpec((1, 1, d, D_EXPERT), lambda b, be, nu, tok: (layer, be[b], 0, 0)),
            pl.BlockSpec((1, 1, d, D_EXPERT), lambda b, be, nu, tok: (layer, be[b], 0, 0)),
            pl.BlockSpec((1, 1, D_EXPERT, d), lambda b, be, nu, tok: (layer, be[b], 0, 0)),
        ],
        out_specs=pl.BlockSpec((MOE_ROWS, d), lambda b, be, nu, tok: (b, 0)),
        scratch_shapes=[pltpu.VMEM((EXPERT_BUFFERS, MOE_ROWS, d), F32),
                        pltpu.VMEM((d, D_EXPERT), BF16), pltpu.VMEM((d, D_EXPERT), BF16),
                        pltpu.VMEM((D_EXPERT, d), BF16), pltpu.SemaphoreType.DMA((EXPERT_BUFFERS,))],
    )
    return pl.pallas_call(
        _expert_kernel,
        grid_spec=grid_spec,
        out_shape=jax.ShapeDtypeStruct((n_slots, d), F32),
        compiler_params=_cparams(("arbitrary",)),
        name="moe_experts",
    )(blk_exp, n_used, slot_tok, h, w_gate, w_up, w_down)


def _combine_kernel(sa_ref, sb_ref, na_ref, nb_ref, x_ref, w_ref, gt_ref, gf_ref, y_ref, o_ref, buf, sems, *,
                    final):
    b = pl.program_id(0)
    tt = x_ref.shape[0]

    def gather(ia_ref, ib_ref, slot, wait):
        def row(r, carry):
            for k, idx_ref in enumerate((ia_ref, ib_ref)):
                cp = _row_copy(y_ref, buf.at[slot].at[k], idx_ref[0, 0, r], r, sems.at[slot].at[k])
                if wait:
                    cp.wait()
                else:
                    cp.start()
            return carry

        lax.fori_loop(0, tt, row, 0, unroll=DMA_UNROLL)

    @pl.when(b == 0)
    def _():
        gather(sa_ref, sb_ref, 0, False)

    @pl.when(b + 1 < pl.num_programs(0))
    def _():
        gather(na_ref, nb_ref, (b + 1) % 2, False)

    gather(sa_ref, sb_ref, b % 2, True)
    w = w_ref[...]
    x2 = x_ref[...] + gt_ref[...] * (w[:, 0:1] * buf[b % 2, 0] + w[:, 1:2] * buf[b % 2, 1])
    if final:
        x2 = x2 * lax.rsqrt(jnp.mean(x2 * x2, axis=-1, keepdims=True) + RMS_EPS) * gf_ref[...]
    o_ref[...] = x2


def _combine(x1, ys, slot_a, slot_b, pair_w, gt, g_final, final):
    s, d = x1.shape
    tt = min(GATHER_ROWS, s)
    nt = s // tt
    cur_spec = pl.BlockSpec((1, 1, tt), lambda b: (b, 0, 0), memory_space=pltpu.SMEM)
    nxt_spec = pl.BlockSpec((1, 1, tt), lambda b: (jnp.minimum(b + 1, nt - 1), 0, 0), memory_space=pltpu.SMEM)
    slot_a = slot_a.reshape(nt, 1, tt)
    slot_b = slot_b.reshape(nt, 1, tt)
    return pl.pallas_call(
        functools.partial(_combine_kernel, final=final),
        grid=(nt,),
        in_specs=[
            cur_spec, cur_spec, nxt_spec, nxt_spec,
            pl.BlockSpec((tt, d), lambda b: (b, 0)),
            pl.BlockSpec((tt, LANE), lambda b: (b, 0)),
            pl.BlockSpec((1, d), lambda b: (0, 0)),
            pl.BlockSpec((1, d), lambda b: (0, 0)),
            pl.BlockSpec(memory_space=pl.ANY),
        ],
        out_specs=pl.BlockSpec((tt, d), lambda b: (b, 0)),
        out_shape=jax.ShapeDtypeStruct((s, d), F32),
        scratch_shapes=[pltpu.VMEM((2, 2, tt, d), F32), pltpu.SemaphoreType.DMA((2, 2))],
        compiler_params=_cparams(("arbitrary",)),
        name="moe_combine",
    )(slot_a, slot_b, slot_a, slot_b, x1, jnp.pad(pair_w, ((0, 0), (0, LANE - TOP_K_EXPERTS))), gt, g_final, ys)


def _route(logits, b_router):
    n = logits.shape[0]
    scores = jax.nn.sigmoid(logits[:, :N_EXPERTS])
    biased = (scores + b_router.astype(F32)).reshape(n, N_GROUPS, EXPERTS_PER_GROUP)

    def top2(v):
        idx = lax.broadcasted_iota(jnp.int32, v.shape, v.ndim - 1)
        v1 = jnp.max(v, axis=-1, keepdims=True)
        i1 = jnp.min(jnp.where(v == v1, idx, v.shape[-1]), axis=-1, keepdims=True)
        rest = jnp.where(idx == i1, -jnp.inf, v)
        v2 = jnp.max(rest, axis=-1, keepdims=True)
        i2 = jnp.min(jnp.where(rest == v2, idx, v.shape[-1]), axis=-1, keepdims=True)
        return v1, v2, i1, i2

    g1, g2, _, _ = top2(biased)
    g_sum = (g1 + g2)[..., 0]
    g_idx = lax.broadcasted_iota(jnp.int32, g_sum.shape, 1)
    grp = jnp.min(jnp.where(g_sum == jnp.max(g_sum, axis=-1, keepdims=True), g_idx, N_GROUPS), axis=-1)
    grp_hot = grp[:, None] == jnp.arange(N_GROUPS, dtype=grp.dtype)[None, :]
    in_grp = jnp.sum(jnp.where(grp_hot[:, :, None], biased, 0.0), axis=1)
    _, _, l1, l2 = top2(in_grp)
    expert = (grp[:, None] * EXPERTS_PER_GROUP + jnp.concatenate([l1, l2], axis=1)).astype(jnp.int32)
    e_hot = expert[:, :, None] == jnp.arange(N_EXPERTS, dtype=jnp.int32)[None, None, :]
    wsel = jnp.sum(jnp.where(e_hot, scores[:, None, :], 0.0), axis=-1)
    wsel = wsel / jnp.sum(wsel, axis=-1, keepdims=True)

    n_pair = n * TOP_K_EXPERTS
    flat_e = expert.reshape(-1)
    flat_t = jnp.repeat(jnp.arange(n, dtype=jnp.int32), TOP_K_EXPERTS)
    hot = flat_e[:, None] == jnp.arange(N_EXPERTS, dtype=jnp.int32)[None, :]
    onehot = hot.astype(jnp.int32)
    chunk = min(LANE, n_pair)
    n_chunk = n_pair // chunk
    tri = (jnp.arange(chunk)[:, None] >= jnp.arange(chunk)[None, :]).astype(BF16)
    within = jnp.einsum('ij,cjk->cik', tri, hot.astype(BF16).reshape(n_chunk, chunk, N_EXPERTS),
                        preferred_element_type=F32)
    totals = within[:, -1, :]
    before = jnp.cumsum(totals, axis=0) - totals
    running = (within + before[:, None, :]).astype(jnp.int32).reshape(n_pair, N_EXPERTS)
    rank = jnp.sum(running * onehot, axis=1) - 1
    counts = running[-1]
    padded = (counts + MOE_ROWS - 1) // MOE_ROWS * MOE_ROWS
    pend = jnp.cumsum(padded)
    pstart = pend - padded
    slot = (jnp.sum(pstart[None, :] * onehot, axis=1) + rank).astype(jnp.int32)
    n_slots = n_pair + N_EXPERTS * MOE_ROWS
    slot_tok = jnp.zeros((n_slots,), jnp.int32).at[slot].set(flat_t)
    pair_slot = slot.reshape(n, TOP_K_EXPERTS)
    n_blk = n_slots // MOE_ROWS
    blk_start = jnp.arange(n_blk, dtype=jnp.int32) * MOE_ROWS
    blk_exp = jnp.minimum(jnp.sum((pend[None, :] <= blk_start[:, None]).astype(jnp.int32), axis=1), N_EXPERTS - 1)
    n_used = (pend[-1:] // MOE_ROWS).astype(jnp.int32)
    return slot_tok, wsel, pair_slot, blk_exp, n_used


def _rope_tables(pos):
    inv = 1.0 / (ROPE_THETA ** (jnp.arange(0, HEAD_DIM, 2, dtype=F32) / HEAD_DIM))
    ang = pos.astype(F32)[:, None] * inv[None, :]
    cos = jnp.cos(ang)
    sin = jnp.sin(ang)
    return jnp.concatenate([cos, cos], axis=-1), jnp.concatenate([-sin, sin], axis=-1)


def _layout_w_in(w_in):
    o_g = NSA_WIDTH + 6 * NSA_KV_WIDTH
    main = jnp.concatenate([w_in[:, :o_g], w_in[:, o_g + N_GATES:]], axis=1).astype(BF16)
    wg = w_in[:, o_g:o_g + N_GATES].reshape(-1, N_KV_NSA, 3 * NSA_REP)
    wg = jnp.pad(wg, ((0, 0), (0, 0), (0, LANE - 3 * NSA_REP))).reshape(-1, N_KV_NSA * LANE)
    return main, wg.astype(BF16)


def _overlap_t(ncp, nsb):
    cs = jnp.arange(ncp, dtype=jnp.int32)[None, :] * CMP_STRIDE
    ss = jnp.arange(nsb, dtype=jnp.int32)[:, None] * SLC_BLOCK
    return ((cs + CMP_LEN - 1 >= ss) & (cs < ss + SLC_BLOCK)).astype(BF16)


def kernel(x, c, w_ada, b_ada, g_attn, g_ffn, w_in, w_out, cmp_pe, cmp_w1, cmp_b1, cmp_w2,
           w_router, b_router, w_gate, w_up, w_down, g_final):
    b, s, d = x.shape
    assert b == 1 and d == D_MODEL and s % max(Q_TILE, SLC_KEY_TILE, MOBA_Q_TILE) == 0
    depth = w_ada.shape[0]
    ncp = s // CMP_STRIDE
    nsb = s // SLC_BLOCK

    cos, sin = _rope_tables(jnp.arange(s))
    cos_c, sin_c = _rope_tables(jnp.arange(ncp) * CMP_STRIDE + CMP_LEN - 1)
    overlap_t = _overlap_t(ncp, nsb)
    flags = jnp.asarray(ROPE_FLAGS, jnp.int32)
    wr = jnp.pad(w_router.astype(F32), ((0, 0), (0, LANE - N_EXPERTS)))
    wr_hi, wr_lo = _split_bf16(wr)

    mod = _ada_mod(c.astype(F32), w_ada, b_ada)
    xs = x.reshape(s, d)
    for l in range(depth):
        sh1, sc1, gt1, sh2, sc2, gt2 = [mod[l, :, k * d:(k + 1) * d] for k in range(6)]
        w_main, w_gates = _layout_w_in(w_in[l])
        proj, gates = _project(xs, g_attn[l].reshape(1, d), sc1, sh1, w_main, w_gates, cos, sin, flags)

        chunks = proj[:, COL_KC * LANE:COL_KS * LANE].reshape(s, 2 * N_KV_NSA, HEAD_DIM)
        chunks = chunks.transpose(1, 0, 2).reshape(2 * N_KV_NSA, ncp, CMP_STRIDE * HEAD_DIM)
        pe8 = jnp.pad(cmp_pe[l].reshape(2, 1, CMP_LEN * HEAD_DIM), ((0, 0), (0, 7), (0, 0)))
        cmp_kv = _compress(chunks, cmp_w1[l], cmp_b1[l].reshape(2, 1, CMP_HIDDEN), cmp_w2[l], pe8, cos_c, sin_c)

        ocmp, sel = _nsa_cmp(proj, cmp_kv, overlap_t, gates)
        o_nsa = _nsa_slc(proj, sel, gates, ocmp)
        o_moba = _moba(proj)

        w_o = w_out[l].astype(BF16)
        x1, h2, logits = _out_proj(o_nsa, o_moba, xs, w_o[:NSA_WIDTH], w_o[NSA_WIDTH:], gt1,
                                   g_ffn[l].reshape(1, d), sc2, sh2, wr_hi, wr_lo)

        slot_tok, pair_w, pair_slot, blk_exp, n_used = _route(logits, b_router)
        ys = _experts(h2, slot_tok, blk_exp, n_used, w_gate, w_up, w_down, l)
        xs = _combine(x1, ys, pair_slot[:, 0], pair_slot[:, 1], pair_w, gt2, g_final.reshape(1, d),
                      l == depth - 1)
    return xs.reshape(b, s, d)
```

```python
import functools

import jax
import jax.numpy as jnp
from jax import lax
from jax.experimental import pallas as pl
from jax.experimental.pallas import tpu as pltpu

F32 = jnp.float32
BF16 = jnp.bfloat16

D_MODEL = 2048
HEAD_DIM = 128
N_HEADS_NSA = 8
N_KV_NSA = 2
NSA_REP = N_HEADS_NSA // N_KV_NSA
N_HEADS_MOBA = 8
NSA_WIDTH = N_HEADS_NSA * HEAD_DIM
NSA_KV_WIDTH = N_KV_NSA * HEAD_DIM
MOBA_WIDTH = N_HEADS_MOBA * HEAD_DIM
N_GATES = 3 * N_HEADS_NSA
ROPE_THETA = 10000.0
CMP_LEN = 32
CMP_STRIDE = 16
CMP_HIDDEN = 256
SLC_BLOCK = 64
SLC_TOPN = 16
N_FORCED = 3
WINDOW = 512
MOBA_BLOCK = 256
MOBA_TOPK = 3
N_EXPERTS = 32
N_GROUPS = 4
EXPERTS_PER_GROUP = N_EXPERTS // N_GROUPS
TOP_K_EXPERTS = 2
D_EXPERT = 512
RMS_EPS = 1e-6
NEG_INF = -1e30
LOWEST = -3e38
PICKED = -2e38
SCALE = HEAD_DIM ** -0.5
EXP2_SCALE = SCALE * 1.4426950408889634

LANE = 128
Q_TILE = 256
MOBA_Q_TILE = 1024
MOBA_KEY_TILE = 512
SLC_KEY_TILE = 512
CMP_BAND = 256
MOE_ROWS = 256
GATHER_ROWS = 256
DMA_UNROLL = 8
EXPERT_BUFFERS = 3
PROJ_TM = 1024
PROJ_TN = 512
OUT_TM = 256
VMEM_LIMIT = 56 * 1024 * 1024

COL_QN = 0
COL_KC = 8
COL_VC = 10
COL_KS = 12
COL_VS = 14
COL_KW = 16
COL_VW = 18
COL_QM = 20
COL_KM = 28
COL_VM = 36
PROJ_COLS = 44
ROPE_FLAGS = tuple(
    1 if (c < 8 or c in (12, 13, 16, 17) or 20 <= c < 36) else 0 for c in range(PROJ_COLS))


def _cparams(sem):
    return pltpu.CompilerParams(dimension_semantics=sem, vmem_limit_bytes=VMEM_LIMIT)


def _dot(a, b):
    return jnp.dot(a, b, preferred_element_type=F32)


def _dot_nt(a, b):
    return lax.dot_general(a, b, (((1,), (1,)), ((), ())), preferred_element_type=F32)


def _split_bf16(v):
    hi = v.astype(BF16)
    lo = (v - hi.astype(F32)).astype(BF16)
    return hi, lo


def _mod_kernel(c_ref, w_ref, b_ref, o_ref):
    c = c_ref[...]
    cond = c * jax.nn.sigmoid(c)
    o_ref[0] = jnp.sum(w_ref[0] * cond, axis=0, keepdims=True) + b_ref[0]


def _ada_mod(c, w_ada, b_ada):
    depth, d, n = w_ada.shape
    tn = 1024
    return pl.pallas_call(
        _mod_kernel,
        grid=(depth, n // tn),
        in_specs=[
            pl.BlockSpec((d, 1), lambda l, j: (0, 0)),
            pl.BlockSpec((1, d, tn), lambda l, j: (l, 0, j)),
            pl.BlockSpec((1, 1, tn), lambda l, j: (l, 0, j)),
        ],
        out_specs=pl.BlockSpec((1, 1, tn), lambda l, j: (l, 0, j)),
        out_shape=jax.ShapeDtypeStruct((depth, 1, n), F32),
        compiler_params=_cparams(("arbitrary", "arbitrary")),
        name="ada_mod",
    )(c.reshape(d, 1), w_ada, b_ada.reshape(depth, 1, n))


def _rms_mod(x, g, sc, sh):
    y = x * lax.rsqrt(jnp.mean(x * x, axis=-1, keepdims=True) + RMS_EPS)
    return (y * g) * (1.0 + sc) + sh


def _proj_kernel(flags_ref, x_ref, g_ref, sc_ref, sh_ref, w_ref, wg_ref, cos_ref, sin_ref,
                 o_ref, gate_ref, h_scr):
    j = pl.program_id(1)

    @pl.when(j == 0)
    def _():
        hb = _rms_mod(x_ref[...], g_ref[...], sc_ref[...], sh_ref[...]).astype(BF16)
        h_scr[...] = hb
        gate_ref[...] = _dot(hb, wg_ref[...])

    acc = _dot(h_scr[...], w_ref[...])
    cos = cos_ref[...]
    sin = sin_ref[...]
    nch = PROJ_TN // LANE
    for c in range(nch):
        a = acc[:, c * LANE:(c + 1) * LANE]
        roped = a * cos + pltpu.roll(a, HEAD_DIM // 2, 1) * sin
        f = flags_ref[j * nch + c]
        o_ref[:, c * LANE:(c + 1) * LANE] = jnp.where(f > 0, roped, a).astype(BF16)


def _project(x, g, sc, sh, w_main, w_gate, cos, sin, flags):
    s, d = x.shape
    n = w_main.shape[1]
    tm = min(PROJ_TM, s)
    grid_spec = pltpu.PrefetchScalarGridSpec(
        num_scalar_prefetch=1,
        grid=(s // tm, n // PROJ_TN),
        in_specs=[
            pl.BlockSpec((tm, d), lambda i, j, f: (i, 0)),
            pl.BlockSpec((1, d), lambda i, j, f: (0, 0)),
            pl.BlockSpec((1, d), lambda i, j, f: (0, 0)),
            pl.BlockSpec((1, d), lambda i, j, f: (0, 0)),
            pl.BlockSpec((d, PROJ_TN), lambda i, j, f: (0, j)),
            pl.BlockSpec((d, 2 * LANE), lambda i, j, f: (0, 0)),
            pl.BlockSpec((tm, LANE), lambda i, j, f: (i, 0)),
            pl.BlockSpec((tm, LANE), lambda i, j, f: (i, 0)),
        ],
        out_specs=[
            pl.BlockSpec((tm, PROJ_TN), lambda i, j, f: (i, j)),
            pl.BlockSpec((tm, 2 * LANE), lambda i, j, f: (i, 0)),
        ],
        scratch_shapes=[pltpu.VMEM((tm, d), BF16)],
    )
    return pl.pallas_call(
        _proj_kernel,
        grid_spec=grid_spec,
        out_shape=[jax.ShapeDtypeStruct((s, n), BF16), jax.ShapeDtypeStruct((s, 2 * LANE), F32)],
        compiler_params=_cparams(("arbitrary", "arbitrary")),
        name="norm_proj_rope",
    )(flags, x, g, sc, sh, w_main, w_gate, cos, sin)


def _compress_kernel(ch_ref, w1_ref, b1_ref, w2_ref, pe_ref, cos_ref, sin_ref, o_ref):
    a = pl.program_id(0)
    ch = ch_ref[0]
    nc = ch.shape[0]
    half = CMP_STRIDE * HEAD_DIM
    w1 = w1_ref[0].astype(BF16)
    top = _dot(ch, w1[:half])
    bot = _dot(ch, w1[half:])
    bot = pltpu.roll(bot, nc - 1, 0)
    pe_term = _dot(pe_ref[0].astype(BF16), w1)[0:1]
    hid = jax.nn.gelu(top + bot + pe_term + b1_ref[0])
    out = _dot(hid.astype(BF16), w2_ref[0].astype(BF16))
    roped = out * cos_ref[...] + pltpu.roll(out, HEAD_DIM // 2, 1) * sin_ref[...]
    out = jnp.where(a < N_KV_NSA, roped, out)
    row = lax.broadcasted_iota(jnp.int32, out.shape, 0)
    o_ref[0] = jnp.where(row < nc - 1, out, 0.0).astype(BF16)


def _compress(chunks, w1, b1, w2, pe8, cos_c, sin_c):
    na, nc, kw = chunks.shape
    return pl.pallas_call(
        _compress_kernel,
        grid=(na,),
        in_specs=[
            pl.BlockSpec((1, nc, kw), lambda a: (a, 0, 0)),
            pl.BlockSpec((1, CMP_LEN * HEAD_DIM, CMP_HIDDEN), lambda a: (a // N_KV_NSA, 0, 0)),
            pl.BlockSpec((1, 1, CMP_HIDDEN), lambda a: (a // N_KV_NSA, 0, 0)),
            pl.BlockSpec((1, CMP_HIDDEN, HEAD_DIM), lambda a: (a // N_KV_NSA, 0, 0)),
            pl.BlockSpec((1, 8, CMP_LEN * HEAD_DIM), lambda a: (a // N_KV_NSA, 0, 0)),
            pl.BlockSpec((nc, HEAD_DIM), lambda a: (0, 0)),
            pl.BlockSpec((nc, HEAD_DIM), lambda a: (0, 0)),
        ],
        out_specs=pl.BlockSpec((1, nc, HEAD_DIM), lambda a: (a, 0, 0)),
        out_shape=jax.ShapeDtypeStruct((na, nc, HEAD_DIM), BF16),
        compiler_params=_cparams(("arbitrary",)),
        name="nsa_compress",
    )(chunks, w1, b1, w2, pe8, cos_c, sin_c)


def _nsa_cmp_band(q_ref, kc_ref, vc_ref, ovt_ref, gate_ref, ocmp_ref, sel_ref, psum_scr, imp_scr, cw, sw):
    i = pl.program_id(1)
    tq = q_ref.shape[0]
    nsb = sel_ref.shape[2]
    t = i * tq + lax.broadcasted_iota(jnp.int32, (tq, 1), 0)
    cend = lax.broadcasted_iota(jnp.int32, (1, cw), 1) * CMP_STRIDE + (CMP_LEN - 1)
    mask = cend <= t
    kc = kc_ref[0, :cw, :]
    vc = vc_ref[0, :cw, :]
    gates = jax.nn.sigmoid(gate_ref[...])
    row_ok = t >= CMP_LEN - 1
    for r in range(NSA_REP):
        q = q_ref[:, r * HEAD_DIM:(r + 1) * HEAD_DIM]
        chunks = _lane_chunks(jnp.where(mask, _dot_nt(q, kc), NEG_INF))
        m = _row_max(chunks)
        e = [jnp.exp2((c - m) * EXP2_SCALE) for c in chunks]
        l = jnp.sum(functools.reduce(jnp.add, e), axis=-1, keepdims=True)
        inv = jnp.broadcast_to(jnp.where(row_ok, 1.0 / l, 0.0), m.shape)
        p = jnp.concatenate([c * inv for c in e], axis=1)
        psum_scr[:, :cw] = p if r == 0 else psum_scr[:, :cw] + p
        o = _dot(p.astype(BF16), vc)
        ocmp_ref[:, r * HEAD_DIM:(r + 1) * HEAD_DIM] = gates[:, 3 * r:3 * r + 1] * o

    p_hi, p_lo = _split_bf16(psum_scr[:, :cw])
    ovt = ovt_ref[:sw, :cw]
    imp_scr[:sw, :] = _dot_nt(ovt, p_hi) + _dot_nt(ovt, p_lo)

    jblk = lax.broadcasted_iota(jnp.int32, (sw, 1), 0)
    jf = jblk.astype(F32)
    lt = min(LANE, tq)
    picks = []
    for c in range(tq // lt):
        tb = lax.shift_right_logical(i * tq + c * lt + lax.broadcasted_iota(jnp.int32, (1, lt), 1),
                                     SLC_BLOCK.bit_length() - 1)
        valid = jblk <= tb
        forced = valid & ((jblk == 0) | (jblk == tb) | (jblk == tb - 1))
        work = jnp.where(valid & jnp.logical_not(forced), imp_scr[:sw, c * lt:(c + 1) * lt], LOWEST)
        for _ in range(min(SLC_TOPN, nsb) - N_FORCED):
            m = jnp.max(work, axis=0, keepdims=True)
            first = jnp.min(jnp.where(work == m, jf, 1e9), axis=0, keepdims=True)
            work = jnp.where(jf == first, PICKED, work)
        picks.append(jnp.where(forced | (work == PICKED), 1.0, 0.0))
    sel_ref[0, :, :sw] = jnp.concatenate(picks, axis=1).T.astype(BF16)
    if sw < nsb:
        sel_ref[0, :, sw:] = jnp.zeros((tq, nsb - sw), BF16)


def _nsa_cmp_kernel(q_ref, kc_ref, vc_ref, ovt_ref, gate_ref, ocmp_ref, sel_ref, psum_scr, imp_scr):
    i = pl.program_id(1)
    tq = q_ref.shape[0]
    ncp = kc_ref.shape[1]
    nsb = sel_ref.shape[2]
    band_w = min(CMP_BAND, ncp)
    n_band = ncp // band_w
    band = jnp.minimum(((i + 1) * (tq // CMP_STRIDE) + band_w - 1) // band_w, n_band) - 1
    for b in range(n_band):
        cw = band_w * (b + 1)
        sw = min(nsb, -(-(cw * CMP_STRIDE // SLC_BLOCK) // LANE) * LANE)

        @pl.when(band == b)
        def _():
            _nsa_cmp_band(q_ref, kc_ref, vc_ref, ovt_ref, gate_ref, ocmp_ref, sel_ref, psum_scr, imp_scr, cw, sw)


def _nsa_cmp(proj, cmp_kv, overlap_t, gates):
    s = proj.shape[0]
    ncp = cmp_kv.shape[1]
    nsb = overlap_t.shape[0]
    tq = min(Q_TILE, s)
    gw = NSA_REP * HEAD_DIM
    assert ncp % min(CMP_BAND, ncp) == 0
    return pl.pallas_call(
        _nsa_cmp_kernel,
        grid=(N_KV_NSA, s // tq),
        in_specs=[
            pl.BlockSpec((tq, gw), lambda g, i: (i, g)),
            pl.BlockSpec((1, ncp, HEAD_DIM), lambda g, i: (g, 0, 0)),
            pl.BlockSpec((1, ncp, HEAD_DIM), lambda g, i: (N_KV_NSA + g, 0, 0)),
            pl.BlockSpec((nsb, ncp), lambda g, i: (0, 0)),
            pl.BlockSpec((tq, LANE), lambda g, i: (i, g)),
        ],
        out_specs=[
            pl.BlockSpec((tq, gw), lambda g, i: (i, g)),
            pl.BlockSpec((1, tq, nsb), lambda g, i: (g, i, 0)),
        ],
        out_shape=[jax.ShapeDtypeStruct((s, NSA_WIDTH), F32),
                   jax.ShapeDtypeStruct((N_KV_NSA, s, nsb), BF16)],
        scratch_shapes=[pltpu.VMEM((tq, ncp), F32), pltpu.VMEM((nsb, tq), F32)],
        compiler_params=_cparams(("arbitrary", "arbitrary")),
        name="nsa_cmp_select",
    )(proj, cmp_kv, cmp_kv, overlap_t, gates)


def _lane_chunks(s):
    return [s[:, c * LANE:(c + 1) * LANE] for c in range(s.shape[1] // LANE)]


def _row_max(chunks):
    m = functools.reduce(jnp.maximum, chunks)
    return jnp.broadcast_to(jnp.max(m, axis=-1, keepdims=True), m.shape)


def _softmax_weights(chunks, m):
    return jnp.concatenate([jnp.exp2((c - m) * EXP2_SCALE) for c in chunks], axis=1).astype(BF16)


def _flash_init(s, v_aug, m_scr, acc_scr, idx):
    chunks = _lane_chunks(s)
    m = _row_max(chunks)
    acc_scr[idx] = _dot(_softmax_weights(chunks, m), v_aug)
    m_scr[idx] = m


def _flash_step(s, v_aug, m_scr, acc_scr, idx):
    chunks = _lane_chunks(s)
    m_old = m_scr[idx]
    m_new = jnp.maximum(m_old, _row_max(chunks))
    alpha = jnp.exp2((m_old - m_new) * EXP2_SCALE)
    pv = _dot(_softmax_weights(chunks, m_new), v_aug)
    acc = acc_scr[idx]
    acc_scr[idx] = jnp.concatenate([alpha * a for a in _lane_chunks(acc)], axis=1) + pv
    m_scr[idx] = m_new


def _with_ones(v):
    return jnp.concatenate([v, jnp.ones(v.shape, v.dtype)], axis=1)


def _block_onehot(k0, tk, block, width):
    pos = k0 + lax.broadcasted_iota(jnp.int32, (tk, 1), 0)
    blk = lax.shift_right_logical(pos, block.bit_length() - 1) & (width - 1)
    return (blk == lax.broadcasted_iota(jnp.int32, (1, width), 1)).astype(BF16)


def _nsa_slc_kernel(q_ref, ks_ref, vs_ref, kw0_ref, kw1_ref, kw2_ref, vw0_ref, vw1_ref, vw2_ref,
                    sel_ref, gate_ref, ocmp_ref, o_ref, ka_scr, va_scr, qa_scr, m_scr, acc_scr, sa_scr, sb_scr,
                    ow_scr):
    i = pl.program_id(1)
    tq = q_ref.shape[0]
    nsb = sel_ref.shape[2]
    s_len = ks_ref.shape[0]
    tk = min(SLC_KEY_TILE, s_len)
    n_kt = s_len // tk
    width = min(nsb, LANE)
    nhalf = max(nsb // LANE, 1)
    kt_per_half = (LANE * SLC_BLOCK) // tk
    t = i * tq + lax.broadcasted_iota(jnp.int32, (tq, 1), 0)
    kt_diag = (i * tq) // tk

    @pl.when(i == 0)
    def _():
        def build(n, carry):
            k0 = pl.multiple_of(n * tk, tk)
            ka_scr[pl.ds(k0, tk), :] = jnp.concatenate(
                [ks_ref[pl.ds(k0, tk), :], _block_onehot(k0, tk, SLC_BLOCK, width)], axis=1)
            va_scr[pl.ds(k0, tk), :] = _with_ones(vs_ref[pl.ds(k0, tk), :])
            return carry

        lax.fori_loop(0, n_kt, build, 0)
        ka_scr[pl.ds(s_len, tk), :] = jnp.concatenate(
            [jnp.zeros((tk, HEAD_DIM), BF16), jnp.ones((tk, width), BF16)], axis=1)
        va_scr[pl.ds(s_len, tk), :] = jnp.zeros((tk, 2 * HEAD_DIM), BF16)

    bias = ((1.0 - sel_ref[0].astype(F32)) * NEG_INF).astype(BF16)
    for half in range(nhalf):
        for r in range(NSA_REP):
            qa_scr[half, r * tq:(r + 1) * tq, :] = jnp.concatenate(
                [q_ref[:, r * HEAD_DIM:(r + 1) * HEAD_DIM], bias[:, half * width:(half + 1) * width]], axis=1)
    t_all = jnp.concatenate([t] * NSA_REP, axis=0)

    def scores(kt):
        half = jnp.where(kt < n_kt, kt // kt_per_half, 0)
        return _dot_nt(qa_scr[half], ka_scr[pl.ds(pl.multiple_of(kt * tk, tk), tk), :])

    def values(kt):
        return va_scr[pl.ds(pl.multiple_of(kt * tk, tk), tk), :]

    kwin = jnp.concatenate([kw0_ref[...], kw1_ref[...], kw2_ref[...]], axis=0)
    vwin = _with_ones(jnp.concatenate([vw0_ref[...], vw1_ref[...], vw2_ref[...]], axis=0))
    nw = kwin.shape[0]
    wpos = (i - 2) * tq + lax.broadcasted_iota(jnp.int32, (1, nw), 1)
    wvalid = (wpos <= t) & (wpos > t - WINDOW) & (wpos >= 0)
    s = _dot_nt(qa_scr[0, :, :HEAD_DIM], kwin)
    s = jnp.concatenate([jnp.where(wvalid, s[r * tq:(r + 1) * tq], NEG_INF) for r in range(NSA_REP)], axis=0)
    chunks = _lane_chunks(s)
    ow = _dot(_softmax_weights(chunks, _row_max(chunks)), vwin)
    o_win = ow[:, :HEAD_DIM] / ow[:, HEAD_DIM:]
    gates = jax.nn.sigmoid(gate_ref[...])
    for r in range(NSA_REP):
        rows = slice(r * tq, (r + 1) * tq)
        ow_scr[rows, :] = (ocmp_ref[:, r * HEAD_DIM:(r + 1) * HEAD_DIM]
                           + gates[:, 3 * r + 2:3 * r + 3] * o_win[rows])

    kpos = kt_diag * tk + lax.broadcasted_iota(jnp.int32, (1, tk), 1)
    _flash_init(jnp.where(kpos <= t_all, scores(kt_diag), NEG_INF), values(kt_diag), m_scr, acc_scr, Ellipsis)

    last = jnp.maximum(kt_diag - 1, 0)
    sa_scr[...] = scores(0)

    def tile_pair(j, carry):
        kt_b = jnp.where(2 * j + 1 < kt_diag, 2 * j + 1, n_kt)
        sb_scr[...] = scores(kt_b)
        _flash_step(sa_scr[...], values(2 * j), m_scr, acc_scr, Ellipsis)
        sa_scr[...] = scores(jnp.minimum(2 * j + 2, last))
        _flash_step(sb_scr[...], values(kt_b), m_scr, acc_scr, Ellipsis)
        return carry

    lax.fori_loop(0, (kt_diag + 1) // 2, tile_pair, 0)

    acc = acc_scr[...]
    o_slc = acc[:, :HEAD_DIM] / acc[:, HEAD_DIM:]
    gates = jax.nn.sigmoid(gate_ref[...])
    for r in range(NSA_REP):
        rows = slice(r * tq, (r + 1) * tq)
        o = ow_scr[rows, :] + gates[:, 3 * r + 1:3 * r + 2] * o_slc[rows]
        o_ref[:, r * HEAD_DIM:(r + 1) * HEAD_DIM] = o.astype(BF16)


def _nsa_slc(proj, sel, gates, ocmp):
    s = proj.shape[0]
    nsb = sel.shape[2]
    tq = min(Q_TILE, s)
    tk = min(SLC_KEY_TILE, s)
    gw = NSA_REP * HEAD_DIM
    assert WINDOW == 2 * tq and (nsb <= LANE or nsb % LANE == 0)

    def win_spec(col, d):
        return pl.BlockSpec((tq, HEAD_DIM), lambda g, i: (jnp.maximum(i - 2 + d, 0), col + g))

    return pl.pallas_call(
        _nsa_slc_kernel,
        grid=(N_KV_NSA, s // tq),
        in_specs=[
            pl.BlockSpec((tq, gw), lambda g, i: (i, g)),
            pl.BlockSpec((s, HEAD_DIM), lambda g, i: (0, COL_KS + g)),
            pl.BlockSpec((s, HEAD_DIM), lambda g, i: (0, COL_VS + g)),
            win_spec(COL_KW, 0), win_spec(COL_KW, 1), win_spec(COL_KW, 2),
            win_spec(COL_VW, 0), win_spec(COL_VW, 1), win_spec(COL_VW, 2),
            pl.BlockSpec((1, tq, nsb), lambda g, i: (g, i, 0)),
            pl.BlockSpec((tq, LANE), lambda g, i: (i, g)),
            pl.BlockSpec((tq, gw), lambda g, i: (i, g)),
        ],
        out_specs=pl.BlockSpec((tq, gw), lambda g, i: (i, g)),
        out_shape=jax.ShapeDtypeStruct((s, NSA_WIDTH), BF16),
        scratch_shapes=[pltpu.VMEM((s + tk, HEAD_DIM + min(nsb, LANE)), BF16),
                        pltpu.VMEM((s + tk, 2 * HEAD_DIM), BF16),
                        pltpu.VMEM((max(nsb // LANE, 1), NSA_REP * tq, HEAD_DIM + min(nsb, LANE)), BF16),
                        pltpu.VMEM((NSA_REP * tq, LANE), F32),
                        pltpu.VMEM((NSA_REP * tq, 2 * HEAD_DIM), F32),
                        pltpu.VMEM((NSA_REP * tq, tk), F32), pltpu.VMEM((NSA_REP * tq, tk), F32),
                        pltpu.VMEM((NSA_REP * tq, HEAD_DIM), F32)],
        compiler_params=_cparams(("arbitrary", "arbitrary")),
        name="nsa_slc_win",
    )(proj, proj, proj, proj, proj, proj, proj, proj, proj, sel, gates, ocmp)


def _moba_kernel(q_ref, k_ref, v_ref, o_ref, ka_scr, va_scr, qa_scr, km_scr, m_scr, acc_scr, sa_scr, sb_scr):
    i = pl.program_id(1)
    tq = q_ref.shape[0]
    s_len = k_ref.shape[0]
    nbp = km_scr.shape[0]
    tk = min(MOBA_KEY_TILE, tq)
    bpt = tk // MOBA_BLOCK
    n_kt = s_len // tk

    @pl.when(i == 0)
    def _():
        km_scr[...] = jnp.zeros(km_scr.shape, F32)

        def build(n, carry):
            k0 = pl.multiple_of(n * tk, tk)
            kt = k_ref[pl.ds(k0, tk), :]
            ka_scr[pl.ds(k0, tk), :] = jnp.concatenate([kt, _block_onehot(k0, tk, MOBA_BLOCK, nbp)], axis=1)
            va_scr[pl.ds(k0, tk), :] = _with_ones(v_ref[pl.ds(k0, tk), :])
            for j in range(bpt):
                kb = kt[j * MOBA_BLOCK:(j + 1) * MOBA_BLOCK].astype(F32)
                km_scr[pl.ds(n * bpt + j, 1), :] = jnp.mean(kb, axis=0, keepdims=True)
            return carry

        lax.fori_loop(0, n_kt, build, 0)
        ka_scr[pl.ds(s_len, tk), :] = jnp.concatenate(
            [jnp.zeros((tk, HEAD_DIM), BF16), jnp.ones((tk, nbp), BF16)], axis=1)
        va_scr[pl.ds(s_len, tk), :] = jnp.zeros((tk, 2 * HEAD_DIM), BF16)

    q = q_ref[...]
    km_hi, km_lo = _split_bf16(km_scr[...])
    gate = _dot_nt(q, km_hi) + _dot_nt(q, km_lo)
    t = i * tq + lax.broadcasted_iota(jnp.int32, (tq, 1), 0)
    own = lax.shift_right_logical(t, MOBA_BLOCK.bit_length() - 1)
    nidx = lax.broadcasted_iota(jnp.int32, (1, nbp), 1)
    nf = nidx.astype(F32)
    work = jnp.where(nidx < own, gate, LOWEST)
    sel = nidx == own
    for _ in range(MOBA_TOPK):
        m = jnp.max(work, axis=-1, keepdims=True)
        first = jnp.min(jnp.where((work == m) & (m > 0.5 * LOWEST), nf, 1e9), axis=-1, keepdims=True)
        hit = nf == first
        sel = sel | hit
        work = jnp.where(hit, LOWEST, work)
    qa_scr[...] = jnp.concatenate([q, jnp.where(sel, 0.0, NEG_INF).astype(BF16)], axis=1)

    def scores(kt):
        return _dot_nt(qa_scr[...], ka_scr[pl.ds(pl.multiple_of(kt * tk, tk), tk), :])

    def values(kt):
        return va_scr[pl.ds(pl.multiple_of(kt * tk, tk), tk), :]

    d0 = pl.multiple_of(i * tq, tq)
    for c in range(tq // tk):
        rows = slice(c * tk, (c + 1) * tk)
        nk = (c + 1) * tk
        kpos = d0 + lax.broadcasted_iota(jnp.int32, (1, nk), 1)
        s_diag = _dot_nt(qa_scr[rows, :], ka_scr[pl.ds(d0, nk), :])
        _flash_init(jnp.where(kpos <= t[rows], s_diag, NEG_INF), va_scr[pl.ds(d0, nk), :], m_scr, acc_scr, rows)

    n_past = i * (tq // tk)
    last = jnp.maximum(n_past - 1, 0)
    sa_scr[...] = scores(0)

    def tile_pair(j, carry):
        kt_b = jnp.where(2 * j + 1 < n_past, 2 * j + 1, n_kt)
        sb_scr[...] = scores(kt_b)
        _flash_step(sa_scr[...], values(2 * j), m_scr, acc_scr, Ellipsis)
        sa_scr[...] = scores(jnp.minimum(2 * j + 2, last))
        _flash_step(sb_scr[...], values(kt_b), m_scr, acc_scr, Ellipsis)
        return carry

    lax.fori_loop(0, (n_past + 1) // 2, tile_pair, 0)
    acc = acc_scr[...]
    o_ref[...] = (acc[:, :HEAD_DIM] / acc[:, HEAD_DIM:]).astype(BF16)


def _moba(proj):
    s = proj.shape[0]
    tq = min(MOBA_Q_TILE, s)
    tk = min(MOBA_KEY_TILE, tq)
    nbp = max(LANE, s // MOBA_BLOCK)
    assert nbp & (nbp - 1) == 0 and tq % tk == 0
    return pl.pallas_call(
        _moba_kernel,
        grid=(N_HEADS_MOBA, s // tq),
        in_specs=[
            pl.BlockSpec((tq, HEAD_DIM), lambda h, i: (i, COL_QM + h)),
            pl.BlockSpec((s, HEAD_DIM), lambda h, i: (0, COL_KM + h)),
            pl.BlockSpec((s, HEAD_DIM), lambda h, i: (0, COL_VM + h)),
        ],
        out_specs=pl.BlockSpec((tq, HEAD_DIM), lambda h, i: (i, h)),
        out_shape=jax.ShapeDtypeStruct((s, MOBA_WIDTH), BF16),
        scratch_shapes=[pltpu.VMEM((s + tk, HEAD_DIM + nbp), BF16), pltpu.VMEM((s + tk, 2 * HEAD_DIM), BF16),
                        pltpu.VMEM((tq, HEAD_DIM + nbp), BF16),
                        pltpu.VMEM((nbp, HEAD_DIM), F32), pltpu.VMEM((tq, LANE), F32),
                        pltpu.VMEM((tq, 2 * HEAD_DIM), F32),
                        pltpu.VMEM((tq, tk), F32), pltpu.VMEM((tq, tk), F32)],
        compiler_params=_cparams(("arbitrary", "arbitrary")),
        name="moba",
    )(proj, proj, proj)


def _out_kernel(on_ref, om_ref, x_ref, wn_ref, wm_ref, gt_ref, g_ref, sc_ref, sh_ref, wrh_ref, wrl_ref,
                x1_ref, h_ref, lg_ref):
    mix = _dot(on_ref[...], wn_ref[...]) + _dot(om_ref[...], wm_ref[...])
    x1 = x_ref[...] + gt_ref[...] * mix
    x1_ref[...] = x1
    h = _rms_mod(x1, g_ref[...], sc_ref[...], sh_ref[...])
    h_hi, h_lo = _split_bf16(h)
    h_ref[...] = h
    wrh = wrh_ref[...]
    lg_ref[...] = _dot(h_hi, wrh) + _dot(h_lo, wrh) + _dot(h_hi, wrl_ref[...])


def _out_proj(o_nsa, o_moba, x, w_n, w_m, gt, g, sc, sh, wr_hi, wr_lo):
    s, d = x.shape
    tm = min(OUT_TM, s)
    row = lambda i: (i, 0)
    fix = lambda i: (0, 0)
    return pl.pallas_call(
        _out_kernel,
        grid=(s // tm,),
        in_specs=[
            pl.BlockSpec((tm, NSA_WIDTH), row),
            pl.BlockSpec((tm, MOBA_WIDTH), row),
            pl.BlockSpec((tm, d), row),
            pl.BlockSpec((NSA_WIDTH, d), fix),
            pl.BlockSpec((MOBA_WIDTH, d), fix),
            pl.BlockSpec((1, d), fix),
            pl.BlockSpec((1, d), fix),
            pl.BlockSpec((1, d), fix),
            pl.BlockSpec((1, d), fix),
            pl.BlockSpec((d, LANE), fix),
            pl.BlockSpec((d, LANE), fix),
        ],
        out_specs=[pl.BlockSpec((tm, d), row), pl.BlockSpec((tm, d), row), pl.BlockSpec((tm, LANE), row)],
        out_shape=[jax.ShapeDtypeStruct((s, d), F32), jax.ShapeDtypeStruct((s, d), F32),
                   jax.ShapeDtypeStruct((s, LANE), F32)],
        compiler_params=_cparams(("arbitrary",)),
        name="out_proj_norm_router",
    )(o_nsa, o_moba, x, w_n, w_m, gt, g, sc, sh, wr_hi, wr_lo)


def _row_copy(src_ref, dst_ref, src_row, dst_row, sem):
    return pltpu.make_async_copy(src_ref.at[pl.ds(src_row, 1)], dst_ref.at[pl.ds(dst_row, 1)], sem)


def _expert_kernel(be_ref, nu_ref, tok_ref, h_ref, wg_ref, wu_ref, wd_ref, y_ref,
                   x_buf, wg_s, wu_s, wd_s, sems):
    b = pl.program_id(0)
    n_used = nu_ref[0]

    def row_copy(target, r):
        blk = jnp.minimum(target, n_used - 1)
        buf = target % EXPERT_BUFFERS
        return _row_copy(h_ref, x_buf.at[buf], tok_ref[blk * MOE_ROWS + r], r, sems.at[buf])

    def gather(target, wait):
        def row(r, carry):
            if wait:
                row_copy(target, r).wait()
            else:
                row_copy(target, r).start()
            return carry

        lax.fori_loop(0, MOE_ROWS, row, 0, unroll=DMA_UNROLL)

    @pl.when((b == 0) & (n_used > 0))
    def _():
        gather(0, False)
        gather(1, False)

    prev = be_ref[jnp.maximum(b - 1, 0)]

    @pl.when((b == 0) | (be_ref[b] != prev))
    def _():
        wg_s[...] = wg_ref[0, 0].astype(BF16)
        wu_s[...] = wu_ref[0, 0].astype(BF16)
        wd_s[...] = wd_ref[0, 0].astype(BF16)

    @pl.when(b < n_used)
    def _():
        gather(b, True)
        for r in range(MOE_ROWS):
            row_copy(b + EXPERT_BUFFERS - 1, r).start()
        x = x_buf[b % EXPERT_BUFFERS].astype(BF16)
        a = _dot(x, wg_s[...])
        u = _dot(x, wu_s[...])
        hid = (a * jax.nn.sigmoid(a) * u).astype(BF16)
        y_ref[...] = _dot(hid, wd_s[...])

    @pl.when((b < n_used) & (b + EXPERT_BUFFERS - 1 >= n_used))
    def _():
        gather(b + EXPERT_BUFFERS - 1, True)

    @pl.when((b == 0) & (n_used == 1))
    def _():
        gather(1, True)

    @pl.when(b >= n_used)
    def _():
        y_ref[...] = jnp.zeros(y_ref.shape, F32)


def _experts(h, slot_tok, blk_exp, n_used, w_gate, w_up, w_down, layer):
    n_slots = slot_tok.shape[0]
    d = h.shape[1]
    n_blk = n_slots // MOE_ROWS
    grid_spec = pltpu.PrefetchScalarGridSpec(
        num_scalar_prefetch=3,
        grid=(n_blk,),
        in_specs=[
            pl.BlockSpec(memory_space=pl.ANY),
            pl.BlockSpec((1, 1, d, D_EXPERT), lambda b, be, nu, tok: (layer, be[b], 0, 0)),
            pl.BlockSpec((1, 1, d, D_EXPERT), lambda b, be, nu, tok: (layer, be[b], 0, 0)),
            pl.BlockSpec((1, 1, D_EXPERT, d), lambda b, be, nu, tok: (layer, be[b], 0, 0)),
        ],
        out_specs=pl.BlockSpec((MOE_ROWS, d), lambda b, be, nu, tok: (b, 0)),
        scratch_shapes=[pltpu.VMEM((EXPERT_BUFFERS, MOE_ROWS, d), F32),
                        pltpu.VMEM((d, D_EXPERT), BF16), pltpu.VMEM((d, D_EXPERT), BF16),
                        pltpu.VMEM((D_EXPERT, d), BF16), pltpu.SemaphoreType.DMA((EXPERT_BUFFERS,))],
    )
    return pl.pallas_call(
        _expert_kernel,
        grid_spec=grid_spec,
        out_shape=jax.ShapeDtypeStruct((n_slots, d), F32),
        compiler_params=_cparams(("arbitrary",)),
        name="moe_experts",
    )(blk_exp, n_used, slot_tok, h, w_gate, w_up, w_down)


def _combine_kernel(sa_ref, sb_ref, na_ref, nb_ref, x_ref, w_ref, gt_ref, gf_ref, y_ref, o_ref, buf, sems, *,
                    final):
    b = pl.program_id(0)
    tt = x_ref.shape[0]

    def gather(ia_ref, ib_ref, slot, wait):
        def row(r, carry):
            for k, idx_ref in enumerate((ia_ref, ib_ref)):
                cp = _row_copy(y_ref, buf.at[slot].at[k], idx_ref[0, 0, r], r, sems.at[slot].at[k])
                if wait:
                    cp.wait()
                else:
                    cp.start()
            return carry

        lax.fori_loop(0, tt, row, 0, unroll=DMA_UNROLL)

    @pl.when(b == 0)
    def _():
        gather(sa_ref, sb_ref, 0, False)

    @pl.when(b + 1 < pl.num_programs(0))
    def _():
        gather(na_ref, nb_ref, (b + 1) % 2, False)

    gather(sa_ref, sb_ref, b % 2, True)
    w = w_ref[...]
    x2 = x_ref[...] + gt_ref[...] * (w[:, 0:1] * buf[b % 2, 0] + w[:, 1:2] * buf[b % 2, 1])
    if final:
        x2 = x2 * lax.rsqrt(jnp.mean(x2 * x2, axis=-1, keepdims=True) + RMS_EPS) * gf_ref[...]
    o_ref[...] = x2


def _combine(x1, ys, slot_a, slot_b, pair_w, gt, g_final, final):
    s, d = x1.shape
    tt = min(GATHER_ROWS, s)
    nt = s // tt
    cur_spec = pl.BlockSpec((1, 1, tt), lambda b: (b, 0, 0), memory_space=pltpu.SMEM)
    nxt_spec = pl.BlockSpec((1, 1, tt), lambda b: (jnp.minimum(b + 1, nt - 1), 0, 0), memory_space=pltpu.SMEM)
    slot_a = slot_a.reshape(nt, 1, tt)
    slot_b = slot_b.reshape(nt, 1, tt)
    return pl.pallas_call(
        functools.partial(_combine_kernel, final=final),
        grid=(nt,),
        in_specs=[
            cur_spec, cur_spec, nxt_spec, nxt_spec,
            pl.BlockSpec((tt, d), lambda b: (b, 0)),
            pl.BlockSpec((tt, LANE), lambda b: (b, 0)),
            pl.BlockSpec((1, d), lambda b: (0, 0)),
            pl.BlockSpec((1, d), lambda b: (0, 0)),
            pl.BlockSpec(memory_space=pl.ANY),
        ],
        out_specs=pl.BlockSpec((tt, d), lambda b: (b, 0)),
        out_shape=jax.ShapeDtypeStruct((s, d), F32),
        scratch_shapes=[pltpu.VMEM((2, 2, tt, d), F32), pltpu.SemaphoreType.DMA((2, 2))],
        compiler_params=_cparams(("arbitrary",)),
        name="moe_combine",
    )(slot_a, slot_b, slot_a, slot_b, x1, jnp.pad(pair_w, ((0, 0), (0, LANE - TOP_K_EXPERTS))), gt, g_final, ys)


def _route(logits, b_router):
    n = logits.shape[0]
    scores = jax.nn.sigmoid(logits[:, :N_EXPERTS])
    biased = (scores + b_router.astype(F32)).reshape(n, N_GROUPS, EXPERTS_PER_GROUP)

    def top2(v):
        idx = lax.broadcasted_iota(jnp.int32, v.shape, v.ndim - 1)
        v1 = jnp.max(v, axis=-1, keepdims=True)
        i1 = jnp.min(jnp.where(v == v1, idx, v.shape[-1]), axis=-1, keepdims=True)
        rest = jnp.where(idx == i1, -jnp.inf, v)
        v2 = jnp.max(rest, axis=-1, keepdims=True)
        i2 = jnp.min(jnp.where(rest == v2, idx, v.shape[-1]), axis=-1, keepdims=True)
        return v1, v2, i1, i2

    g1, g2, _, _ = top2(biased)
    g_sum = (g1 + g2)[..., 0]
    g_idx = lax.broadcasted_iota(jnp.int32, g_sum.shape, 1)
    grp = jnp.min(jnp.where(g_sum == jnp.max(g_sum, axis=-1, keepdims=True), g_idx, N_GROUPS), axis=-1)
    grp_hot = grp[:, None] == jnp.arange(N_GROUPS, dtype=grp.dtype)[None, :]
    in_grp = jnp.sum(jnp.where(grp_hot[:, :, None], biased, 0.0), axis=1)
    _, _, l1, l2 = top2(in_grp)
    expert = (grp[:, None] * EXPERTS_PER_GROUP + jnp.concatenate([l1, l2], axis=1)).astype(jnp.int32)
    e_hot = expert[:, :, None] == jnp.arange(N_EXPERTS, dtype=jnp.int32)[None, None, :]
    wsel = jnp.sum(jnp.where(e_hot, scores[:, None, :], 0.0), axis=-1)
    wsel = wsel / jnp.sum(wsel, axis=-1, keepdims=True)

    n_pair = n * TOP_K_EXPERTS
    flat_e = expert.reshape(-1)
    flat_t = jnp.repeat(jnp.arange(n, dtype=jnp.int32), TOP_K_EXPERTS)
    hot = flat_e[:, None] == jnp.arange(N_EXPERTS, dtype=jnp.int32)[None, :]
    onehot = hot.astype(jnp.int32)
    chunk = min(LANE, n_pair)
    n_chunk = n_pair // chunk
    tri = (jnp.arange(chunk)[:, None] >= jnp.arange(chunk)[None, :]).astype(BF16)
    within = jnp.einsum('ij,cjk->cik', tri, hot.astype(BF16).reshape(n_chunk, chunk, N_EXPERTS),
                        preferred_element_type=F32)
    totals = within[:, -1, :]
    before = jnp.cumsum(totals, axis=0) - totals
    running = (within + before[:, None, :]).astype(jnp.int32).reshape(n_pair, N_EXPERTS)
    rank = jnp.sum(running * onehot, axis=1) - 1
    counts = running[-1]
    padded = (counts + MOE_ROWS - 1) // MOE_ROWS * MOE_ROWS
    pend = jnp.cumsum(padded)
    pstart = pend - padded
    slot = (jnp.sum(pstart[None, :] * onehot, axis=1) + rank).astype(jnp.int32)
    n_slots = n_pair + N_EXPERTS * MOE_ROWS
    slot_tok = jnp.zeros((n_slots,), jnp.int32).at[slot].set(flat_t)
    pair_slot = slot.reshape(n, TOP_K_EXPERTS)
    n_blk = n_slots // MOE_ROWS
    blk_start = jnp.arange(n_blk, dtype=jnp.int32) * MOE_ROWS
    blk_exp = jnp.minimum(jnp.sum((pend[None, :] <= blk_start[:, None]).astype(jnp.int32), axis=1), N_EXPERTS - 1)
    n_used = (pend[-1:] // MOE_ROWS).astype(jnp.int32)
    return slot_tok, wsel, pair_slot, blk_exp, n_used


def _rope_tables(pos):
    inv = 1.0 / (ROPE_THETA ** (jnp.arange(0, HEAD_DIM, 2, dtype=F32) / HEAD_DIM))
    ang = pos.astype(F32)[:, None] * inv[None, :]
    cos = jnp.cos(ang)
    sin = jnp.sin(ang)
    return jnp.concatenate([cos, cos], axis=-1), jnp.concatenate([-sin, sin], axis=-1)


def _layout_w_in(w_in):
    o_g = NSA_WIDTH + 6 * NSA_KV_WIDTH
    main = jnp.concatenate([w_in[:, :o_g], w_in[:, o_g + N_GATES:]], axis=1).astype(BF16)
    wg = w_in[:, o_g:o_g + N_GATES].reshape(-1, N_KV_NSA, 3 * NSA_REP)
    wg = jnp.pad(wg, ((0, 0), (0, 0), (0, LANE - 3 * NSA_REP))).reshape(-1, N_KV_NSA * LANE)
    return main, wg.astype(BF16)


def _overlap_t(ncp, nsb):
    cs = jnp.arange(ncp, dtype=jnp.int32)[None, :] * CMP_STRIDE
    ss = jnp.arange(nsb, dtype=jnp.int32)[:, None] * SLC_BLOCK
    return ((cs + CMP_LEN - 1 >= ss) & (cs < ss + SLC_BLOCK)).astype(BF16)


def kernel(x, c, w_ada, b_ada, g_attn, g_ffn, w_in, w_out, cmp_pe, cmp_w1, cmp_b1, cmp_w2,
           w_router, b_router, w_gate, w_up, w_down, g_final):
    b, s, d = x.shape
    assert b == 1 and d == D_MODEL and s % max(Q_TILE, SLC_KEY_TILE, MOBA_Q_TILE) == 0
    depth = w_ada.shape[0]
    ncp = s // CMP_STRIDE
    nsb = s // SLC_BLOCK

    cos, sin = _rope_tables(jnp.arange(s))
    cos_c, sin_c = _rope_tables(jnp.arange(ncp) * CMP_STRIDE + CMP_LEN - 1)
    overlap_t = _overlap_t(ncp, nsb)
    flags = jnp.asarray(ROPE_FLAGS, jnp.int32)
    wr = jnp.pad(w_router.astype(F32), ((0, 0), (0, LANE - N_EXPERTS)))
    wr_hi, wr_lo = _split_bf16(wr)

    mod = _ada_mod(c.astype(F32), w_ada, b_ada)
    xs = x.reshape(s, d)
    for l in range(depth):
        sh1, sc1, gt1, sh2, sc2, gt2 = [mod[l, :, k * d:(k + 1) * d] for k in range(6)]
        w_main, w_gates = _layout_w_in(w_in[l])
        proj, gates = _project(xs, g_attn[l].reshape(1, d), sc1, sh1, w_main, w_gates, cos, sin, flags)

        chunks = proj[:, COL_KC * LANE:COL_KS * LANE].reshape(s, 2 * N_KV_NSA, HEAD_DIM)
        chunks = chunks.transpose(1, 0, 2).reshape(2 * N_KV_NSA, ncp, CMP_STRIDE * HEAD_DIM)
        pe8 = jnp.pad(cmp_pe[l].reshape(2, 1, CMP_LEN * HEAD_DIM), ((0, 0), (0, 7), (0, 0)))
        cmp_kv = _compress(chunks, cmp_w1[l], cmp_b1[l].reshape(2, 1, CMP_HIDDEN), cmp_w2[l], pe8, cos_c, sin_c)

        ocmp, sel = _nsa_cmp(proj, cmp_kv, overlap_t, gates)
        o_nsa = _nsa_slc(proj, sel, gates, ocmp)
        o_moba = _moba(proj)

        w_o = w_out[l].astype(BF16)
        x1, h2, logits = _out_proj(o_nsa, o_moba, xs, w_o[:NSA_WIDTH], w_o[NSA_WIDTH:], gt1,
                                   g_ffn[l].reshape(1, d), sc2, sh2, wr_hi, wr_lo)

        slot_tok, pair_w, pair_slot, blk_exp, n_used = _route(logits, b_router)
        ys = _experts(h2, slot_tok, blk_exp, n_used, w_gate, w_up, w_down, l)
        xs = _combine(x1, ys, pair_slot[:, 0], pair_slot[:, 1], pair_w, gt2, g_final.reshape(1, d),
                      l == depth - 1)
    return xs.reshape(b, s, d)
```

```python
import functools

import jax
import jax.numpy as jnp
from jax import lax
from jax.experimental import pallas as pl
from jax.experimental.pallas import tpu as pltpu

F32 = jnp.float32
BF16 = jnp.bfloat16

D_MODEL = 2048
HEAD_DIM = 128
N_HEADS_NSA = 8
N_KV_NSA = 2
NSA_REP = N_HEADS_NSA // N_KV_NSA
N_HEADS_MOBA = 8
NSA_WIDTH = N_HEADS_NSA * HEAD_DIM
NSA_KV_WIDTH = N_KV_NSA * HEAD_DIM
MOBA_WIDTH = N_HEADS_MOBA * HEAD_DIM
N_GATES = 3 * N_HEADS_NSA
ROPE_THETA = 10000.0
CMP_LEN = 32
CMP_STRIDE = 16
CMP_HIDDEN = 256
SLC_BLOCK = 64
SLC_TOPN = 16
N_FORCED = 3
WINDOW = 512
MOBA_BLOCK = 256
MOBA_TOPK = 3
N_EXPERTS = 32
N_GROUPS = 4
EXPERTS_PER_GROUP = N_EXPERTS // N_GROUPS
TOP_K_EXPERTS = 2
D_EXPERT = 512
RMS_EPS = 1e-6
NEG_INF = -1e30
LOWEST = -3e38
PICKED = -2e38
SCALE = HEAD_DIM ** -0.5
EXP2_SCALE = SCALE * 1.4426950408889634

LANE = 128
Q_TILE = 256
MOBA_Q_TILE = 1024
MOBA_KEY_TILE = 512
SLC_KEY_TILE = 512
CMP_BAND = 256
MOE_ROWS = 256
GATHER_ROWS = 256
DMA_UNROLL = 8
EXPERT_BUFFERS = 3
PROJ_TM = 1024
PROJ_TN = 512
OUT_TM = 256
VMEM_LIMIT = 56 * 1024 * 1024

COL_QN = 0
COL_KC = 8
COL_VC = 10
COL_KS = 12
COL_VS = 14
COL_KW = 16
COL_VW = 18
COL_QM = 20
COL_KM = 28
COL_VM = 36
PROJ_COLS = 44
ROPE_FLAGS = tuple(
    1 if (c < 8 or c in (12, 13, 16, 17) or 20 <= c < 36) else 0 for c in range(PROJ_COLS))


def _cparams(sem):
    return pltpu.CompilerParams(dimension_semantics=sem, vmem_limit_bytes=VMEM_LIMIT)


def _dot(a, b):
    return jnp.dot(a, b, preferred_element_type=F32)


def _dot_nt(a, b):
    return lax.dot_general(a, b, (((1,), (1,)), ((), ())), preferred_element_type=F32)


def _split_bf16(v):
    hi = v.astype(BF16)
    lo = (v - hi.astype(F32)).astype(BF16)
    return hi, lo


def _mod_kernel(c_ref, w_ref, b_ref, o_ref):
    c = c_ref[...]
    cond = c * jax.nn.sigmoid(c)
    o_ref[0] = jnp.sum(w_ref[0] * cond, axis=0, keepdims=True) + b_ref[0]


def _ada_mod(c, w_ada, b_ada):
    depth, d, n = w_ada.shape
    tn = 1024
    return pl.pallas_call(
        _mod_kernel,
        grid=(depth, n // tn),
        in_specs=[
            pl.BlockSpec((d, 1), lambda l, j: (0, 0)),
            pl.BlockSpec((1, d, tn), lambda l, j: (l, 0, j)),
            pl.BlockSpec((1, 1, tn), lambda l, j: (l, 0, j)),
        ],
        out_specs=pl.BlockSpec((1, 1, tn), lambda l, j: (l, 0, j)),
        out_shape=jax.ShapeDtypeStruct((depth, 1, n), F32),
        compiler_params=_cparams(("arbitrary", "arbitrary")),
        name="ada_mod",
    )(c.reshape(d, 1), w_ada, b_ada.reshape(depth, 1, n))


def _rms_mod(x, g, sc, sh):
    y = x * lax.rsqrt(jnp.mean(x * x, axis=-1, keepdims=True) + RMS_EPS)
    return (y * g) * (1.0 + sc) + sh


def _proj_kernel(flags_ref, x_ref, g_ref, sc_ref, sh_ref, w_ref, wg_ref, cos_ref, sin_ref,
                 o_ref, gate_ref, h_scr):
    j = pl.program_id(1)

    @pl.when(j == 0)
    def _():
        hb = _rms_mod(x_ref[...], g_ref[...], sc_ref[...], sh_ref[...]).astype(BF16)
        h_scr[...] = hb
        gate_ref[...] = _dot(hb, wg_ref[...])

    acc = _dot(h_scr[...], w_ref[...])
    cos = cos_ref[...]
    sin = sin_ref[...]
    nch = PROJ_TN // LANE
    for c in range(nch):
        a = acc[:, c * LANE:(c + 1) * LANE]
        roped = a * cos + pltpu.roll(a, HEAD_DIM // 2, 1) * sin
        f = flags_ref[j * nch + c]
        o_ref[:, c * LANE:(c + 1) * LANE] = jnp.where(f > 0, roped, a).astype(BF16)


def _project(x, g, sc, sh, w_main, w_gate, cos, sin, flags):
    s, d = x.shape
    n = w_main.shape[1]
    tm = min(PROJ_TM, s)
    grid_spec = pltpu.PrefetchScalarGridSpec(
        num_scalar_prefetch=1,
        grid=(s // tm, n // PROJ_TN),
        in_specs=[
            pl.BlockSpec((tm, d), lambda i, j, f: (i, 0)),
            pl.BlockSpec((1, d), lambda i, j, f: (0, 0)),
            pl.BlockSpec((1, d), lambda i, j, f: (0, 0)),
            pl.BlockSpec((1, d), lambda i, j, f: (0, 0)),
            pl.BlockSpec((d, PROJ_TN), lambda i, j, f: (0, j)),
            pl.BlockSpec((d, 2 * LANE), lambda i, j, f: (0, 0)),
            pl.BlockSpec((tm, LANE), lambda i, j, f: (i, 0)),
            pl.BlockSpec((tm, LANE), lambda i, j, f: (i, 0)),
        ],
        out_specs=[
            pl.BlockSpec((tm, PROJ_TN), lambda i, j, f: (i, j)),
            pl.BlockSpec((tm, 2 * LANE), lambda i, j, f: (i, 0)),
        ],
        scratch_shapes=[pltpu.VMEM((tm, d), BF16)],
    )
    return pl.pallas_call(
        _proj_kernel,
        grid_spec=grid_spec,
        out_shape=[jax.ShapeDtypeStruct((s, n), BF16), jax.ShapeDtypeStruct((s, 2 * LANE), F32)],
        compiler_params=_cparams(("arbitrary", "arbitrary")),
        name="norm_proj_rope",
    )(flags, x, g, sc, sh, w_main, w_gate, cos, sin)


def _compress_kernel(ch_ref, w1_ref, b1_ref, w2_ref, pe_ref, cos_ref, sin_ref, o_ref):
    a = pl.program_id(0)
    ch = ch_ref[0]
    nc = ch.shape[0]
    half = CMP_STRIDE * HEAD_DIM
    w1 = w1_ref[0].astype(BF16)
    top = _dot(ch, w1[:half])
    bot = _dot(ch, w1[half:])
    bot = pltpu.roll(bot, nc - 1, 0)
    pe_term = _dot(pe_ref[0].astype(BF16), w1)[0:1]
    hid = jax.nn.gelu(top + bot + pe_term + b1_ref[0])
    out = _dot(hid.astype(BF16), w2_ref[0].astype(BF16))
    roped = out * cos_ref[...] + pltpu.roll(out, HEAD_DIM // 2, 1) * sin_ref[...]
    out = jnp.where(a < N_KV_NSA, roped, out)
    row = lax.broadcasted_iota(jnp.int32, out.shape, 0)
    o_ref[0] = jnp.where(row < nc - 1, out, 0.0).astype(BF16)


def _compress(chunks, w1, b1, w2, pe8, cos_c, sin_c):
    na, nc, kw = chunks.shape
    return pl.pallas_call(
        _compress_kernel,
        grid=(na,),
        in_specs=[
            pl.BlockSpec((1, nc, kw), lambda a: (a, 0, 0)),
            pl.BlockSpec((1, CMP_LEN * HEAD_DIM, CMP_HIDDEN), lambda a: (a // N_KV_NSA, 0, 0)),
            pl.BlockSpec((1, 1, CMP_HIDDEN), lambda a: (a // N_KV_NSA, 0, 0)),
            pl.BlockSpec((1, CMP_HIDDEN, HEAD_DIM), lambda a: (a // N_KV_NSA, 0, 0)),
            pl.BlockSpec((1, 8, CMP_LEN * HEAD_DIM), lambda a: (a // N_KV_NSA, 0, 0)),
            pl.BlockSpec((nc, HEAD_DIM), lambda a: (0, 0)),
            pl.BlockSpec((nc, HEAD_DIM), lambda a: (0, 0)),
        ],
        out_specs=pl.BlockSpec((1, nc, HEAD_DIM), lambda a: (a, 0, 0)),
        out_shape=jax.ShapeDtypeStruct((na, nc, HEAD_DIM), BF16),
        compiler_params=_cparams(("arbitrary",)),
        name="nsa_compress",
    )(chunks, w1, b1, w2, pe8, cos_c, sin_c)


def _nsa_cmp_band(q_ref, kc_ref, vc_ref, ovt_ref, gate_ref, ocmp_ref, sel_ref, psum_scr, imp_scr, cw, sw):
    i = pl.program_id(1)
    tq = q_ref.shape[0]
    nsb = sel_ref.shape[2]
    t = i * tq + lax.broadcasted_iota(jnp.int32, (tq, 1), 0)
    cend = lax.broadcasted_iota(jnp.int32, (1, cw), 1) * CMP_STRIDE + (CMP_LEN - 1)
    mask = cend <= t
    kc = kc_ref[0, :cw, :]
    vc = vc_ref[0, :cw, :]
    gates = jax.nn.sigmoid(gate_ref[...])
    row_ok = t >= CMP_LEN - 1
    for r in range(NSA_REP):
        q = q_ref[:, r * HEAD_DIM:(r + 1) * HEAD_DIM]
        chunks = _lane_chunks(jnp.where(mask, _dot_nt(q, kc), NEG_INF))
        m = _row_max(chunks)
        e = [jnp.exp2((c - m) * EXP2_SCALE) for c in chunks]
        l = jnp.sum(functools.reduce(jnp.add, e), axis=-1, keepdims=True)
        inv = jnp.broadcast_to(jnp.where(row_ok, 1.0 / l, 0.0), m.shape)
        p = jnp.concatenate([c * inv for c in e], axis=1)
        psum_scr[:, :cw] = p if r == 0 else psum_scr[:, :cw] + p
        o = _dot(p.astype(BF16), vc)
        ocmp_ref[:, r * HEAD_DIM:(r + 1) * HEAD_DIM] = gates[:, 3 * r:3 * r + 1] * o

    p_hi, p_lo = _split_bf16(psum_scr[:, :cw])
    ovt = ovt_ref[:sw, :cw]
    imp_scr[:sw, :] = _dot_nt(ovt, p_hi) + _dot_nt(ovt, p_lo)

    jblk = lax.broadcasted_iota(jnp.int32, (sw, 1), 0)
    jf = jblk.astype(F32)
    lt = min(LANE, tq)
    picks = []
    for c in range(tq // lt):
        tb = lax.shift_right_logical(i * tq + c * lt + lax.broadcasted_iota(jnp.int32, (1, lt), 1),
                                     SLC_BLOCK.bit_length() - 1)
        valid = jblk <= tb
        forced = valid & ((jblk == 0) | (jblk == tb) | (jblk == tb - 1))
        work = jnp.where(valid & jnp.logical_not(forced), imp_scr[:sw, c * lt:(c + 1) * lt], LOWEST)
        for _ in range(min(SLC_TOPN, nsb) - N_FORCED):
            m = jnp.max(work, axis=0, keepdims=True)
            first = jnp.min(jnp.where(work == m, jf, 1e9), axis=0, keepdims=True)
            work = jnp.where(jf == first, PICKED, work)
        picks.append(jnp.where(forced | (work == PICKED), 1.0, 0.0))
    sel_ref[0, :, :sw] = jnp.concatenate(picks, axis=1).T.astype(BF16)
    if sw < nsb:
        sel_ref[0, :, sw:] = jnp.zeros((tq, nsb - sw), BF16)


def _nsa_cmp_kernel(q_ref, kc_ref, vc_ref, ovt_ref, gate_ref, ocmp_ref, sel_ref, psum_scr, imp_scr):
    i = pl.program_id(1)
    tq = q_ref.shape[0]
    ncp = kc_ref.shape[1]
    nsb = sel_ref.shape[2]
    band_w = min(CMP_BAND, ncp)
    n_band = ncp // band_w
    band = jnp.minimum(((i + 1) * (tq // CMP_STRIDE) + band_w - 1) // band_w, n_band) - 1
    for b in range(n_band):
        cw = band_w * (b + 1)
        sw = min(nsb, -(-(cw * CMP_STRIDE // SLC_BLOCK) // LANE) * LANE)

        @pl.when(band == b)
        def _():
            _nsa_cmp_band(q_ref, kc_ref, vc_ref, ovt_ref, gate_ref, ocmp_ref, sel_ref, psum_scr, imp_scr, cw, sw)


def _nsa_cmp(proj, cmp_kv, overlap_t, gates):
    s = proj.shape[0]
    ncp = cmp_kv.shape[1]
    nsb = overlap_t.shape[0]
    tq = min(Q_TILE, s)
    gw = NSA_REP * HEAD_DIM
    assert ncp % min(CMP_BAND, ncp) == 0
    return pl.pallas_call(
        _nsa_cmp_kernel,
        grid=(N_KV_NSA, s // tq),
        in_specs=[
            pl.BlockSpec((tq, gw), lambda g, i: (i, g)),
            pl.BlockSpec((1, ncp, HEAD_DIM), lambda g, i: (g, 0, 0)),
            pl.BlockSpec((1, ncp, HEAD_DIM), lambda g, i: (N_KV_NSA + g, 0, 0)),
            pl.BlockSpec((nsb, ncp), lambda g, i: (0, 0)),
            pl.BlockSpec((tq, LANE), lambda g, i: (i, g)),
        ],
        out_specs=[
            pl.BlockSpec((tq, gw), lambda g, i: (i, g)),
            pl.BlockSpec((1, tq, nsb), lambda g, i: (g, i, 0)),
        ],
        out_shape=[jax.ShapeDtypeStruct((s, NSA_WIDTH), F32),
                   jax.ShapeDtypeStruct((N_KV_NSA, s, nsb), BF16)],
        scratch_shapes=[pltpu.VMEM((tq, ncp), F32), pltpu.VMEM((nsb, tq), F32)],
        compiler_params=_cparams(("arbitrary", "arbitrary")),
        name="nsa_cmp_select",
    )(proj, cmp_kv, cmp_kv, overlap_t, gates)


def _lane_chunks(s):
    return [s[:, c * LANE:(c + 1) * LANE] for c in range(s.shape[1] // LANE)]


def _row_max(chunks):
    m = functools.reduce(jnp.maximum, chunks)
    return jnp.broadcast_to(jnp.max(m, axis=-1, keepdims=True), m.shape)


def _softmax_weights(chunks, m):
    return jnp.concatenate([jnp.exp2((c - m) * EXP2_SCALE) for c in chunks], axis=1).astype(BF16)


def _flash_init(s, v_aug, m_scr, acc_scr, idx):
    chunks = _lane_chunks(s)
    m = _row_max(chunks)
    acc_scr[idx] = _dot(_softmax_weights(chunks, m), v_aug)
    m_scr[idx] = m


def _flash_step(s, v_aug, m_scr, acc_scr, idx):
    chunks = _lane_chunks(s)
    m_old = m_scr[idx]
    m_new = jnp.maximum(m_old, _row_max(chunks))
    alpha = jnp.exp2((m_old - m_new) * EXP2_SCALE)
    pv = _dot(_softmax_weights(chunks, m_new), v_aug)
    acc = acc_scr[idx]
    acc_scr[idx] = jnp.concatenate([alpha * a for a in _lane_chunks(acc)], axis=1) + pv
    m_scr[idx] = m_new


def _with_ones(v):
    return jnp.concatenate([v, jnp.ones(v.shape, v.dtype)], axis=1)


def _block_onehot(k0, tk, block, width):
    pos = k0 + lax.broadcasted_iota(jnp.int32, (tk, 1), 0)
    blk = lax.shift_right_logical(pos, block.bit_length() - 1) & (width - 1)
    return (blk == lax.broadcasted_iota(jnp.int32, (1, width), 1)).astype(BF16)


def _nsa_slc_kernel(q_ref, ks_ref, vs_ref, kw0_ref, kw1_ref, kw2_ref, vw0_ref, vw1_ref, vw2_ref,
                    sel_ref, gate_ref, ocmp_ref, o_ref, ka_scr, va_scr, qa_scr, m_scr, acc_scr, sa_scr, sb_scr,
                    ow_scr):
    i = pl.program_id(1)
    tq = q_ref.shape[0]
    nsb = sel_ref.shape[2]
    s_len = ks_ref.shape[0]
    tk = min(SLC_KEY_TILE, s_len)
    n_kt = s_len // tk
    width = min(nsb, LANE)
    nhalf = max(nsb // LANE, 1)
    kt_per_half = (LANE * SLC_BLOCK) // tk
    t = i * tq + lax.broadcasted_iota(jnp.int32, (tq, 1), 0)
    kt_diag = (i * tq) // tk

    @pl.when(i == 0)
    def _():
        def build(n, carry):
            k0 = pl.multiple_of(n * tk, tk)
            ka_scr[pl.ds(k0, tk), :] = jnp.concatenate(
                [ks_ref[pl.ds(k0, tk), :], _block_onehot(k0, tk, SLC_BLOCK, width)], axis=1)
            va_scr[pl.ds(k0, tk), :] = _with_ones(vs_ref[pl.ds(k0, tk), :])
            return carry

        lax.fori_loop(0, n_kt, build, 0)
        ka_scr[pl.ds(s_len, tk), :] = jnp.concatenate(
            [jnp.zeros((tk, HEAD_DIM), BF16), jnp.ones((tk, width), BF16)], axis=1)
        va_scr[pl.ds(s_len, tk), :] = jnp.zeros((tk, 2 * HEAD_DIM), BF16)

    bias = ((1.0 - sel_ref[0].astype(F32)) * NEG_INF).astype(BF16)
    for half in range(nhalf):
        for r in range(NSA_REP):
            qa_scr[half, r * tq:(r + 1) * tq, :] = jnp.concatenate(
                [q_ref[:, r * HEAD_DIM:(r + 1) * HEAD_DIM], bias[:, half * width:(half + 1) * width]], axis=1)
    t_all = jnp.concatenate([t] * NSA_REP, axis=0)

    def scores(kt):
        half = jnp.where(kt < n_kt, kt // kt_per_half, 0)
        return _dot_nt(qa_scr[half], ka_scr[pl.ds(pl.multiple_of(kt * tk, tk), tk), :])

    def values(kt):
        return va_scr[pl.ds(pl.multiple_of(kt * tk, tk), tk), :]

    kwin = jnp.concatenate([kw0_ref[...], kw1_ref[...], kw2_ref[...]], axis=0)
    vwin = _with_ones(jnp.concatenate([vw0_ref[...], vw1_ref[...], vw2_ref[...]], axis=0))
    nw = kwin.shape[0]
    wpos = (i - 2) * tq + lax.broadcasted_iota(jnp.int32, (1, nw), 1)
    wvalid = (wpos <= t) & (wpos > t - WINDOW) & (wpos >= 0)
    s = _dot_nt(qa_scr[0, :, :HEAD_DIM], kwin)
    s = jnp.concatenate([jnp.where(wvalid, s[r * tq:(r + 1) * tq], NEG_INF) for r in range(NSA_REP)], axis=0)
    chunks = _lane_chunks(s)
    ow = _dot(_softmax_weights(chunks, _row_max(chunks)), vwin)
    o_win = ow[:, :HEAD_DIM] / ow[:, HEAD_DIM:]
    gates = jax.nn.sigmoid(gate_ref[...])
    for r in range(NSA_REP):
        rows = slice(r * tq, (r + 1) * tq)
        ow_scr[rows, :] = (ocmp_ref[:, r * HEAD_DIM:(r + 1) * HEAD_DIM]
                           + gates[:, 3 * r + 2:3 * r + 3] * o_win[rows])

    kpos = kt_diag * tk + lax.broadcasted_iota(jnp.int32, (1, tk), 1)
    _flash_init(jnp.where(kpos <= t_all, scores(kt_diag), NEG_INF), values(kt_diag), m_scr, acc_scr, Ellipsis)

    last = jnp.maximum(kt_diag - 1, 0)
    sa_scr[...] = scores(0)

    def tile_pair(j, carry):
        kt_b = jnp.where(2 * j + 1 < kt_diag, 2 * j + 1, n_kt)
        sb_scr[...] = scores(kt_b)
        _flash_step(sa_scr[...], values(2 * j), m_scr, acc_scr, Ellipsis)
        sa_scr[...] = scores(jnp.minimum(2 * j + 2, last))
        _flash_step(sb_scr[...], values(kt_b), m_scr, acc_scr, Ellipsis)
        return carry

    lax.fori_loop(0, (kt_diag + 1) // 2, tile_pair, 0)

    acc = acc_scr[...]
    o_slc = acc[:, :HEAD_DIM] / acc[:, HEAD_DIM:]
    gates = jax.nn.sigmoid(gate_ref[...])
    for r in range(NSA_REP):
        rows = slice(r * tq, (r + 1) * tq)
        o = ow_scr[rows, :] + gates[:, 3 * r + 1:3 * r + 2] * o_slc[rows]
        o_ref[:, r * HEAD_DIM:(r + 1) * HEAD_DIM] = o.astype(BF16)


def _nsa_slc(proj, sel, gates, ocmp):
    s = proj.shape[0]
    nsb = sel.shape[2]
    tq = min(Q_TILE, s)
    tk = min(SLC_KEY_TILE, s)
    gw = NSA_REP * HEAD_DIM
    assert WINDOW == 2 * tq and (nsb <= LANE or nsb % LANE == 0)

    def win_spec(col, d):
        return pl.BlockSpec((tq, HEAD_DIM), lambda g, i: (jnp.maximum(i - 2 + d, 0), col + g))

    return pl.pallas_call(
        _nsa_slc_kernel,
        grid=(N_KV_NSA, s // tq),
        in_specs=[
            pl.BlockSpec((tq, gw), lambda g, i: (i, g)),
            pl.BlockSpec((s, HEAD_DIM), lambda g, i: (0, COL_KS + g)),
            pl.BlockSpec((s, HEAD_DIM), lambda g, i: (0, COL_VS + g)),
            win_spec(COL_KW, 0), win_spec(COL_KW, 1), win_spec(COL_KW, 2),
            win_spec(COL_VW, 0), win_spec(COL_VW, 1), win_spec(COL_VW, 2),
            pl.BlockSpec((1, tq, nsb), lambda g, i: (g, i, 0)),
            pl.BlockSpec((tq, LANE), lambda g, i: (i, g)),
            pl.BlockSpec((tq, gw), lambda g, i: (i, g)),
        ],
        out_specs=pl.BlockSpec((tq, gw), lambda g, i: (i, g)),
        out_shape=jax.ShapeDtypeStruct((s, NSA_WIDTH), BF16),
        scratch_shapes=[pltpu.VMEM((s + tk, HEAD_DIM + min(nsb, LANE)), BF16),
                        pltpu.VMEM((s + tk, 2 * HEAD_DIM), BF16),
                        pltpu.VMEM((max(nsb // LANE, 1), NSA_REP * tq, HEAD_DIM + min(nsb, LANE)), BF16),
                        pltpu.VMEM((NSA_REP * tq, LANE), F32),
                        pltpu.VMEM((NSA_REP * tq, 2 * HEAD_DIM), F32),
                        pltpu.VMEM((NSA_REP * tq, tk), F32), pltpu.VMEM((NSA_REP * tq, tk), F32),
                        pltpu.VMEM((NSA_REP * tq, HEAD_DIM), F32)],
        compiler_params=_cparams(("arbitrary", "arbitrary")),
        name="nsa_slc_win",
    )(proj, proj, proj, proj, proj, proj, proj, proj, proj, sel, gates, ocmp)


def _moba_kernel(q_ref, k_ref, v_ref, o_ref, ka_scr, va_scr, qa_scr, km_scr, m_scr, acc_scr, sa_scr, sb_scr):
    i = pl.program_id(1)
    tq = q_ref.shape[0]
    s_len = k_ref.shape[0]
    nbp = km_scr.shape[0]
    tk = min(MOBA_KEY_TILE, tq)
    bpt = tk // MOBA_BLOCK
    n_kt = s_len // tk

    @pl.when(i == 0)
    def _():
        km_scr[...] = jnp.zeros(km_scr.shape, F32)

        def build(n, carry):
            k0 = pl.multiple_of(n * tk, tk)
            kt = k_ref[pl.ds(k0, tk), :]
            ka_scr[pl.ds(k0, tk), :] = jnp.concatenate([kt, _block_onehot(k0, tk, MOBA_BLOCK, nbp)], axis=1)
            va_scr[pl.ds(k0, tk), :] = _with_ones(v_ref[pl.ds(k0, tk), :])
            for j in range(bpt):
                kb = kt[j * MOBA_BLOCK:(j + 1) * MOBA_BLOCK].astype(F32)
                km_scr[pl.ds(n * bpt + j, 1), :] = jnp.mean(kb, axis=0, keepdims=True)
            return carry

        lax.fori_loop(0, n_kt, build, 0)
        ka_scr[pl.ds(s_len, tk), :] = jnp.concatenate(
            [jnp.zeros((tk, HEAD_DIM), BF16), jnp.ones((tk, nbp), BF16)], axis=1)
        va_scr[pl.ds(s_len, tk), :] = jnp.zeros((tk, 2 * HEAD_DIM), BF16)

    q = q_ref[...]
    km_hi, km_lo = _split_bf16(km_scr[...])
    gate = _dot_nt(km_hi, q) + _dot_nt(km_lo, q)
    t = i * tq + lax.broadcasted_iota(jnp.int32, (tq, 1), 0)
    own = lax.shift_right_logical(i * tq + lax.broadcasted_iota(jnp.int32, (1, tq), 1), MOBA_BLOCK.bit_length() - 1)
    nidx = lax.broadcasted_iota(jnp.int32, (nbp, 1), 0)
    nf = nidx.astype(F32)
    work = jnp.where(nidx < own, gate, LOWEST)
    for _ in range(MOBA_TOPK):
        m = jnp.max(work, axis=0, keepdims=True)
        first = jnp.min(jnp.where((work == m) & (m > 0.5 * LOWEST), nf, 1e9), axis=0, keepdims=True)
        work = jnp.where(nf == first, PICKED, work)
    bias = jnp.where((work == PICKED) | (nidx == own), 0.0, NEG_INF)
    qa_scr[...] = jnp.concatenate([q, bias.T.astype(BF16)], axis=1)

    def scores(kt):
        return _dot_nt(qa_scr[...], ka_scr[pl.ds(pl.multiple_of(kt * tk, tk), tk), :])

    def values(kt):
        return va_scr[pl.ds(pl.multiple_of(kt * tk, tk), tk), :]

    d0 = pl.multiple_of(i * tq, tq)
    for c in range(tq // tk):
        rows = slice(c * tk, (c + 1) * tk)
        nk = (c + 1) * tk
        kpos = d0 + lax.broadcasted_iota(jnp.int32, (1, nk), 1)
        s_diag = _dot_nt(qa_scr[rows, :], ka_scr[pl.ds(d0, nk), :])
        _flash_init(jnp.where(kpos <= t[rows], s_diag, NEG_INF), va_scr[pl.ds(d0, nk), :], m_scr, acc_scr, rows)

    n_past = i * (tq // tk)
    last = jnp.maximum(n_past - 1, 0)
    sa_scr[...] = scores(0)

    def tile_pair(j, carry):
        kt_b = jnp.where(2 * j + 1 < n_past, 2 * j + 1, n_kt)
        sb_scr[...] = scores(kt_b)
        _flash_step(sa_scr[...], values(2 * j), m_scr, acc_scr, Ellipsis)
        sa_scr[...] = scores(jnp.minimum(2 * j + 2, last))
        _flash_step(sb_scr[...], values(kt_b), m_scr, acc_scr, Ellipsis)
        return carry

    lax.fori_loop(0, (n_past + 1) // 2, tile_pair, 0)
    acc = acc_scr[...]
    o_ref[...] = (acc[:, :HEAD_DIM] / acc[:, HEAD_DIM:]).astype(BF16)


def _moba(proj):
    s = proj.shape[0]
    tq = min(MOBA_Q_TILE, s)
    tk = min(MOBA_KEY_TILE, tq)
    nbp = max(LANE, s // MOBA_BLOCK)
    assert nbp & (nbp - 1) == 0 and tq % tk == 0
    return pl.pallas_call(
        _moba_kernel,
        grid=(N_HEADS_MOBA, s // tq),
        in_specs=[
            pl.BlockSpec((tq, HEAD_DIM), lambda h, i: (i, COL_QM + h)),
            pl.BlockSpec((s, HEAD_DIM), lambda h, i: (0, COL_KM + h)),
            pl.BlockSpec((s, HEAD_DIM), lambda h, i: (0, COL_VM + h)),
        ],
        out_specs=pl.BlockSpec((tq, HEAD_DIM), lambda h, i: (i, h)),
        out_shape=jax.ShapeDtypeStruct((s, MOBA_WIDTH), BF16),
        scratch_shapes=[pltpu.VMEM((s + tk, HEAD_DIM + nbp), BF16), pltpu.VMEM((s + tk, 2 * HEAD_DIM), BF16),
                        pltpu.VMEM((tq, HEAD_DIM + nbp), BF16),
                        pltpu.VMEM((nbp, HEAD_DIM), F32), pltpu.VMEM((tq, LANE), F32),
                        pltpu.VMEM((tq, 2 * HEAD_DIM), F32),
                        pltpu.VMEM((tq, tk), F32), pltpu.VMEM((tq, tk), F32)],
        compiler_params=_cparams(("arbitrary", "arbitrary")),
        name="moba",
    )(proj, proj, proj)


def _out_kernel(on_ref, om_ref, x_ref, wn_ref, wm_ref, gt_ref, g_ref, sc_ref, sh_ref, wrh_ref, wrl_ref,
                x1_ref, h_ref, lg_ref):
    mix = _dot(on_ref[...], wn_ref[...]) + _dot(om_ref[...], wm_ref[...])
    x1 = x_ref[...] + gt_ref[...] * mix
    x1_ref[...] = x1
    h = _rms_mod(x1, g_ref[...], sc_ref[...], sh_ref[...])
    h_hi, h_lo = _split_bf16(h)
    h_ref[...] = h
    wrh = wrh_ref[...]
    lg_ref[...] = _dot(h_hi, wrh) + _dot(h_lo, wrh) + _dot(h_hi, wrl_ref[...])


def _out_proj(o_nsa, o_moba, x, w_n, w_m, gt, g, sc, sh, wr_hi, wr_lo):
    s, d = x.shape
    tm = min(OUT_TM, s)
    row = lambda i: (i, 0)
    fix = lambda i: (0, 0)
    return pl.pallas_call(
        _out_kernel,
        grid=(s // tm,),
        in_specs=[
            pl.BlockSpec((tm, NSA_WIDTH), row),
            pl.BlockSpec((tm, MOBA_WIDTH), row),
            pl.BlockSpec((tm, d), row),
            pl.BlockSpec((NSA_WIDTH, d), fix),
            pl.BlockSpec((MOBA_WIDTH, d), fix),
            pl.BlockSpec((1, d), fix),
            pl.BlockSpec((1, d), fix),
            pl.BlockSpec((1, d), fix),
            pl.BlockSpec((1, d), fix),
            pl.BlockSpec((d, LANE), fix),
            pl.BlockSpec((d, LANE), fix),
        ],
        out_specs=[pl.BlockSpec((tm, d), row), pl.BlockSpec((tm, d), row), pl.BlockSpec((tm, LANE), row)],
        out_shape=[jax.ShapeDtypeStruct((s, d), F32), jax.ShapeDtypeStruct((s, d), F32),
                   jax.ShapeDtypeStruct((s, LANE), F32)],
        compiler_params=_cparams(("arbitrary",)),
        name="out_proj_norm_router",
    )(o_nsa, o_moba, x, w_n, w_m, gt, g, sc, sh, wr_hi, wr_lo)


def _row_copy(src_ref, dst_ref, src_row, dst_row, sem):
    return pltpu.make_async_copy(src_ref.at[pl.ds(src_row, 1)], dst_ref.at[pl.ds(dst_row, 1)], sem)


def _expert_kernel(be_ref, nu_ref, tok_ref, h_ref, wg_ref, wu_ref, wd_ref, y_ref,
                   x_buf, wg_s, wu_s, wd_s, sems):
    b = pl.program_id(0)
    n_used = nu_ref[0]

    def row_copy(target, r):
        blk = jnp.minimum(target, n_used - 1)
        buf = target % EXPERT_BUFFERS
        return _row_copy(h_ref, x_buf.at[buf], tok_ref[blk * MOE_ROWS + r], r, sems.at[buf])

    def gather(target, wait):
        def row(r, carry):
            if wait:
                row_copy(target, r).wait()
            else:
                row_copy(target, r).start()
            return carry

        lax.fori_loop(0, MOE_ROWS, row, 0, unroll=DMA_UNROLL)

    @pl.when((b == 0) & (n_used > 0))
    def _():
        gather(0, False)
        gather(1, False)

    prev = be_ref[jnp.maximum(b - 1, 0)]

    @pl.when((b == 0) | (be_ref[b] != prev))
    def _():
        wg_s[...] = wg_ref[0, 0].astype(BF16)
        wu_s[...] = wu_ref[0, 0].astype(BF16)
        wd_s[...] = wd_ref[0, 0].astype(BF16)

    @pl.when(b < n_used)
    def _():
        gather(b, True)
        for r in range(MOE_ROWS):
            row_copy(b + EXPERT_BUFFERS - 1, r).start()
        x = x_buf[b % EXPERT_BUFFERS].astype(BF16)
        a = _dot(x, wg_s[...])
        u = _dot(x, wu_s[...])
        hid = (a * jax.nn.sigmoid(a) * u).astype(BF16)
        y_ref[...] = _dot(hid, wd_s[...])

    @pl.when((b < n_used) & (b + EXPERT_BUFFERS - 1 >= n_used))
    def _():
        gather(b + EXPERT_BUFFERS - 1, True)

    @pl.when((b == 0) & (n_used == 1))
    def _():
        gather(1, True)

    @pl.when(b >= n_used)
    def _():
        y_ref[...] = jnp.zeros(y_ref.shape, F32)


def _experts(h, slot_tok, blk_exp, n_used, w_gate, w_up, w_down, layer):
    n_slots = slot_tok.shape[0]
    d = h.shape[1]
    n_blk = n_slots // MOE_ROWS
    grid_spec = pltpu.PrefetchScalarGridSpec(
        num_scalar_prefetch=3,
        grid=(n_blk,),
        in_specs=[
            pl.BlockSpec(memory_space=pl.ANY),
            pl.BlockSpec((1, 1, d, D_EXPERT), lambda b, be, nu, tok: (layer, be[b], 0, 0)),
            pl.BlockSpec((1, 1, d, D_EXPERT), lambda b, be, nu, tok: (layer, be[b], 0, 0)),
            pl.BlockSpec((1, 1, D_EXPERT, d), lambda b, be, nu, tok: (layer, be[b], 0, 0)),
        ],
        out_specs=pl.BlockSpec((MOE_ROWS, d), lambda b, be, nu, tok: (b, 0)),
        scratch_shapes=[pltpu.VMEM((EXPERT_BUFFERS, MOE_ROWS, d), F32),
                        pltpu.VMEM((d, D_EXPERT), BF16), pltpu.VMEM((d, D_EXPERT), BF16),
                        pltpu.VMEM((D_EXPERT, d), BF16), pltpu.SemaphoreType.DMA((EXPERT_BUFFERS,))],
    )
    return pl.pallas_call(
        _expert_kernel,
        grid_spec=grid_spec,
        out_shape=jax.ShapeDtypeStruct((n_slots, d), F32),
        compiler_params=_cparams(("arbitrary",)),
        name="moe_experts",
    )(blk_exp, n_used, slot_tok, h, w_gate, w_up, w_down)


def _combine_kernel(sa_ref, sb_ref, na_ref, nb_ref, x_ref, w_ref, gt_ref, gf_ref, y_ref, o_ref, buf, sems, *,
                    final):
    b = pl.program_id(0)
    tt = x_ref.shape[0]

    def gather(ia_ref, ib_ref, slot, wait):
        def row(r, carry):
            for k, idx_ref in enumerate((ia_ref, ib_ref)):
                cp = _row_copy(y_ref, buf.at[slot].at[k], idx_ref[0, 0, r], r, sems.at[slot].at[k])
                if wait:
                    cp.wait()
                else:
                    cp.start()
            return carry

        lax.fori_loop(0, tt, row, 0, unroll=DMA_UNROLL)

    @pl.when(b == 0)
    def _():
        gather(sa_ref, sb_ref, 0, False)

    @pl.when(b + 1 < pl.num_programs(0))
    def _():
        gather(na_ref, nb_ref, (b + 1) % 2, False)

    gather(sa_ref, sb_ref, b % 2, True)
    w = w_ref[...]
    x2 = x_ref[...] + gt_ref[...] * (w[:, 0:1] * buf[b % 2, 0] + w[:, 1:2] * buf[b % 2, 1])
    if final:
        x2 = x2 * lax.rsqrt(jnp.mean(x2 * x2, axis=-1, keepdims=True) + RMS_EPS) * gf_ref[...]
    o_ref[...] = x2


def _combine(x1, ys, slot_a, slot_b, pair_w, gt, g_final, final):
    s, d = x1.shape
    tt = min(GATHER_ROWS, s)
    nt = s // tt
    cur_spec = pl.BlockSpec((1, 1, tt), lambda b: (b, 0, 0), memory_space=pltpu.SMEM)
    nxt_spec = pl.BlockSpec((1, 1, tt), lambda b: (jnp.minimum(b + 1, nt - 1), 0, 0), memory_space=pltpu.SMEM)
    slot_a = slot_a.reshape(nt, 1, tt)
    slot_b = slot_b.reshape(nt, 1, tt)
    return pl.pallas_call(
        functools.partial(_combine_kernel, final=final),
        grid=(nt,),
        in_specs=[
            cur_spec, cur_spec, nxt_spec, nxt_spec,
            pl.BlockSpec((tt, d), lambda b: (b, 0)),
            pl.BlockSpec((tt, LANE), lambda b: (b, 0)),
            pl.BlockSpec((1, d), lambda b: (0, 0)),
            pl.BlockSpec((1, d), lambda b: (0, 0)),
            pl.BlockSpec(memory_space=pl.ANY),
        ],
        out_specs=pl.BlockSpec((tt, d), lambda b: (b, 0)),
        out_shape=jax.ShapeDtypeStruct((s, d), F32),
        scratch_shapes=[pltpu.VMEM((2, 2, tt, d), F32), pltpu.SemaphoreType.DMA((2, 2))],
        compiler_params=_cparams(("arbitrary",)),
        name="moe_combine",
    )(slot_a, slot_b, slot_a, slot_b, x1, jnp.pad(pair_w, ((0, 0), (0, LANE - TOP_K_EXPERTS))), gt, g_final, ys)


def _route(logits, b_router):
    n = logits.shape[0]
    scores = jax.nn.sigmoid(logits[:, :N_EXPERTS])
    biased = (scores + b_router.astype(F32)).reshape(n, N_GROUPS, EXPERTS_PER_GROUP)

    def top2(v):
        idx = lax.broadcasted_iota(jnp.int32, v.shape, v.ndim - 1)
        v1 = jnp.max(v, axis=-1, keepdims=True)
        i1 = jnp.min(jnp.where(v == v1, idx, v.shape[-1]), axis=-1, keepdims=True)
        rest = jnp.where(idx == i1, -jnp.inf, v)
        v2 = jnp.max(rest, axis=-1, keepdims=True)
        i2 = jnp.min(jnp.where(rest == v2, idx, v.shape[-1]), axis=-1, keepdims=True)
        return v1, v2, i1, i2

    g1, g2, _, _ = top2(biased)
    g_sum = (g1 + g2)[..., 0]
    g_idx = lax.broadcasted_iota(jnp.int32, g_sum.shape, 1)
    grp = jnp.min(jnp.where(g_sum == jnp.max(g_sum, axis=-1, keepdims=True), g_idx, N_GROUPS), axis=-1)
    grp_hot = grp[:, None] == jnp.arange(N_GROUPS, dtype=grp.dtype)[None, :]
    in_grp = jnp.sum(jnp.where(grp_hot[:, :, None], biased, 0.0), axis=1)
    _, _, l1, l2 = top2(in_grp)
    expert = (grp[:, None] * EXPERTS_PER_GROUP + jnp.concatenate([l1, l2], axis=1)).astype(jnp.int32)
    e_hot = expert[:, :, None] == jnp.arange(N_EXPERTS, dtype=jnp.int32)[None, None, :]
    wsel = jnp.sum(jnp.where(e_hot, scores[:, None, :], 0.0), axis=-1)
    wsel = wsel / jnp.sum(wsel, axis=-1, keepdims=True)

    n_pair = n * TOP_K_EXPERTS
    flat_e = expert.reshape(-1)
    flat_t = jnp.repeat(jnp.arange(n, dtype=jnp.int32), TOP_K_EXPERTS)
    hot = flat_e[:, None] == jnp.arange(N_EXPERTS, dtype=jnp.int32)[None, :]
    onehot = hot.astype(jnp.int32)
    chunk = min(LANE, n_pair)
    n_chunk = n_pair // chunk
    tri = (jnp.arange(chunk)[:, None] >= jnp.arange(chunk)[None, :]).astype(BF16)
    within = jnp.einsum('ij,cjk->cik', tri, hot.astype(BF16).reshape(n_chunk, chunk, N_EXPERTS),
                        preferred_element_type=F32)
    totals = within[:, -1, :]
    before = jnp.cumsum(totals, axis=0) - totals
    running = (within + before[:, None, :]).astype(jnp.int32).reshape(n_pair, N_EXPERTS)
    rank = jnp.sum(running * onehot, axis=1) - 1
    counts = running[-1]
    padded = (counts + MOE_ROWS - 1) // MOE_ROWS * MOE_ROWS
    pend = jnp.cumsum(padded)
    pstart = pend - padded
    slot = (jnp.sum(pstart[None, :] * onehot, axis=1) + rank).astype(jnp.int32)
    n_slots = n_pair + N_EXPERTS * MOE_ROWS
    slot_tok = jnp.zeros((n_slots,), jnp.int32).at[slot].set(flat_t)
    pair_slot = slot.reshape(n, TOP_K_EXPERTS)
    n_blk = n_slots // MOE_ROWS
    blk_start = jnp.arange(n_blk, dtype=jnp.int32) * MOE_ROWS
    blk_exp = jnp.minimum(jnp.sum((pend[None, :] <= blk_start[:, None]).astype(jnp.int32), axis=1), N_EXPERTS - 1)
    n_used = (pend[-1:] // MOE_ROWS).astype(jnp.int32)
    return slot_tok, wsel, pair_slot, blk_exp, n_used


def _rope_tables(pos):
    inv = 1.0 / (ROPE_THETA ** (jnp.arange(0, HEAD_DIM, 2, dtype=F32) / HEAD_DIM))
    ang = pos.astype(F32)[:, None] * inv[None, :]
    cos = jnp.cos(ang)
    sin = jnp.sin(ang)
    return jnp.concatenate([cos, cos], axis=-1), jnp.concatenate([-sin, sin], axis=-1)


def _layout_w_in(w_in):
    o_g = NSA_WIDTH + 6 * NSA_KV_WIDTH
    main = jnp.concatenate([w_in[:, :o_g], w_in[:, o_g + N_GATES:]], axis=1).astype(BF16)
    wg = w_in[:, o_g:o_g + N_GATES].reshape(-1, N_KV_NSA, 3 * NSA_REP)
    wg = jnp.pad(wg, ((0, 0), (0, 0), (0, LANE - 3 * NSA_REP))).reshape(-1, N_KV_NSA * LANE)
    return main, wg.astype(BF16)


def _overlap_t(ncp, nsb):
    cs = jnp.arange(ncp, dtype=jnp.int32)[None, :] * CMP_STRIDE
    ss = jnp.arange(nsb, dtype=jnp.int32)[:, None] * SLC_BLOCK
    return ((cs + CMP_LEN - 1 >= ss) & (cs < ss + SLC_BLOCK)).astype(BF16)


def kernel(x, c, w_ada, b_ada, g_attn, g_ffn, w_in, w_out, cmp_pe, cmp_w1, cmp_b1, cmp_w2,
           w_router, b_router, w_gate, w_up, w_down, g_final):
    b, s, d = x.shape
    assert b == 1 and d == D_MODEL and s % max(Q_TILE, SLC_KEY_TILE, MOBA_Q_TILE) == 0
    depth = w_ada.shape[0]
    ncp = s // CMP_STRIDE
    nsb = s // SLC_BLOCK

    cos, sin = _rope_tables(jnp.arange(s))
    cos_c, sin_c = _rope_tables(jnp.arange(ncp) * CMP_STRIDE + CMP_LEN - 1)
    overlap_t = _overlap_t(ncp, nsb)
    flags = jnp.asarray(ROPE_FLAGS, jnp.int32)
    wr = jnp.pad(w_router.astype(F32), ((0, 0), (0, LANE - N_EXPERTS)))
    wr_hi, wr_lo = _split_bf16(wr)

    mod = _ada_mod(c.astype(F32), w_ada, b_ada)
    xs = x.reshape(s, d)
    for l in range(depth):
        sh1, sc1, gt1, sh2, sc2, gt2 = [mod[l, :, k * d:(k + 1) * d] for k in range(6)]
        w_main, w_gates = _layout_w_in(w_in[l])
        proj, gates = _project(xs, g_attn[l].reshape(1, d), sc1, sh1, w_main, w_gates, cos, sin, flags)

        chunks = proj[:, COL_KC * LANE:COL_KS * LANE].reshape(s, 2 * N_KV_NSA, HEAD_DIM)
        chunks = chunks.transpose(1, 0, 2).reshape(2 * N_KV_NSA, ncp, CMP_STRIDE * HEAD_DIM)
        pe8 = jnp.pad(cmp_pe[l].reshape(2, 1, CMP_LEN * HEAD_DIM), ((0, 0), (0, 7), (0, 0)))
        cmp_kv = _compress(chunks, cmp_w1[l], cmp_b1[l].reshape(2, 1, CMP_HIDDEN), cmp_w2[l], pe8, cos_c, sin_c)

        ocmp, sel = _nsa_cmp(proj, cmp_kv, overlap_t, gates)
        o_nsa = _nsa_slc(proj, sel, gates, ocmp)
        o_moba = _moba(proj)

        w_o = w_out[l].astype(BF16)
        x1, h2, logits = _out_proj(o_nsa, o_moba, xs, w_o[:NSA_WIDTH], w_o[NSA_WIDTH:], gt1,
                                   g_ffn[l].reshape(1, d), sc2, sh2, wr_hi, wr_lo)

        slot_tok, pair_w, pair_slot, blk_exp, n_used = _route(logits, b_router)
        ys = _experts(h2, slot_tok, blk_exp, n_used, w_gate, w_up, w_down, l)
        xs = _combine(x1, ys, pair_slot[:, 0], pair_slot[:, 1], pair_w, gt2, g_final.reshape(1, d),
                      l == depth - 1)
    return xs.reshape(b, s, d)
```
